```python
import math
import jax, jax.numpy as jnp
from jax import lax
import numpy as np

D_MODEL = 1024
BATCH = 8
SEQ = 4096
DEPTH = 1

MLA_HEADS = 8
MLA_NOPE = 128
MLA_ROPE = 64
MLA_V = 128
Q_LORA = 384
KV_LORA = 256
ROPE_THETA = 10000.0
SWA_HEADS = 8
SWA_KV_HEADS = 2
SWA_HEAD_DIM = 128
WINDOW = 128
BLOCK = 128
N_BUCKETS = 32
MAX_DISTANCE = 128
D_FF = 4 * D_MODEL
N_BRANCHES = 2
N_MOD = 6
EPS = 1e-6
NEG_INF = -1e30

SPLITS = (Q_LORA, KV_LORA, MLA_ROPE,
          SWA_HEADS * SWA_HEAD_DIM, SWA_KV_HEADS * SWA_HEAD_DIM, SWA_KV_HEADS * SWA_HEAD_DIM,
          N_BRANCHES * D_MODEL)
D_IN = sum(SPLITS)

kernel_name = "hybrid_mla_swa_gated_encoder_block"


def rmsnorm(x, g):
    xf = x.astype(jnp.float32)
    y = xf * lax.rsqrt(jnp.mean(xf * xf, axis=-1, keepdims=True) + EPS) * g.astype(jnp.float32)
    return y.astype(x.dtype)


def rope_angles(pos, dim):
    inv = ROPE_THETA ** (-jnp.arange(0, dim, 2, dtype=jnp.float32) / dim)
    ang = pos.astype(jnp.float32)[..., None] * inv
    return jnp.cos(ang), jnp.sin(ang)


def apply_rope(x, cos, sin):
    half = x.shape[-1] // 2
    x1 = x[..., :half].astype(jnp.float32)
    x2 = x[..., half:].astype(jnp.float32)
    out = jnp.concatenate([x1 * cos - x2 * sin, x2 * cos + x1 * sin], axis=-1)
    return out.astype(x.dtype)


def t5_bucket(rel):
    half = N_BUCKETS // 2
    max_exact = half // 2
    ret = jnp.where(rel > 0, half, 0)
    n = jnp.abs(rel)
    nf = jnp.maximum(n, 1).astype(jnp.float32)
    large = max_exact + (jnp.log(nf / max_exact) / math.log(MAX_DISTANCE / max_exact)
                         * (half - max_exact)).astype(jnp.int32)
    large = jnp.minimum(large, half - 1)
    return ret + jnp.where(n < max_exact, n, large)


def mla_branch(cq, ckv, kr, pos, q_norm, w_uq, kv_norm, w_ukv):
    B, S, _ = cq.shape
    nb = S // BLOCK
    q = (rmsnorm(cq, q_norm) @ w_uq).reshape(B, S, MLA_HEADS, MLA_NOPE + MLA_ROPE)
    q_nope, q_rope = q[..., :MLA_NOPE], q[..., MLA_NOPE:]
    kv = (rmsnorm(ckv, kv_norm) @ w_ukv).reshape(B, S, MLA_HEADS, MLA_NOPE + MLA_V)
    k_nope, v = kv[..., :MLA_NOPE], kv[..., MLA_NOPE:]
    cos, sin = rope_angles(pos, MLA_ROPE)
    q_rope = apply_rope(q_rope, cos[:, :, None], sin[:, :, None])
    k_rope = apply_rope(kr, cos, sin)
    scale = (MLA_NOPE + MLA_ROPE) ** -0.5
    qn_b = q_nope.reshape(B, nb, BLOCK, MLA_HEADS, MLA_NOPE).transpose(1, 0, 2, 3, 4)
    qr_b = q_rope.reshape(B, nb, BLOCK, MLA_HEADS, MLA_ROPE).transpose(1, 0, 2, 3, 4)

    def attend(args):
        qn, qr = args
        s = (jnp.einsum('bqhd,bkhd->bhqk', qn, k_nope)
             + jnp.einsum('bqhr,bkr->bhqk', qr, k_rope))
        p = jax.nn.softmax(s.astype(jnp.float32) * scale, axis=-1).astype(v.dtype)
        return jnp.einsum('bhqk,bkhd->bqhd', p, v)

    o = lax.map(attend, (qn_b, qr_b))
    return o.transpose(1, 0, 2, 3, 4).reshape(B, S, MLA_HEADS * MLA_V)


def swa_branch(q, k, v, pos, rel_bias, sink):
    B, S, _ = q.shape
    nb = S // BLOCK
    G = SWA_HEADS // SWA_KV_HEADS
    span = BLOCK + 2 * WINDOW
    q = q.reshape(B, S, SWA_KV_HEADS, G, SWA_HEAD_DIM) * (SWA_HEAD_DIM ** -0.5)
    k = k.reshape(B, S, SWA_KV_HEADS, SWA_HEAD_DIM)
    v = v.reshape(B, S, SWA_KV_HEADS, SWA_HEAD_DIM)
    pad = ((0, 0), (WINDOW, WINDOW), (0, 0), (0, 0))
    kp = jnp.pad(k, pad)
    vp = jnp.pad(v, pad)
    posp = jnp.pad(pos, ((0, 0), (WINDOW, WINDOW)))
    validp = jnp.pad(jnp.ones((S,), dtype=bool), (WINDOW, WINDOW))
    q_b = q.reshape(B, nb, BLOCK, SWA_KV_HEADS, G, SWA_HEAD_DIM).transpose(1, 0, 2, 3, 4, 5)
    pq_b = pos.reshape(B, nb, BLOCK).transpose(1, 0, 2)
    starts = jnp.arange(nb, dtype=jnp.int32) * BLOCK
    bias_tab = rel_bias.astype(jnp.float32)
    sink_f = sink.astype(jnp.float32).reshape(1, SWA_KV_HEADS, G, 1, 1)

    def attend(args):
        qb, pqb, start = args
        kb = lax.dynamic_slice_in_dim(kp, start, span, axis=1)
        vb = lax.dynamic_slice_in_dim(vp, start, span, axis=1)
        pk = lax.dynamic_slice_in_dim(posp, start, span, axis=1)
        ok = lax.dynamic_slice_in_dim(validp, start, span, axis=0)
        rel = pk[:, None, :] - pqb[:, :, None]
        mask = ok[None, None, :] & (jnp.abs(rel) <= WINDOW)
        bias = bias_tab[t5_bucket(rel)]
        bias = bias.transpose(0, 3, 1, 2).reshape(B, SWA_KV_HEADS, G, BLOCK, span)
        s = jnp.einsum('bqngd,bknd->bngqk', qb, kb).astype(jnp.float32) + bias
        s = jnp.where(mask[:, None, None], s, NEG_INF)
        sk = jnp.broadcast_to(sink_f, (B, SWA_KV_HEADS, G, BLOCK, 1))
        p = jax.nn.softmax(jnp.concatenate([s, sk], axis=-1), axis=-1)[..., :-1]
        return jnp.einsum('bngqk,bknd->bqngd', p.astype(vb.dtype), vb)

    o = lax.map(attend, (q_b, pq_b, starts))
    return o.transpose(1, 0, 2, 3, 4, 5).reshape(B, S, SWA_HEADS * SWA_HEAD_DIM)


def setup_inputs(seed: int = 0) -> dict:
    key = jax.random.key(seed)
    ks = jax.random.split(key, 24)
    D = D_MODEL
    f32 = jnp.float32

    def w(k, shape, fan_in, gain=1.0):
        return jax.random.normal(k, shape, f32) * (gain * fan_in ** -0.5)

    def gain(k, shape):
        return 1.0 + 0.01 * jax.random.normal(k, shape, f32)

    x = jax.random.normal(ks[0], (BATCH, SEQ, D), f32)
    c = jax.random.normal(ks[1], (BATCH, D), f32)
    offset = jax.random.randint(ks[2], (BATCH, 1), 0, 1024, dtype=jnp.int32)
    positions = offset + jnp.arange(SEQ, dtype=jnp.int32)[None, :]
    return {
        "x": x,
        "c": c,
        "positions": positions,
        "w_ada": w(ks[3], (DEPTH, D, N_MOD * D), D, 0.1),
        "b_ada": 0.01 * jax.random.normal(ks[4], (DEPTH, N_MOD * D), f32),
        "norm_mix": gain(ks[5], (DEPTH, D)),
        "w_in": w(ks[6], (DEPTH, D, D_IN), D),
        "q_norm": gain(ks[7], (DEPTH, Q_LORA)),
        "w_uq": w(ks[8], (DEPTH, Q_LORA, MLA_HEADS * (MLA_NOPE + MLA_ROPE)), Q_LORA),
        "kv_norm": gain(ks[9], (DEPTH, KV_LORA)),
        "w_ukv": w(ks[10], (DEPTH, KV_LORA, MLA_HEADS * (MLA_NOPE + MLA_V)), KV_LORA),
        "rel_bias": 0.5 * jax.random.normal(ks[11], (N_BUCKETS, SWA_HEADS), f32),
        "sink": 0.5 * jax.random.normal(ks[12], (DEPTH, SWA_HEADS), f32),
        "w_o_mla": w(ks[13], (DEPTH, MLA_HEADS * MLA_V, D), MLA_HEADS * MLA_V),
        "w_o_swa": w(ks[14], (DEPTH, SWA_HEADS * SWA_HEAD_DIM, D), SWA_HEADS * SWA_HEAD_DIM),
        "w_out": w(ks[15], (DEPTH, D, D), D),
        "norm_mlp": gain(ks[16], (DEPTH, D)),
        "w_ff1": w(ks[17], (DEPTH, D, D_FF), D),
        "w_ff2": w(ks[18], (DEPTH, D_FF, D), D_FF),
        "norm_final": gain(ks[19], (D,)),
    }


def reference(x, c, positions, w_ada, b_ada, norm_mix, w_in, q_norm, w_uq, kv_norm, w_ukv,
              rel_bias, sink, w_o_mla, w_o_swa, w_out, norm_mlp, w_ff1, w_ff2, norm_final):
    B, S, D = x.shape
    split_idx = [int(i) for i in np.cumsum(SPLITS)[:-1]]
    c_act = jax.nn.silu(c)
    for l in range(DEPTH):
        mod = c_act @ w_ada[l] + b_ada[l]
        sh1, sc1, g1, sh2, sc2, g2 = jnp.split(mod, N_MOD, axis=-1)

        h = rmsnorm(x, norm_mix[l]) * (1.0 + sc1[:, None, :]) + sh1[:, None, :]
        proj = h @ w_in[l]
        cq, ckv, kr, qs, ks_, vs, gates = jnp.split(proj, split_idx, axis=-1)
        y_a = mla_branch(cq, ckv, kr, positions, q_norm[l], w_uq[l], kv_norm[l], w_ukv[l]) @ w_o_mla[l]
        y_b = swa_branch(qs, ks_, vs, positions, rel_bias, sink[l]) @ w_o_swa[l]
        gates = jax.nn.sigmoid(gates.astype(jnp.float32)).astype(x.dtype).reshape(B, S, N_BRANCHES, D)
        merged = gates[:, :, 0] * y_a + gates[:, :, 1] * y_b
        x = x + g1[:, None, :] * (merged @ w_out[l])

        h = rmsnorm(x, norm_mlp[l]) * (1.0 + sc2[:, None, :]) + sh2[:, None, :]
        ff = jnp.square(jax.nn.relu(h @ w_ff1[l])) @ w_ff2[l]
        x = x + g2[:, None, :] * ff
    return rmsnorm(x, norm_final)
```

```python
import functools
import math

import jax
import jax.numpy as jnp
from jax import lax
from jax.experimental import pallas as pl
from jax.experimental.pallas import tpu as pltpu

F32 = jnp.float32
BF16 = jnp.bfloat16

D_MODEL = 1024
MLA_HEADS = 8
MLA_NOPE = 128
MLA_ROPE = 64
MLA_V = 128
Q_LORA = 384
KV_LORA = 256
ROPE_THETA = 10000.0
SWA_HEADS = 8
SWA_KV_HEADS = 2
SWA_GROUP = SWA_HEADS // SWA_KV_HEADS
SWA_HEAD_DIM = 128
WINDOW = 128
BLOCK = 128
N_BUCKETS = 32
MAX_DISTANCE = 128
D_FF = 4 * D_MODEL
N_MOD = 6
EPS = 1e-6
NEG_INF = -1e30

LANES = 128
MLA_QK_PAD = 256
VMEM_LIMIT = 56 * 1024 * 1024

MLA_Q_SCALE = (MLA_NOPE + MLA_ROPE) ** -0.5 * math.log2(math.e)
SWA_Q_SCALE = SWA_HEAD_DIM ** -0.5

C_CQ = 0
C_CKV = C_CQ + Q_LORA
C_KR = C_CKV + KV_LORA
C_QS = C_KR + LANES
C_KS = C_QS + SWA_HEADS * SWA_HEAD_DIM
C_VS = C_KS + SWA_KV_HEADS * SWA_HEAD_DIM
C_G = C_VS + SWA_KV_HEADS * SWA_HEAD_DIM
C_END = C_G + 2 * D_MODEL

T5_LARGE_THRESHOLDS = (12, 16, 23, 32, 46, 64, 91)

NT_DIMS = (((1,), (1,)), ((), ()))


def _resident(shape):
    nd = len(shape)
    return pl.BlockSpec(shape, lambda *_: (0,) * nd, pipeline_mode=pl.Buffered(1))


def _params(semantics):
    return pltpu.CompilerParams(dimension_semantics=semantics, vmem_limit_bytes=VMEM_LIMIT)


def _rms(x):
    return x * lax.rsqrt(jnp.mean(x * x, axis=-1, keepdims=True) + EPS)


def _ada_kernel(c_ref, w_ref, b_ref, o_ref):
    c = c_ref[...]
    c_act = c * jax.nn.sigmoid(c)
    o_ref[...] = jnp.dot(c_act, w_ref[...], preferred_element_type=F32,
                         precision=lax.Precision.HIGHEST) + b_ref[...]


def _ada_mod(c, w_ada, b_ada):
    B, D = c.shape
    N = w_ada.shape[1]
    tn = 1024
    return pl.pallas_call(
        _ada_kernel,
        out_shape=jax.ShapeDtypeStruct((B, N), F32),
        grid=(N // tn,),
        in_specs=[pl.BlockSpec((B, D), lambda j: (0, 0)),
                  pl.BlockSpec((D, tn), lambda j: (0, j)),
                  pl.BlockSpec((1, tn), lambda j: (0, j))],
        out_specs=pl.BlockSpec((B, tn), lambda j: (0, j)),
        compiler_params=_params(("arbitrary",)),
        name="ada_mod",
    )(c, w_ada, b_ada.reshape(1, N))


def _rope_table_kernel(pos_ref, inv_ref, o_ref):
    ang = pos_ref[0].astype(F32) * inv_ref[...]
    cos = jnp.cos(ang)
    sin = jnp.sin(ang)
    lane = lax.broadcasted_iota(jnp.int32, ang.shape, 1)
    o_ref[0] = jnp.where(lane < 64, cos, jnp.where(lane < 96, -sin, sin))


def _rope_table(positions):
    B, S = positions.shape
    tr = 1024
    inv = ROPE_THETA ** (-jnp.arange(0, MLA_ROPE, 2, dtype=F32) / MLA_ROPE)
    inv4 = jnp.tile(inv, 4).reshape(1, LANES)
    return pl.pallas_call(
        _rope_table_kernel,
        out_shape=jax.ShapeDtypeStruct((B, S, LANES), F32),
        grid=(B, S // tr),
        in_specs=[pl.BlockSpec((1, tr, 1), lambda b, i: (b, i, 0)),
                  pl.BlockSpec((1, LANES), lambda b, i: (0, 0))],
        out_specs=pl.BlockSpec((1, tr, LANES), lambda b, i: (b, i, 0)),
        compiler_params=_params(("arbitrary", "arbitrary")),
        name="rope_table",
    )(positions.reshape(B, S, 1), inv4)


def _swa_bias_kernel(tab_ref, o_ref):
    h = pl.program_id(0)
    half = N_BUCKETS // 2
    max_exact = half // 2
    qi = lax.broadcasted_iota(jnp.int32, (BLOCK, LANES), 0)
    kj = lax.broadcasted_iota(jnp.int32, (BLOCK, LANES), 1)
    for cb in range(5):
        rel = kj - qi + (cb * LANES - 2 * WINDOW)
        n = jnp.abs(rel)
        large = jnp.full_like(n, max_exact)
        for t in T5_LARGE_THRESHOLDS:
            large = large + jnp.where(n >= t, 1, 0)
        bucket = jnp.where(rel > 0, half, 0) + jnp.where(n < max_exact, n, large)
        bias = jnp.zeros((BLOCK, LANES), F32)
        for b in range(N_BUCKETS):
            bias = jnp.where(bucket == b, tab_ref[b, h], bias)
        o_ref[0, cb] = jnp.where(n <= WINDOW, bias, NEG_INF)


def _swa_bias(rel_bias):
    return pl.pallas_call(
        _swa_bias_kernel,
        out_shape=jax.ShapeDtypeStruct((SWA_HEADS, 5, BLOCK, LANES), F32),
        grid=(SWA_HEADS,),
        in_specs=[pl.BlockSpec(memory_space=pltpu.SMEM)],
        out_specs=pl.BlockSpec((1, 5, BLOCK, LANES), lambda h: (h, 0, 0, 0)),
        compiler_params=_params(("arbitrary",)),
        name="swa_bias",
    )(rel_bias)


def _rope_pair(t, tab):
    u = t * tab
    return u + pltpu.roll(u, 64, 1)


def _in_proj_kernel(x_ref, nm_ref, sc_ref, sh_ref, tab_ref, win_ref, qn_ref, wuq_ref, kvn_ref, wukv_ref,
                    qm_ref, km_ref, vm_ref, qs_ref, ks_ref, vs_ref, g_ref):
    x = x_ref[0]
    h = (_rms(x) * nm_ref[...] * (1.0 + sc_ref[0]) + sh_ref[0]).astype(BF16)
    tab = tab_ref[0]

    lat = jnp.dot(h, win_ref[:, C_CQ:C_QS], preferred_element_type=F32)
    cq = lat[:, C_CQ:C_CKV]
    ckv = lat[:, C_CKV:C_KR]
    kr = lat[:, C_KR:C_QS]

    lane = lax.broadcasted_iota(jnp.int32, kr.shape, 1)
    k_rope = jnp.where(lane < MLA_ROPE, _rope_pair(kr, tab), 0.0).astype(BF16)

    cqn = (_rms(cq) * qn_ref[...]).astype(BF16)
    for hh in range(MLA_HEADS):
        c0 = hh * MLA_QK_PAD
        qh = jnp.dot(cqn, wuq_ref[:, c0:c0 + MLA_QK_PAD], preferred_element_type=F32)
        qm_ref[0, :, c0:c0 + MLA_NOPE] = (qh[:, :MLA_NOPE] * MLA_Q_SCALE).astype(BF16)
        qm_ref[0, :, c0 + MLA_NOPE:c0 + MLA_QK_PAD] = (
            _rope_pair(qh[:, MLA_NOPE:], tab) * MLA_Q_SCALE).astype(BF16)

    ckvn = (_rms(ckv) * kvn_ref[...]).astype(BF16)
    kn = jnp.dot(ckvn, wukv_ref[:, :MLA_HEADS * MLA_NOPE], preferred_element_type=F32)
    for hh in range(MLA_HEADS):
        c0 = hh * MLA_QK_PAD
        km_ref[0, :, c0:c0 + MLA_NOPE] = kn[:, hh * MLA_NOPE:(hh + 1) * MLA_NOPE].astype(BF16)
        km_ref[0, :, c0 + MLA_NOPE:c0 + MLA_QK_PAD] = k_rope
    vm_ref[0] = jnp.dot(ckvn, wukv_ref[:, MLA_HEADS * MLA_NOPE:], preferred_element_type=F32).astype(BF16)

    qs = jnp.dot(h, win_ref[:, C_QS:C_KS], preferred_element_type=F32)
    qs_ref[0] = (qs * SWA_Q_SCALE).astype(BF16)
    ks_ref[0] = jnp.dot(h, win_ref[:, C_KS:C_VS], preferred_element_type=F32).astype(BF16)
    vs_ref[0] = jnp.dot(h, win_ref[:, C_VS:C_G], preferred_element_type=F32).astype(BF16)
    for j in range(2):
        c0 = C_G + j * D_MODEL
        g = jnp.dot(h, win_ref[:, c0:c0 + D_MODEL], preferred_element_type=F32)
        g_ref[0, :, j * D_MODEL:(j + 1) * D_MODEL] = jax.nn.sigmoid(g).astype(BF16)


def _in_proj(x, norm_mix, sc1, sh1, rope_tab, w_in_p, q_norm, w_uq_p, kv_norm, w_ukv_p, tm):
    B, S, D = x.shape
    kvw = SWA_KV_HEADS * SWA_HEAD_DIM
    tok = lambda w: pl.BlockSpec((1, tm, w), lambda b, i: (b, i, 0))
    per_batch = pl.BlockSpec((1, 1, D), lambda b, i: (b, 0, 0))
    out_widths = (MLA_HEADS * MLA_QK_PAD, MLA_HEADS * MLA_QK_PAD, MLA_HEADS * MLA_V,
                  SWA_HEADS * SWA_HEAD_DIM, kvw, kvw, 2 * D)
    return pl.pallas_call(
        _in_proj_kernel,
        out_shape=[jax.ShapeDtypeStruct((B, S, w), BF16) for w in out_widths],
        grid=(B, S // tm),
        in_specs=[tok(D), _resident((1, D)), per_batch, per_batch, tok(LANES),
                  _resident(w_in_p.shape), _resident((1, Q_LORA)), _resident(w_uq_p.shape),
                  _resident((1, KV_LORA)), _resident(w_ukv_p.shape)],
        out_specs=[tok(w) for w in out_widths],
        compiler_params=_params(("arbitrary", "arbitrary")),
        name="in_proj",
    )(x, norm_mix.reshape(1, D), sc1, sh1, rope_tab, w_in_p, q_norm.reshape(1, Q_LORA), w_uq_p,
      kv_norm.reshape(1, KV_LORA), w_ukv_p)


def _mla_kernel(q_ref, k_ref, v_ref, o_ref, *, tk):
    q = q_ref[0]
    tq = q.shape[0]
    S = k_ref.shape[1]
    m = jnp.full((tq, 1), -jnp.inf, F32)
    l = jnp.zeros((tq, 1), F32)
    acc = jnp.zeros((tq, MLA_V), F32)
    for c in range(S // tk):
        k = k_ref[0, c * tk:(c + 1) * tk, :]
        v = v_ref[0, c * tk:(c + 1) * tk, :]
        s = lax.dot_general(q, k, NT_DIMS, preferred_element_type=F32)
        m_new = jnp.maximum(m, jnp.max(s, axis=-1, keepdims=True))
        p = jnp.exp2(s - m_new)
        alpha = jnp.exp2(m - m_new)
        l = alpha * l + jnp.sum(p, axis=-1, keepdims=True)
        acc = alpha * acc + jnp.dot(p.astype(BF16), v, preferred_element_type=F32)
        m = m_new
    o_ref[0] = (acc / l).astype(BF16)


def _mla_attn(q_mla, k_mla, v_mla, tq, tk):
    B, S, _ = q_mla.shape
    return pl.pallas_call(
        functools.partial(_mla_kernel, tk=tk),
        out_shape=jax.ShapeDtypeStruct((B, S, MLA_HEADS * MLA_V), BF16),
        grid=(B, MLA_HEADS, S // tq),
        in_specs=[pl.BlockSpec((1, tq, MLA_QK_PAD), lambda b, h, i: (b, i, h)),
                  pl.BlockSpec((1, S, MLA_QK_PAD), lambda b, h, i: (b, 0, h)),
                  pl.BlockSpec((1, S, MLA_V), lambda b, h, i: (b, 0, h))],
        out_specs=pl.BlockSpec((1, tq, MLA_V), lambda b, h, i: (b, i, h)),
        compiler_params=_params(("arbitrary", "arbitrary", "arbitrary")),
        name="mla_attn",
    )(q_mla, k_mla, v_mla)


def _swa_kernel(q_ref, k_ref, v_ref, t_ref, sink_ref, o_ref, *, sub_blocks):
    S = k_ref.shape[1]
    span = BLOCK + 2 * WINDOW
    step = pl.program_id(1)
    for sb in range(sub_blocks):
        r0 = sb * BLOCK
        q0 = (step * sub_blocks + sb) * BLOCK
        start = pl.multiple_of(jnp.clip(q0 - WINDOW, 0, S - span), BLOCK)
        cb0 = jnp.where(q0 == 0, 2, jnp.where(q0 == S - BLOCK, 0, 1))
        for n in range(SWA_KV_HEADS):
            heads = [n * SWA_GROUP + g for g in range(SWA_GROUP)]
            q = jnp.concatenate(
                [q_ref[0, r0:r0 + BLOCK, hd * SWA_HEAD_DIM:(hd + 1) * SWA_HEAD_DIM] for hd in heads], axis=0)
            kw = k_ref[0, pl.ds(start, span), n * SWA_HEAD_DIM:(n + 1) * SWA_HEAD_DIM]
            vw = v_ref[0, pl.ds(start, span), n * SWA_HEAD_DIM:(n + 1) * SWA_HEAD_DIM]
            s = lax.dot_general(q, kw, NT_DIMS, preferred_element_type=F32)
            bias = jnp.concatenate(
                [jnp.concatenate([t_ref[hd, cb0 + c] for c in range(3)], axis=1) for hd in heads], axis=0)
            s = s + bias
            sink = jnp.concatenate(
                [jnp.broadcast_to(sink_ref[hd:hd + 1, 0:1], (BLOCK, 1)) for hd in heads], axis=0)
            m = jnp.maximum(jnp.max(s, axis=-1, keepdims=True), sink)
            p = jnp.exp(s - m)
            l = jnp.sum(p, axis=-1, keepdims=True) + jnp.exp(sink - m)
            o = jnp.dot(p.astype(BF16), vw, preferred_element_type=F32) / l
            for g, hd in enumerate(heads):
                o_ref[0, r0:r0 + BLOCK, hd * SWA_HEAD_DIM:(hd + 1) * SWA_HEAD_DIM] = (
                    o[g * BLOCK:(g + 1) * BLOCK].astype(BF16))


def _swa_attn(qs, ks, vs, bias_tab, sink_b, sub_blocks):
    B, S, W = qs.shape
    kvw = ks.shape[2]
    tq = sub_blocks * BLOCK
    return pl.pallas_call(
        functools.partial(_swa_kernel, sub_blocks=sub_blocks),
        out_shape=jax.ShapeDtypeStruct((B, S, W), BF16),
        grid=(B, S // tq),
        in_specs=[pl.BlockSpec((1, tq, W), lambda b, i: (b, i, 0)),
                  pl.BlockSpec((1, S, kvw), lambda b, i: (b, 0, 0)),
                  pl.BlockSpec((1, S, kvw), lambda b, i: (b, 0, 0)),
                  _resident(bias_tab.shape),
                  _resident(sink_b.shape)],
        out_specs=pl.BlockSpec((1, tq, W), lambda b, i: (b, i, 0)),
        compiler_params=_params(("arbitrary", "arbitrary")),
        name="swa_attn",
    )(qs, ks, vs, bias_tab, sink_b)


def _out_mlp_kernel(x_ref, oa_ref, ob_ref, g_ref, g1_ref, sc_ref, sh_ref, g2_ref, nmlp_ref, nfin_ref,
                    woa_ref, wob_ref, wout_ref, w1_ref, w2_ref, o_ref, *, tf):
    D = x_ref.shape[2]
    y_a = jnp.dot(oa_ref[0], woa_ref[...], preferred_element_type=F32)
    y_b = jnp.dot(ob_ref[0], wob_ref[...], preferred_element_type=F32)
    merged = g_ref[0, :, :D].astype(F32) * y_a + g_ref[0, :, D:].astype(F32) * y_b
    att = jnp.dot(merged.astype(BF16), wout_ref[...], preferred_element_type=F32)
    x1 = x_ref[0] + g1_ref[0] * att

    h = (_rms(x1) * nmlp_ref[...] * (1.0 + sc_ref[0]) + sh_ref[0]).astype(BF16)
    ff = jnp.zeros_like(x1)
    for c in range(w1_ref.shape[1] // tf):
        a = jnp.dot(h, w1_ref[:, c * tf:(c + 1) * tf], preferred_element_type=F32)
        a = jnp.square(jnp.maximum(a, 0.0)).astype(BF16)
        ff = ff + jnp.dot(a, w2_ref[c * tf:(c + 1) * tf, :], preferred_element_type=F32)
    x2 = x1 + g2_ref[0] * ff
    o_ref[0] = _rms(x2) * nfin_ref[...]


def _out_mlp(x, o_mla, o_swa, gates, g1, sc2, sh2, g2, norm_mlp, norm_final,
             w_o_mla, w_o_swa, w_out, w_ff1, w_ff2, tm, tf):
    B, S, D = x.shape
    tok = lambda w: pl.BlockSpec((1, tm, w), lambda b, i: (b, i, 0))
    per_batch = pl.BlockSpec((1, 1, D), lambda b, i: (b, 0, 0))
    return pl.pallas_call(
        functools.partial(_out_mlp_kernel, tf=tf),
        out_shape=jax.ShapeDtypeStruct((B, S, D), F32),
        grid=(B, S // tm),
        in_specs=[tok(D), tok(D), tok(D), tok(2 * D), per_batch, per_batch, per_batch, per_batch,
                  _resident((1, D)), _resident((1, D)),
                  _resident(w_o_mla.shape), _resident(w_o_swa.shape), _resident(w_out.shape),
                  _resident(w_ff1.shape), _resident(w_ff2.shape)],
        out_specs=tok(D),
        compiler_params=_params(("arbitrary", "arbitrary")),
        name="out_mlp",
    )(x, o_mla, o_swa, gates, g1, sc2, sh2, g2, norm_mlp.reshape(1, D), norm_final.reshape(1, D),
      w_o_mla, w_o_swa, w_out, w_ff1, w_ff2)


def _swap_halves(w):
    half = w.shape[-1] // 2
    return jnp.concatenate([w[..., half:], w[..., :half]], axis=-1)


def _pack_w_in(w_in):
    kr0 = Q_LORA + KV_LORA
    kr = w_in[:, kr0:kr0 + MLA_ROPE]
    return jnp.concatenate([w_in[:, :kr0], kr, _swap_halves(kr), w_in[:, kr0 + MLA_ROPE:]], axis=1).astype(BF16)


def _pack_w_uq(w_uq):
    w = w_uq.reshape(Q_LORA, MLA_HEADS, MLA_NOPE + MLA_ROPE)
    rope = w[:, :, MLA_NOPE:]
    w = jnp.concatenate([w[:, :, :MLA_NOPE], rope, _swap_halves(rope)], axis=-1)
    return w.reshape(Q_LORA, MLA_HEADS * MLA_QK_PAD).astype(BF16)


def _pack_w_ukv(w_ukv):
    w = w_ukv.reshape(KV_LORA, MLA_HEADS, MLA_NOPE + MLA_V)
    k = w[:, :, :MLA_NOPE].reshape(KV_LORA, MLA_HEADS * MLA_NOPE)
    v = w[:, :, MLA_NOPE:].reshape(KV_LORA, MLA_HEADS * MLA_V)
    return jnp.concatenate([k, v], axis=1).astype(BF16)


def kernel(x, c, positions, w_ada, b_ada, norm_mix, w_in, q_norm, w_uq, kv_norm, w_ukv, rel_bias, sink,
           w_o_mla, w_o_swa, w_out, norm_mlp, w_ff1, w_ff2, norm_final):
    B, S, D = x.shape
    assert w_ada.shape[0] == 1, "single-layer block"
    assert D == D_MODEL and S % (4 * BLOCK) == 0 and S >= BLOCK + 2 * WINDOW

    mod = _ada_mod(c, w_ada[0], b_ada[0])
    sh1, sc1, g1, sh2, sc2, g2 = [m.reshape(B, 1, D) for m in jnp.split(mod, N_MOD, axis=-1)]

    rope_tab = _rope_table(positions)
    bias_tab = _swa_bias(rel_bias)
    sink_b = jnp.broadcast_to(sink[0][:, None], (SWA_HEADS, LANES))

    q_mla, k_mla, v_mla, qs, ks, vs, gates = _in_proj(
        x, norm_mix[0], sc1, sh1, rope_tab, _pack_w_in(w_in[0]), q_norm[0], _pack_w_uq(w_uq[0]),
        kv_norm[0], _pack_w_ukv(w_ukv[0]), tm=512)

    o_mla = _mla_attn(q_mla, k_mla, v_mla, tq=512, tk=1024)
    o_swa = _swa_attn(qs, ks, vs, bias_tab, sink_b, sub_blocks=4)

    return _out_mlp(x, o_mla, o_swa, gates, g1, sc2, sh2, g2, norm_mlp[0], norm_final,
                    w_o_mla[0].astype(BF16), w_o_swa[0].astype(BF16), w_out[0].astype(BF16),
                    w_ff1[0].astype(BF16), w_ff2[0].astype(BF16), tm=512, tf=1024)
```

```python
import functools
import math

import jax
import jax.numpy as jnp
from jax import lax
from jax.experimental import pallas as pl
from jax.experimental.pallas import tpu as pltpu

F32 = jnp.float32
BF16 = jnp.bfloat16

D_MODEL = 1024
MLA_HEADS = 8
MLA_NOPE = 128
MLA_ROPE = 64
MLA_V = 128
Q_LORA = 384
KV_LORA = 256
ROPE_THETA = 10000.0
SWA_HEADS = 8
SWA_KV_HEADS = 2
SWA_GROUP = SWA_HEADS // SWA_KV_HEADS
SWA_HEAD_DIM = 128
WINDOW = 128
BLOCK = 128
N_BUCKETS = 32
MAX_DISTANCE = 128
D_FF = 4 * D_MODEL
N_MOD = 6
EPS = 1e-6
NEG_INF = -1e30

ROPE_HALF = MLA_ROPE // 2
LANES = 128
MLA_QK_PAD = 256
VMEM_LIMIT = 56 * 1024 * 1024

MLA_Q_SCALE = (MLA_NOPE + MLA_ROPE) ** -0.5 * math.log2(math.e)
SWA_Q_SCALE = SWA_HEAD_DIM ** -0.5

C_CQ = 0
C_CKV = C_CQ + Q_LORA
C_KR = C_CKV + KV_LORA
C_KS = C_KR + LANES
C_G = C_KS + SWA_KV_HEADS * SWA_HEAD_DIM
C_END = C_G + 2 * D_MODEL

T5_LARGE_THRESHOLDS = (12, 16, 23, 32, 46, 64, 91)

NT_DIMS = (((1,), (1,)), ((), ()))


def _resident(shape):
    nd = len(shape)
    return pl.BlockSpec(shape, lambda *_: (0,) * nd, pipeline_mode=pl.Buffered(1))


def _params(semantics):
    return pltpu.CompilerParams(dimension_semantics=semantics, vmem_limit_bytes=VMEM_LIMIT)


def _rms(x):
    return x * lax.rsqrt(jnp.mean(x * x, axis=-1, keepdims=True) + EPS)


def _ada_kernel(c_ref, w_ref, b_ref, o_ref):
    c = c_ref[...]
    c_act = c * jax.nn.sigmoid(c)
    o_ref[...] = jnp.dot(c_act, w_ref[...], preferred_element_type=F32,
                         precision=lax.Precision.HIGHEST) + b_ref[...]


def _ada_mod(c, w_ada, b_ada):
    B, D = c.shape
    N = w_ada.shape[1]
    tn = 1024
    return pl.pallas_call(
        _ada_kernel,
        out_shape=jax.ShapeDtypeStruct((B, N), F32),
        grid=(N // tn,),
        in_specs=[pl.BlockSpec((B, D), lambda j: (0, 0)),
                  pl.BlockSpec((D, tn), lambda j: (0, j)),
                  pl.BlockSpec((1, tn), lambda j: (0, j))],
        out_specs=pl.BlockSpec((B, tn), lambda j: (0, j)),
        compiler_params=_params(("arbitrary",)),
        name="ada_mod",
    )(c, w_ada, b_ada.reshape(1, N))


def _rope_table_kernel(pos_ref, inv_ref, o_ref):
    ang = inv_ref[...] * pos_ref[0].astype(F32)
    o_ref[0, :ROPE_HALF, :] = jnp.cos(ang)
    o_ref[0, ROPE_HALF:, :] = jnp.sin(ang)


def _rope_table(positions):
    B, S = positions.shape
    inv = ROPE_THETA ** (-jnp.arange(0, MLA_ROPE, 2, dtype=F32) / MLA_ROPE)
    return pl.pallas_call(
        _rope_table_kernel,
        out_shape=jax.ShapeDtypeStruct((B, MLA_ROPE, S), F32),
        grid=(B,),
        in_specs=[pl.BlockSpec((1, 1, S), lambda b: (b, 0, 0)),
                  pl.BlockSpec((ROPE_HALF, 1), lambda b: (0, 0))],
        out_specs=pl.BlockSpec((1, MLA_ROPE, S), lambda b: (b, 0, 0)),
        compiler_params=_params(("arbitrary",)),
        name="rope_table",
    )(positions.reshape(B, 1, S), inv.reshape(ROPE_HALF, 1))


def _swa_bias_kernel(tab_ref, o_ref):
    h = pl.program_id(0)
    half = N_BUCKETS // 2
    max_exact = half // 2
    kj = lax.broadcasted_iota(jnp.int32, (BLOCK, LANES), 0)
    qi = lax.broadcasted_iota(jnp.int32, (BLOCK, LANES), 1)
    for cb in range(5):
        rel = kj - qi + (cb * LANES - 2 * WINDOW)
        n = jnp.abs(rel)
        large = jnp.full_like(n, max_exact)
        for t in T5_LARGE_THRESHOLDS:
            large = large + jnp.where(n >= t, 1, 0)
        bucket = jnp.where(rel > 0, half, 0) + jnp.where(n < max_exact, n, large)
        bias = jnp.zeros((BLOCK, LANES), F32)
        for b in range(N_BUCKETS):
            bias = jnp.where(bucket == b, tab_ref[b, h], bias)
        o_ref[0, cb] = jnp.where(n <= WINDOW, bias, NEG_INF)


def _swa_bias(rel_bias):
    return pl.pallas_call(
        _swa_bias_kernel,
        out_shape=jax.ShapeDtypeStruct((SWA_HEADS, 5, BLOCK, LANES), F32),
        grid=(SWA_HEADS,),
        in_specs=[pl.BlockSpec(memory_space=pltpu.SMEM)],
        out_specs=pl.BlockSpec((1, 5, BLOCK, LANES), lambda h: (h, 0, 0, 0)),
        compiler_params=_params(("arbitrary",)),
        name="swa_bias",
    )(rel_bias)


def _rope_rows(x, tab):
    x1, x2 = x[:ROPE_HALF], x[ROPE_HALF:]
    cos, sin = tab[:ROPE_HALF], tab[ROPE_HALF:]
    return jnp.concatenate([x1 * cos - x2 * sin, x2 * cos + x1 * sin], axis=0)


def _in_proj_kernel(x_ref, nm_ref, sc_ref, sh_ref, tab_ref, win_ref, wqvt_ref, qn_ref, wuqt_ref, kvn_ref, wk_ref,
                    wvt_ref, qt_ref, km_ref, vt_ref, qst_ref, ks_ref, vst_ref, g_ref):
    x = x_ref[0]
    tm = x.shape[0]
    h = (_rms(x) * nm_ref[...] * (1.0 + sc_ref[0]) + sh_ref[0]).astype(BF16)
    tab = tab_ref[0]
    rope_pad = jnp.zeros((MLA_QK_PAD - MLA_NOPE - MLA_ROPE, tm), F32)

    lat = jnp.dot(h, win_ref[:, C_CQ:C_KS], preferred_element_type=F32)
    cq = lat[:, C_CQ:C_CKV]
    ckv = lat[:, C_CKV:C_KR]
    kr_t = lat[:, C_KR:C_KS].T
    k_rope = jnp.concatenate([_rope_rows(kr_t[:MLA_ROPE], tab), rope_pad], axis=0).T.astype(BF16)

    cqn = (_rms(cq) * qn_ref[...]).astype(BF16)
    for hh in range(MLA_HEADS):
        r0 = hh * MLA_QK_PAD
        w0 = hh * (MLA_NOPE + MLA_ROPE)
        qh = lax.dot_general(wuqt_ref[w0:w0 + MLA_NOPE + MLA_ROPE, :], cqn, NT_DIMS, preferred_element_type=F32)
        qt_ref[0, r0:r0 + MLA_NOPE, :] = (qh[:MLA_NOPE] * MLA_Q_SCALE).astype(BF16)
        q_rope = jnp.concatenate([_rope_rows(qh[MLA_NOPE:], tab) * MLA_Q_SCALE, rope_pad], axis=0)
        qt_ref[0, r0 + MLA_NOPE:r0 + MLA_QK_PAD, :] = q_rope.astype(BF16)

    ckvn = (_rms(ckv) * kvn_ref[...]).astype(BF16)
    kn = jnp.dot(ckvn, wk_ref[...], preferred_element_type=F32)
    for hh in range(MLA_HEADS):
        c0 = hh * MLA_QK_PAD
        km_ref[0, :, c0:c0 + MLA_NOPE] = kn[:, hh * MLA_NOPE:(hh + 1) * MLA_NOPE].astype(BF16)
        km_ref[0, :, c0 + MLA_NOPE:c0 + MLA_QK_PAD] = k_rope
    vt_ref[0] = lax.dot_general(wvt_ref[...], ckvn, NT_DIMS, preferred_element_type=F32).astype(BF16)

    n_q = SWA_HEADS * SWA_HEAD_DIM
    qv_t = lax.dot_general(wqvt_ref[...], h, NT_DIMS, preferred_element_type=F32)
    qst_ref[0] = (qv_t[:n_q] * SWA_Q_SCALE).astype(BF16)
    vs_t = qv_t[n_q:].astype(BF16)
    for j in range(tm // BLOCK):
        vst_ref[0, j] = vs_t[:, j * BLOCK:(j + 1) * BLOCK]
    ks_ref[0] = jnp.dot(h, win_ref[:, C_KS:C_G], preferred_element_type=F32).astype(BF16)
    for j in range(2):
        c0 = C_G + j * D_MODEL
        g = jnp.dot(h, win_ref[:, c0:c0 + D_MODEL], preferred_element_type=F32)
        g_ref[0, :, j * D_MODEL:(j + 1) * D_MODEL] = jax.nn.sigmoid(g).astype(BF16)


def _in_proj(x, norm_mix, sc1, sh1, rope_tab, w_in_p, w_qv_t, q_norm, w_uq_t, kv_norm, w_k, w_v_t, tm):
    B, S, D = x.shape
    kvw = SWA_KV_HEADS * SWA_HEAD_DIM
    tok = lambda w: pl.BlockSpec((1, tm, w), lambda b, i: (b, i, 0))
    tok_t = lambda r: pl.BlockSpec((1, r, tm), lambda b, i: (b, 0, i))
    per_batch = pl.BlockSpec((1, 1, D), lambda b, i: (b, 0, 0))
    sds = lambda *shape: jax.ShapeDtypeStruct(shape, BF16)
    return pl.pallas_call(
        _in_proj_kernel,
        out_shape=[sds(B, MLA_HEADS * MLA_QK_PAD, S), sds(B, S, MLA_HEADS * MLA_QK_PAD),
                   sds(B, MLA_HEADS * MLA_V, S), sds(B, SWA_HEADS * SWA_HEAD_DIM, S),
                   sds(B, S, kvw), sds(B, S // BLOCK, kvw, BLOCK), sds(B, S, 2 * D)],
        grid=(B, S // tm),
        in_specs=[tok(D), _resident((1, D)), per_batch, per_batch, tok_t(MLA_ROPE),
                  _resident(w_in_p.shape), _resident(w_qv_t.shape), _resident((1, Q_LORA)),
                  _resident(w_uq_t.shape), _resident((1, KV_LORA)), _resident(w_k.shape),
                  _resident(w_v_t.shape)],
        out_specs=[tok_t(MLA_HEADS * MLA_QK_PAD), tok(MLA_HEADS * MLA_QK_PAD), tok_t(MLA_HEADS * MLA_V),
                   tok_t(SWA_HEADS * SWA_HEAD_DIM), tok(kvw),
                   pl.BlockSpec((1, tm // BLOCK, kvw, BLOCK), lambda b, i: (b, i, 0, 0)), tok(2 * D)],
        compiler_params=_params(("arbitrary", "arbitrary")),
        name="in_proj",
    )(x, norm_mix.reshape(1, D), sc1, sh1, rope_tab, w_in_p, w_qv_t, q_norm.reshape(1, Q_LORA), w_uq_t,
      kv_norm.reshape(1, KV_LORA), w_k, w_v_t)


def _mla_kernel(qt_ref, k_ref, vt_ref, o_ref, *, tk):
    qt = qt_ref[0]
    tq = qt.shape[1]
    S = k_ref.shape[1]
    m = jnp.full((1, tq), -jnp.inf, F32)
    l = jnp.zeros((1, tq), F32)
    acc = jnp.zeros((MLA_V, tq), F32)
    n_chunks = S // tk
    scores = lambda c: jnp.dot(k_ref[0, c * tk:(c + 1) * tk, :], qt, preferred_element_type=F32)
    s_next = scores(0)
    for c in range(n_chunks):
        s = s_next
        if c + 1 < n_chunks:
            s_next = scores(c + 1)
        vt = vt_ref[0, :, c * tk:(c + 1) * tk]
        m_new = jnp.maximum(m, jnp.max(s, axis=0, keepdims=True))
        p = jnp.exp2(s - m_new)
        alpha = jnp.exp2(m - m_new)
        l = alpha * l + jnp.sum(p, axis=0, keepdims=True)
        acc = alpha * acc + jnp.dot(vt, p.astype(BF16), preferred_element_type=F32)
        m = m_new
    o_ref[0] = (acc / l).T.astype(BF16)


def _mla_attn(qt_mla, k_mla, vt_mla, tq, tk):
    B, S, _ = k_mla.shape
    return pl.pallas_call(
        functools.partial(_mla_kernel, tk=tk),
        out_shape=jax.ShapeDtypeStruct((B, S, MLA_HEADS * MLA_V), BF16),
        grid=(B, MLA_HEADS, S // tq),
        in_specs=[pl.BlockSpec((1, MLA_QK_PAD, tq), lambda b, h, i: (b, h, i)),
                  pl.BlockSpec((1, S, MLA_QK_PAD), lambda b, h, i: (b, 0, h)),
                  pl.BlockSpec((1, MLA_V, S), lambda b, h, i: (b, h, 0))],
        out_specs=pl.BlockSpec((1, tq, MLA_V), lambda b, h, i: (b, i, h)),
        compiler_params=_params(("arbitrary", "arbitrary", "arbitrary")),
        name="mla_attn",
    )(qt_mla, k_mla, vt_mla)


def _swa_kernel(qt_ref, k_ref, vt_ref, t_ref, sink_ref, o_ref, *, sub_blocks):
    S = k_ref.shape[1]
    span = BLOCK + 2 * WINDOW
    n_win = span // BLOCK
    step = pl.program_id(1)
    units = [(sb, n) for sb in range(sub_blocks) for n in range(SWA_KV_HEADS)]

    def window(sb):
        q0 = (step * sub_blocks + sb) * BLOCK
        start = pl.multiple_of(jnp.clip(q0 - WINDOW, 0, S - span), BLOCK)
        cb0 = jnp.where(q0 == 0, 2, jnp.where(q0 == S - BLOCK, 0, 1))
        return start, cb0

    def scores(sb, n):
        start, cb0 = window(sb)
        heads = [n * SWA_GROUP + g for g in range(SWA_GROUP)]
        qt = jnp.concatenate(
            [qt_ref[0, hd * SWA_HEAD_DIM:(hd + 1) * SWA_HEAD_DIM, sb * BLOCK:(sb + 1) * BLOCK] for hd in heads],
            axis=1)
        kw = k_ref[0, pl.ds(start, span), n * SWA_HEAD_DIM:(n + 1) * SWA_HEAD_DIM]
        bias = jnp.concatenate(
            [jnp.concatenate([t_ref[hd, cb0 + c] for c in range(n_win)], axis=0) for hd in heads], axis=1)
        return jnp.dot(kw, qt, preferred_element_type=F32) + bias

    s_next = scores(*units[0])
    for u, (sb, n) in enumerate(units):
        s = s_next
        if u + 1 < len(units):
            s_next = scores(*units[u + 1])
        start, _ = window(sb)
        blk0 = start // BLOCK
        heads = [n * SWA_GROUP + g for g in range(SWA_GROUP)]
        vt = jnp.concatenate(
            [vt_ref[0, blk0 + c, n * SWA_HEAD_DIM:(n + 1) * SWA_HEAD_DIM, :] for c in range(n_win)], axis=1)
        sink = jnp.concatenate([sink_ref[hd:hd + 1, :] for hd in heads], axis=1)
        m = jnp.maximum(jnp.max(s, axis=0, keepdims=True), sink)
        p = jnp.exp(s - m)
        l = jnp.sum(p, axis=0, keepdims=True) + jnp.exp(sink - m)
        o = jnp.dot(vt, p.astype(BF16), preferred_element_type=F32) / l
        for g, hd in enumerate(heads):
            o_ref[0, sb * BLOCK:(sb + 1) * BLOCK, hd * SWA_HEAD_DIM:(hd + 1) * SWA_HEAD_DIM] = (
                o[:, g * BLOCK:(g + 1) * BLOCK].T.astype(BF16))


def _swa_attn(qs_t, ks, vs_t, bias_tab, sink_b, sub_blocks):
    B, W, S = qs_t.shape
    kvw = ks.shape[2]
    tq = sub_blocks * BLOCK
    return pl.pallas_call(
        functools.partial(_swa_kernel, sub_blocks=sub_blocks),
        out_shape=jax.ShapeDtypeStruct((B, S, W), BF16),
        grid=(B, S // tq),
        in_specs=[pl.BlockSpec((1, W, tq), lambda b, i: (b, 0, i)),
                  pl.BlockSpec((1, S, kvw), lambda b, i: (b, 0, 0)),
                  pl.BlockSpec((1, S // BLOCK, kvw, BLOCK), lambda b, i: (b, 0, 0, 0)),
                  _resident(bias_tab.shape),
                  _resident(sink_b.shape)],
        out_specs=pl.BlockSpec((1, tq, W), lambda b, i: (b, i, 0)),
        compiler_params=_params(("arbitrary", "arbitrary")),
        name="swa_attn",
    )(qs_t, ks, vs_t, bias_tab, sink_b)


def _out_mlp_kernel(x_ref, oa_ref, ob_ref, g_ref, g1_ref, sc_ref, sh_ref, g2_ref, nmlp_ref, nfin_ref,
                    woa_ref, wob_ref, wout_ref, w1_ref, w2_ref, o_ref, *, tf):
    D = x_ref.shape[2]
    y_a = jnp.dot(oa_ref[0], woa_ref[...], preferred_element_type=F32)
    y_b = jnp.dot(ob_ref[0], wob_ref[...], preferred_element_type=F32)
    merged = g_ref[0, :, :D].astype(F32) * y_a + g_ref[0, :, D:].astype(F32) * y_b
    att = jnp.dot(merged.astype(BF16), wout_ref[...], preferred_element_type=F32)
    x1 = x_ref[0] + g1_ref[0] * att

    h = (_rms(x1) * nmlp_ref[...] * (1.0 + sc_ref[0]) + sh_ref[0]).astype(BF16)
    ff = jnp.zeros_like(x1)
    for c in range(w1_ref.shape[1] // tf):
        a = jnp.dot(h, w1_ref[:, c * tf:(c + 1) * tf], preferred_element_type=F32)
        a = jnp.square(jnp.maximum(a, 0.0)).astype(BF16)
        ff = ff + jnp.dot(a, w2_ref[c * tf:(c + 1) * tf, :], preferred_element_type=F32)
    x2 = x1 + g2_ref[0] * ff
    o_ref[0] = _rms(x2) * nfin_ref[...]


def _out_mlp(x, o_mla, o_swa, gates, g1, sc2, sh2, g2, norm_mlp, norm_final,
             w_o_mla, w_o_swa, w_out, w_ff1, w_ff2, tm, tf):
    B, S, D = x.shape
    tok = lambda w: pl.BlockSpec((1, tm, w), lambda b, i: (b, i, 0))
    per_batch = pl.BlockSpec((1, 1, D), lambda b, i: (b, 0, 0))
    return pl.pallas_call(
        functools.partial(_out_mlp_kernel, tf=tf),
        out_shape=jax.ShapeDtypeStruct((B, S, D), F32),
        grid=(B, S // tm),
        in_specs=[tok(D), tok(D), tok(D), tok(2 * D), per_batch, per_batch, per_batch, per_batch,
                  _resident((1, D)), _resident((1, D)),
                  _resident(w_o_mla.shape), _resident(w_o_swa.shape), _resident(w_out.shape),
                  _resident(w_ff1.shape), _resident(w_ff2.shape)],
        out_specs=tok(D),
        compiler_params=_params(("arbitrary", "arbitrary")),
        name="out_mlp",
    )(x, o_mla, o_swa, gates, g1, sc2, sh2, g2, norm_mlp.reshape(1, D), norm_final.reshape(1, D),
      w_o_mla, w_o_swa, w_out, w_ff1, w_ff2)


def _pack_w_in(w_in):
    kr1 = Q_LORA + KV_LORA + MLA_ROPE
    qs1 = kr1 + SWA_HEADS * SWA_HEAD_DIM
    ks1 = qs1 + SWA_KV_HEADS * SWA_HEAD_DIM
    vs1 = ks1 + SWA_KV_HEADS * SWA_HEAD_DIM
    pad = jnp.zeros((w_in.shape[0], LANES - MLA_ROPE), w_in.dtype)
    w_tok = jnp.concatenate([w_in[:, :kr1], pad, w_in[:, qs1:ks1], w_in[:, vs1:]], axis=1)
    w_t = jnp.concatenate([w_in[:, kr1:qs1], w_in[:, ks1:vs1]], axis=1).T
    return w_tok.astype(BF16), w_t.astype(BF16)


def _split_w_ukv(w_ukv):
    w = w_ukv.reshape(KV_LORA, MLA_HEADS, MLA_NOPE + MLA_V)
    k = w[:, :, :MLA_NOPE].reshape(KV_LORA, MLA_HEADS * MLA_NOPE)
    v = w[:, :, MLA_NOPE:].reshape(KV_LORA, MLA_HEADS * MLA_V)
    return k.astype(BF16), v.T.astype(BF16)


def kernel(x, c, positions, w_ada, b_ada, norm_mix, w_in, q_norm, w_uq, kv_norm, w_ukv, rel_bias, sink,
           w_o_mla, w_o_swa, w_out, norm_mlp, w_ff1, w_ff2, norm_final):
    B, S, D = x.shape
    assert w_ada.shape[0] == 1, "single-layer block"
    assert D == D_MODEL and S % (4 * BLOCK) == 0 and S >= BLOCK + 2 * WINDOW

    mod = _ada_mod(c, w_ada[0], b_ada[0])
    sh1, sc1, g1, sh2, sc2, g2 = [m.reshape(B, 1, D) for m in jnp.split(mod, N_MOD, axis=-1)]

    rope_tab = _rope_table(positions)
    bias_tab = _swa_bias(rel_bias)
    sink_b = jnp.broadcast_to(sink[0][:, None], (SWA_HEADS, LANES))

    w_k, w_v_t = _split_w_ukv(w_ukv[0])
    w_in_tok, w_in_t = _pack_w_in(w_in[0])
    qt_mla, k_mla, vt_mla, qs_t, ks, vs_t, gates = _in_proj(
        x, norm_mix[0], sc1, sh1, rope_tab, w_in_tok, w_in_t, q_norm[0], w_uq[0].T.astype(BF16),
        kv_norm[0], w_k, w_v_t, tm=512)

    o_mla = _mla_attn(qt_mla, k_mla, vt_mla, tq=512, tk=512)
    o_swa = _swa_attn(qs_t, ks, vs_t, bias_tab, sink_b, sub_blocks=4)

    return _out_mlp(x, o_mla, o_swa, gates, g1, sc2, sh2, g2, norm_mlp[0], norm_final,
                    w_o_mla[0].astype(BF16), w_o_swa[0].astype(BF16), w_out[0].astype(BF16),
                    w_ff1[0].astype(BF16), w_ff2[0].astype(BF16), tm=512, tf=1024)
```

```python
import functools
import math

import jax
import jax.numpy as jnp
from jax import lax
from jax.experimental import pallas as pl
from jax.experimental.pallas import tpu as pltpu

F32 = jnp.float32
BF16 = jnp.bfloat16

D_MODEL = 1024
MLA_HEADS = 8
MLA_NOPE = 128
MLA_ROPE = 64
MLA_V = 128
Q_LORA = 384
KV_LORA = 256
ROPE_THETA = 10000.0
SWA_HEADS = 8
SWA_KV_HEADS = 2
SWA_GROUP = SWA_HEADS // SWA_KV_HEADS
SWA_HEAD_DIM = 128
WINDOW = 128
BLOCK = 128
N_BUCKETS = 32
MAX_DISTANCE = 128
D_FF = 4 * D_MODEL
N_MOD = 6
EPS = 1e-6
NEG_INF = -1e30

ROPE_HALF = MLA_ROPE // 2
LANES = 128
MLA_QK_PAD = 256
MLA_SHIFT_COL = MLA_NOPE + MLA_ROPE
MLA_V_PAD = 256
VMEM_LIMIT = 56 * 1024 * 1024

MLA_Q_SCALE = (MLA_NOPE + MLA_ROPE) ** -0.5 * math.log2(math.e)
SWA_Q_SCALE = SWA_HEAD_DIM ** -0.5

C_CQ = 0
C_CKV = C_CQ + Q_LORA
C_KR = C_CKV + KV_LORA
C_KS = C_KR + LANES
C_G = C_KS + SWA_KV_HEADS * SWA_HEAD_DIM
C_END = C_G + 2 * D_MODEL

T5_LARGE_THRESHOLDS = (12, 16, 23, 32, 46, 64, 91)

NT_DIMS = (((1,), (1,)), ((), ()))


def _resident(shape):
    nd = len(shape)
    return pl.BlockSpec(shape, lambda *_: (0,) * nd, pipeline_mode=pl.Buffered(1))


def _params(semantics):
    return pltpu.CompilerParams(dimension_semantics=semantics, vmem_limit_bytes=VMEM_LIMIT)


def _rms(x):
    return x * lax.rsqrt(jnp.mean(x * x, axis=-1, keepdims=True) + EPS)


def _ada_kernel(c_ref, w_ref, b_ref, o_ref):
    c = c_ref[...]
    c_act = c * jax.nn.sigmoid(c)
    o_ref[...] = jnp.dot(c_act, w_ref[...], preferred_element_type=F32,
                         precision=lax.Precision.HIGHEST) + b_ref[...]


def _ada_mod(c, w_ada, b_ada):
    B, D = c.shape
    N = w_ada.shape[1]
    tn = 1024
    return pl.pallas_call(
        _ada_kernel,
        out_shape=jax.ShapeDtypeStruct((B, N), F32),
        grid=(N // tn,),
        in_specs=[pl.BlockSpec((B, D), lambda j: (0, 0)),
                  pl.BlockSpec((D, tn), lambda j: (0, j)),
                  pl.BlockSpec((1, tn), lambda j: (0, j))],
        out_specs=pl.BlockSpec((B, tn), lambda j: (0, j)),
        compiler_params=_params(("arbitrary",)),
        name="ada_mod",
    )(c, w_ada, b_ada.reshape(1, N))


def _rope_table_kernel(pos_ref, inv_ref, o_ref):
    ang = inv_ref[...] * pos_ref[0].astype(F32)
    o_ref[0, :ROPE_HALF, :] = jnp.cos(ang)
    o_ref[0, ROPE_HALF:, :] = jnp.sin(ang)


def _rope_table(positions):
    B, S = positions.shape
    inv = ROPE_THETA ** (-jnp.arange(0, MLA_ROPE, 2, dtype=F32) / MLA_ROPE)
    return pl.pallas_call(
        _rope_table_kernel,
        out_shape=jax.ShapeDtypeStruct((B, MLA_ROPE, S), F32),
        grid=(B,),
        in_specs=[pl.BlockSpec((1, 1, S), lambda b: (b, 0, 0)),
                  pl.BlockSpec((ROPE_HALF, 1), lambda b: (0, 0))],
        out_specs=pl.BlockSpec((1, MLA_ROPE, S), lambda b: (b, 0, 0)),
        compiler_params=_params(("arbitrary",)),
        name="rope_table",
    )(positions.reshape(B, 1, S), inv.reshape(ROPE_HALF, 1))


def _swa_bias_kernel(tab_ref, o_ref):
    h = pl.program_id(0)
    half = N_BUCKETS // 2
    max_exact = half // 2
    kj = lax.broadcasted_iota(jnp.int32, (BLOCK, LANES), 0)
    qi = lax.broadcasted_iota(jnp.int32, (BLOCK, LANES), 1)
    for cb in range(5):
        rel = kj - qi + (cb * LANES - 2 * WINDOW)
        n = jnp.abs(rel)
        large = jnp.full_like(n, max_exact)
        for t in T5_LARGE_THRESHOLDS:
            large = large + jnp.where(n >= t, 1, 0)
        bucket = jnp.where(rel > 0, half, 0) + jnp.where(n < max_exact, n, large)
        bias = jnp.zeros((BLOCK, LANES), F32)
        for b in range(N_BUCKETS):
            bias = jnp.where(bucket == b, tab_ref[b, h], bias)
        o_ref[0, cb] = jnp.where(n <= WINDOW, bias, NEG_INF)


def _swa_bias(rel_bias):
    return pl.pallas_call(
        _swa_bias_kernel,
        out_shape=jax.ShapeDtypeStruct((SWA_HEADS, 5, BLOCK, LANES), F32),
        grid=(SWA_HEADS,),
        in_specs=[pl.BlockSpec(memory_space=pltpu.SMEM)],
        out_specs=pl.BlockSpec((1, 5, BLOCK, LANES), lambda h: (h, 0, 0, 0)),
        compiler_params=_params(("arbitrary",)),
        name="swa_bias",
    )(rel_bias)


def _rope_rows(x, tab):
    x1, x2 = x[:ROPE_HALF], x[ROPE_HALF:]
    cos, sin = tab[:ROPE_HALF], tab[ROPE_HALF:]
    return jnp.concatenate([x1 * cos - x2 * sin, x2 * cos + x1 * sin], axis=0)


def _in_proj_kernel(x_ref, nm_ref, sc_ref, sh_ref, tab_ref, win_ref, wqvt_ref, qn_ref, wqn_ref, wqrt_ref, kvn_ref,
                    wk_ref, wv_ref, qm_ref, km_ref, vm_ref, qst_ref, ks_ref, vst_ref, g_ref):
    x = x_ref[0]
    tm = x.shape[0]
    h = (_rms(x) * nm_ref[...] * (1.0 + sc_ref[0]) + sh_ref[0]).astype(BF16)
    tab = tab_ref[0]
    rope_pad = jnp.zeros((MLA_QK_PAD - MLA_NOPE - MLA_ROPE, tm), F32)

    lat = jnp.dot(h, win_ref[:, C_CQ:C_KS], preferred_element_type=F32)
    cq = lat[:, C_CQ:C_CKV]
    ckv = lat[:, C_CKV:C_KR]
    kr_t = lat[:, C_KR:C_KS].T
    one_row = jnp.where(lax.broadcasted_iota(jnp.int32, rope_pad.shape, 0) == 0, 1.0, 0.0)
    k_rope = jnp.concatenate([_rope_rows(kr_t[:MLA_ROPE], tab), one_row], axis=0).T.astype(BF16)
    ones_col = jnp.where(lax.broadcasted_iota(jnp.int32, (tm, LANES), 1) == 0, 1.0, 0.0).astype(BF16)

    cqn = (_rms(cq) * qn_ref[...]).astype(BF16)
    qn = jnp.dot(cqn, wqn_ref[...], preferred_element_type=F32)
    qr_t = lax.dot_general(wqrt_ref[...], cqn, NT_DIMS, preferred_element_type=F32)
    for hh in range(MLA_HEADS):
        c0 = hh * MLA_QK_PAD
        qm_ref[0, :, c0:c0 + MLA_NOPE] = (qn[:, hh * MLA_NOPE:(hh + 1) * MLA_NOPE] * MLA_Q_SCALE).astype(BF16)
        q_rope = _rope_rows(qr_t[hh * MLA_ROPE:(hh + 1) * MLA_ROPE], tab) * MLA_Q_SCALE
        qm_ref[0, :, c0 + MLA_NOPE:c0 + MLA_QK_PAD] = jnp.concatenate([q_rope, rope_pad], axis=0).T.astype(BF16)

    ckvn = (_rms(ckv) * kvn_ref[...]).astype(BF16)
    kn = jnp.dot(ckvn, wk_ref[...], preferred_element_type=F32)
    for hh in range(MLA_HEADS):
        c0 = hh * MLA_QK_PAD
        km_ref[0, :, c0:c0 + MLA_NOPE] = kn[:, hh * MLA_NOPE:(hh + 1) * MLA_NOPE].astype(BF16)
        km_ref[0, :, c0 + MLA_NOPE:c0 + MLA_QK_PAD] = k_rope
    vn = jnp.dot(ckvn, wv_ref[...], preferred_element_type=F32)
    for hh in range(MLA_HEADS):
        c0 = hh * MLA_V_PAD
        vm_ref[0, :, c0:c0 + MLA_V] = vn[:, hh * MLA_V:(hh + 1) * MLA_V].astype(BF16)
        vm_ref[0, :, c0 + MLA_V:c0 + MLA_V_PAD] = ones_col

    n_q = SWA_HEADS * SWA_HEAD_DIM
    qv_t = lax.dot_general(wqvt_ref[...], h, NT_DIMS, preferred_element_type=F32)
    qst_ref[0] = (qv_t[:n_q] * SWA_Q_SCALE).astype(BF16)
    vs_t = qv_t[n_q:].astype(BF16)
    for j in range(tm // BLOCK):
        vst_ref[0, j] = vs_t[:, j * BLOCK:(j + 1) * BLOCK]
    ks_ref[0] = jnp.dot(h, win_ref[:, C_KS:C_G], preferred_element_type=F32).astype(BF16)
    for j in range(2):
        c0 = C_G + j * D_MODEL
        g = jnp.dot(h, win_ref[:, c0:c0 + D_MODEL], preferred_element_type=F32)
        g_ref[0, :, j * D_MODEL:(j + 1) * D_MODEL] = jax.nn.sigmoid(g).astype(BF16)


def _in_proj(x, norm_mix, sc1, sh1, rope_tab, w_in_p, w_qv_t, q_norm, w_qn, w_qr_t, kv_norm, w_k, w_v, tm):
    B, S, D = x.shape
    kvw = SWA_KV_HEADS * SWA_HEAD_DIM
    tok = lambda w: pl.BlockSpec((1, tm, w), lambda b, i: (b, i, 0))
    tok_t = lambda r: pl.BlockSpec((1, r, tm), lambda b, i: (b, 0, i))
    per_batch = pl.BlockSpec((1, 1, D), lambda b, i: (b, 0, 0))
    sds = lambda *shape: jax.ShapeDtypeStruct(shape, BF16)
    return pl.pallas_call(
        _in_proj_kernel,
        out_shape=[sds(B, S, MLA_HEADS * MLA_QK_PAD), sds(B, S, MLA_HEADS * MLA_QK_PAD),
                   sds(B, S, MLA_HEADS * MLA_V_PAD), sds(B, SWA_HEADS * SWA_HEAD_DIM, S),
                   sds(B, S, kvw), sds(B, S // BLOCK, kvw, BLOCK), sds(B, S, 2 * D)],
        grid=(B, S // tm),
        in_specs=[tok(D), _resident((1, D)), per_batch, per_batch, tok_t(MLA_ROPE),
                  _resident(w_in_p.shape), _resident(w_qv_t.shape), _resident((1, Q_LORA)),
                  _resident(w_qn.shape), _resident(w_qr_t.shape), _resident((1, KV_LORA)),
                  _resident(w_k.shape), _resident(w_v.shape)],
        out_specs=[tok(MLA_HEADS * MLA_QK_PAD), tok(MLA_HEADS * MLA_QK_PAD), tok(MLA_HEADS * MLA_V_PAD),
                   tok_t(SWA_HEADS * SWA_HEAD_DIM), tok(kvw),
                   pl.BlockSpec((1, tm // BLOCK, kvw, BLOCK), lambda b, i: (b, i, 0, 0)), tok(2 * D)],
        compiler_params=_params(("arbitrary", "arbitrary")),
        name="in_proj",
    )(x, norm_mix.reshape(1, D), sc1, sh1, rope_tab, w_in_p, w_qv_t, q_norm.reshape(1, Q_LORA), w_qn, w_qr_t,
      kv_norm.reshape(1, KV_LORA), w_k, w_v)


def _mla_kernel(q_ref, k_ref, v_ref, o_ref, *, tq, tk, tr):
    S = k_ref.shape[1]
    n_tiles = q_ref.shape[1] // tq
    n_chunks = S // tk
    lane = lax.broadcasted_iota(jnp.int32, (tq, LANES), 1)

    def shifted_q(i):
        q = q_ref[0, i * tq:(i + 1) * tq, :]
        s = lax.dot_general(q, k_ref[0, :tr, :], NT_DIMS, preferred_element_type=F32)
        ref = jnp.max(s, axis=-1, keepdims=True)
        q_hi = jnp.where(lane == MLA_SHIFT_COL - LANES, -ref, q[:, LANES:].astype(F32)).astype(BF16)
        return jnp.concatenate([q[:, :LANES], q_hi], axis=1)

    def probs(q, c):
        s = lax.dot_general(q, k_ref[0, c * tk:(c + 1) * tk, :], NT_DIMS, preferred_element_type=F32)
        return jnp.exp2(s).astype(BF16)

    items = [(i, c) for i in range(n_tiles) for c in range(n_chunks)]
    q_next = shifted_q(0)
    p_next = probs(q_next, 0)
    bad = jnp.zeros((), F32)
    for idx, (i, c) in enumerate(items):
        if c == 0:
            q_cur = q_next
            if i + 1 < n_tiles:
                q_next = shifted_q(i + 1)
        p = p_next
        if idx + 1 < len(items):
            i_n, c_n = items[idx + 1]
            p_next = probs(q_cur if i_n == i else q_next, c_n)
        pv = jnp.dot(p, v_ref[0, c * tk:(c + 1) * tk, :], preferred_element_type=F32)
        acc = pv if c == 0 else acc + pv
        if c == n_chunks - 1:
            o = acc[:, :MLA_V] / acc[:, MLA_V:MLA_V + 1]
            o_ref[0, i * tq:(i + 1) * tq, :] = o.astype(BF16)
            bad = jnp.maximum(bad, jnp.max(jnp.where(jnp.isfinite(o), 0.0, 1.0)))

    @pl.when(bad > 0.0)
    def _recompute_with_running_max():
        def tile_body(i, carry):
            r0 = pl.multiple_of(i * tq, tq)
            q = q_ref[0, pl.ds(r0, tq), :]

            def chunk_body(c, state):
                m, l, acc = state
                c0 = pl.multiple_of(c * tk, tk)
                s = lax.dot_general(q, k_ref[0, pl.ds(c0, tk), :], NT_DIMS, preferred_element_type=F32)
                m_new = jnp.maximum(m, jnp.max(s, axis=-1, keepdims=True))
                p = jnp.exp2(s - m_new)
                alpha = jnp.exp2(m - m_new)
                l = alpha * l + jnp.sum(p, axis=-1, keepdims=True)
                acc = alpha * acc + jnp.dot(p.astype(BF16), v_ref[0, pl.ds(c0, tk), :MLA_V],
                                            preferred_element_type=F32)
                return m_new, l, acc

            init = (jnp.full((tq, 1), -jnp.inf, F32), jnp.zeros((tq, 1), F32), jnp.zeros((tq, MLA_V), F32))
            _, l, acc = lax.fori_loop(0, n_chunks, chunk_body, init)
            o_ref[0, pl.ds(r0, tq), :] = (acc / l).astype(BF16)
            return carry

        lax.fori_loop(0, n_tiles, tile_body, 0)


def _mla_attn(q_mla, k_mla, v_mla, tqs, tq, tk, tr):
    B, S, _ = k_mla.shape
    return pl.pallas_call(
        functools.partial(_mla_kernel, tq=tq, tk=tk, tr=tr),
        out_shape=jax.ShapeDtypeStruct((B, S, MLA_HEADS * MLA_V), BF16),
        grid=(B, MLA_HEADS, S // tqs),
        in_specs=[pl.BlockSpec((1, tqs, MLA_QK_PAD), lambda b, h, i: (b, i, h)),
                  pl.BlockSpec((1, S, MLA_QK_PAD), lambda b, h, i: (b, 0, h)),
                  pl.BlockSpec((1, S, MLA_V_PAD), lambda b, h, i: (b, 0, h))],
        out_specs=pl.BlockSpec((1, tqs, MLA_V), lambda b, h, i: (b, i, h)),
        compiler_params=_params(("arbitrary", "arbitrary", "arbitrary")),
        name="mla_attn",
    )(q_mla, k_mla, v_mla)


def _swa_kernel(qt_ref, k_ref, vt_ref, t_ref, sink_ref, o_ref, *, sub_blocks):
    S = k_ref.shape[1]
    span = BLOCK + 2 * WINDOW
    n_win = span // BLOCK
    step = pl.program_id(1)
    units = [(sb, n) for sb in range(sub_blocks) for n in range(SWA_KV_HEADS)]

    def window(sb):
        q0 = (step * sub_blocks + sb) * BLOCK
        start = pl.multiple_of(jnp.clip(q0 - WINDOW, 0, S - span), BLOCK)
        cb0 = jnp.where(q0 == 0, 2, jnp.where(q0 == S - BLOCK, 0, 1))
        return start, cb0

    def scores(sb, n):
        start, cb0 = window(sb)
        heads = [n * SWA_GROUP + g for g in range(SWA_GROUP)]
        qt = jnp.concatenate(
            [qt_ref[0, hd * SWA_HEAD_DIM:(hd + 1) * SWA_HEAD_DIM, sb * BLOCK:(sb + 1) * BLOCK] for hd in heads],
            axis=1)
        kw = k_ref[0, pl.ds(start, span), n * SWA_HEAD_DIM:(n + 1) * SWA_HEAD_DIM]
        bias = jnp.concatenate(
            [jnp.concatenate([t_ref[hd, cb0 + c] for c in range(n_win)], axis=0) for hd in heads], axis=1)
        return jnp.dot(kw, qt, preferred_element_type=F32) + bias

    s_next = scores(*units[0])
    for u, (sb, n) in enumerate(units):
        s = s_next
        if u + 1 < len(units):
            s_next = scores(*units[u + 1])
        start, _ = window(sb)
        blk0 = start // BLOCK
        heads = [n * SWA_GROUP + g for g in range(SWA_GROUP)]
        vt = jnp.concatenate(
            [vt_ref[0, blk0 + c, n * SWA_HEAD_DIM:(n + 1) * SWA_HEAD_DIM, :] for c in range(n_win)], axis=1)
        sink = jnp.concatenate([sink_ref[hd:hd + 1, :] for hd in heads], axis=1)
        m = jnp.maximum(jnp.max(s, axis=0, keepdims=True), sink)
        p = jnp.exp(s - m)
        l = jnp.sum(p, axis=0, keepdims=True) + jnp.exp(sink - m)
        o = jnp.dot(vt, p.astype(BF16), preferred_element_type=F32) / l
        for g, hd in enumerate(heads):
            o_ref[0, sb * BLOCK:(sb + 1) * BLOCK, hd * SWA_HEAD_DIM:(hd + 1) * SWA_HEAD_DIM] = (
                o[:, g * BLOCK:(g + 1) * BLOCK].T.astype(BF16))


def _swa_attn(qs_t, ks, vs_t, bias_tab, sink_b, sub_blocks):
    B, W, S = qs_t.shape
    kvw = ks.shape[2]
    tq = sub_blocks * BLOCK
    return pl.pallas_call(
        functools.partial(_swa_kernel, sub_blocks=sub_blocks),
        out_shape=jax.ShapeDtypeStruct((B, S, W), BF16),
        grid=(B, S // tq),
        in_specs=[pl.BlockSpec((1, W, tq), lambda b, i: (b, 0, i)),
                  pl.BlockSpec((1, S, kvw), lambda b, i: (b, 0, 0)),
                  pl.BlockSpec((1, S // BLOCK, kvw, BLOCK), lambda b, i: (b, 0, 0, 0)),
                  _resident(bias_tab.shape),
                  _resident(sink_b.shape)],
        out_specs=pl.BlockSpec((1, tq, W), lambda b, i: (b, i, 0)),
        compiler_params=_params(("arbitrary", "arbitrary")),
        name="swa_attn",
    )(qs_t, ks, vs_t, bias_tab, sink_b)


def _out_mlp_kernel(x_ref, oa_ref, ob_ref, g_ref, g1_ref, sc_ref, sh_ref, g2_ref, nmlp_ref, nfin_ref,
                    woa_ref, wob_ref, wout_ref, w1_ref, w2_ref, o_ref, *, tf):
    D = x_ref.shape[2]
    y_a = jnp.dot(oa_ref[0], woa_ref[...], preferred_element_type=F32)
    y_b = jnp.dot(ob_ref[0], wob_ref[...], preferred_element_type=F32)
    merged = g_ref[0, :, :D].astype(F32) * y_a + g_ref[0, :, D:].astype(F32) * y_b
    att = jnp.dot(merged.astype(BF16), wout_ref[...], preferred_element_type=F32)
    x1 = x_ref[0] + g1_ref[0] * att

    h = (_rms(x1) * nmlp_ref[...] * (1.0 + sc_ref[0]) + sh_ref[0]).astype(BF16)
    ff = jnp.zeros_like(x1)
    for c in range(w1_ref.shape[1] // tf):
        a = jnp.dot(h, w1_ref[:, c * tf:(c + 1) * tf], preferred_element_type=F32)
        a = jnp.square(jnp.maximum(a, 0.0)).astype(BF16)
        ff = ff + jnp.dot(a, w2_ref[c * tf:(c + 1) * tf, :], preferred_element_type=F32)
    x2 = x1 + g2_ref[0] * ff
    o_ref[0] = _rms(x2) * nfin_ref[...]


def _out_mlp(x, o_mla, o_swa, gates, g1, sc2, sh2, g2, norm_mlp, norm_final,
             w_o_mla, w_o_swa, w_out, w_ff1, w_ff2, tm, tf):
    B, S, D = x.shape
    tok = lambda w: pl.BlockSpec((1, tm, w), lambda b, i: (b, i, 0))
    per_batch = pl.BlockSpec((1, 1, D), lambda b, i: (b, 0, 0))
    return pl.pallas_call(
        functools.partial(_out_mlp_kernel, tf=tf),
        out_shape=jax.ShapeDtypeStruct((B, S, D), F32),
        grid=(B, S // tm),
        in_specs=[tok(D), tok(D), tok(D), tok(2 * D), per_batch, per_batch, per_batch, per_batch,
                  _resident((1, D)), _resident((1, D)),
                  _resident(w_o_mla.shape), _resident(w_o_swa.shape), _resident(w_out.shape),
                  _resident(w_ff1.shape), _resident(w_ff2.shape)],
        out_specs=tok(D),
        compiler_params=_params(("arbitrary", "arbitrary")),
        name="out_mlp",
    )(x, o_mla, o_swa, gates, g1, sc2, sh2, g2, norm_mlp.reshape(1, D), norm_final.reshape(1, D),
      w_o_mla, w_o_swa, w_out, w_ff1, w_ff2)


def _pack_w_in(w_in):
    kr1 = Q_LORA + KV_LORA + MLA_ROPE
    qs1 = kr1 + SWA_HEADS * SWA_HEAD_DIM
    ks1 = qs1 + SWA_KV_HEADS * SWA_HEAD_DIM
    vs1 = ks1 + SWA_KV_HEADS * SWA_HEAD_DIM
    pad = jnp.zeros((w_in.shape[0], LANES - MLA_ROPE), w_in.dtype)
    w_tok = jnp.concatenate([w_in[:, :kr1], pad, w_in[:, qs1:ks1], w_in[:, vs1:]], axis=1)
    w_t = jnp.concatenate([w_in[:, kr1:qs1], w_in[:, ks1:vs1]], axis=1).T
    return w_tok.astype(BF16), w_t.astype(BF16)


def _split_heads(w, n_first):
    w3 = w.reshape(w.shape[0], MLA_HEADS, -1)
    first = w3[:, :, :n_first].reshape(w.shape[0], -1)
    second = w3[:, :, n_first:].reshape(w.shape[0], -1)
    return first.astype(BF16), second.astype(BF16)


def kernel(x, c, positions, w_ada, b_ada, norm_mix, w_in, q_norm, w_uq, kv_norm, w_ukv, rel_bias, sink,
           w_o_mla, w_o_swa, w_out, norm_mlp, w_ff1, w_ff2, norm_final):
    B, S, D = x.shape
    assert w_ada.shape[0] == 1, "single-layer block"
    assert D == D_MODEL and S % (4 * BLOCK) == 0 and S >= BLOCK + 2 * WINDOW

    mod = _ada_mod(c, w_ada[0], b_ada[0])
    sh1, sc1, g1, sh2, sc2, g2 = [m.reshape(B, 1, D) for m in jnp.split(mod, N_MOD, axis=-1)]

    rope_tab = _rope_table(positions)
    bias_tab = _swa_bias(rel_bias)
    sink_b = jnp.broadcast_to(sink[0][:, None], (SWA_HEADS, LANES))

    w_k, w_v = _split_heads(w_ukv[0], MLA_NOPE)
    w_qn, w_qr = _split_heads(w_uq[0], MLA_NOPE)
    w_in_tok, w_in_t = _pack_w_in(w_in[0])
    q_mla, k_mla, v_mla, qs_t, ks, vs_t, gates = _in_proj(
        x, norm_mix[0], sc1, sh1, rope_tab, w_in_tok, w_in_t, q_norm[0], w_qn, w_qr.T,
        kv_norm[0], w_k, w_v, tm=512)

    o_mla = _mla_attn(q_mla, k_mla, v_mla, tqs=4096, tq=512, tk=1024, tr=256)
    o_swa = _swa_attn(qs_t, ks, vs_t, bias_tab, sink_b, sub_blocks=4)

    return _out_mlp(x, o_mla, o_swa, gates, g1, sc2, sh2, g2, norm_mlp[0], norm_final,
                    w_o_mla[0].astype(BF16), w_o_swa[0].astype(BF16), w_out[0].astype(BF16),
                    w_ff1[0].astype(BF16), w_ff2[0].astype(BF16), tm=512, tf=1024)
```

```python
import functools
import math

import jax
import jax.numpy as jnp
from jax import lax
from jax.experimental import pallas as pl
from jax.experimental.pallas import tpu as pltpu

F32 = jnp.float32
BF16 = jnp.bfloat16

D_MODEL = 1024
MLA_HEADS = 8
MLA_NOPE = 128
MLA_ROPE = 64
MLA_V = 128
Q_LORA = 384
KV_LORA = 256
ROPE_THETA = 10000.0
SWA_HEADS = 8
SWA_KV_HEADS = 2
SWA_GROUP = SWA_HEADS // SWA_KV_HEADS
SWA_HEAD_DIM = 128
WINDOW = 128
BLOCK = 128
N_BUCKETS = 32
MAX_DISTANCE = 128
D_FF = 4 * D_MODEL
N_MOD = 6
EPS = 1e-6
NEG_INF = -1e30

ROPE_HALF = MLA_ROPE // 2
LANES = 128
MLA_QK_PAD = 256
MLA_SHIFT_COL = MLA_NOPE + MLA_ROPE
MLA_V_PAD = 256
SWA_VT_ROWS = SWA_HEAD_DIM + 16
VMEM_LIMIT = 56 * 1024 * 1024

LOG2_E = math.log2(math.e)
MLA_Q_SCALE = (MLA_NOPE + MLA_ROPE) ** -0.5 * LOG2_E
SWA_Q_SCALE = SWA_HEAD_DIM ** -0.5 * LOG2_E

C_CQ = 0
C_CKV = C_CQ + Q_LORA
C_KR = C_CKV + KV_LORA
C_KS = C_KR + LANES
C_G = C_KS + SWA_KV_HEADS * SWA_HEAD_DIM
C_END = C_G + 2 * D_MODEL

T5_LARGE_THRESHOLDS = (12, 16, 23, 32, 46, 64, 91)

NT_DIMS = (((1,), (1,)), ((), ()))
TN_DIMS = (((0,), (0,)), ((), ()))


def _resident(shape):
    nd = len(shape)
    return pl.BlockSpec(shape, lambda *_: (0,) * nd, pipeline_mode=pl.Buffered(1))


def _params(semantics):
    return pltpu.CompilerParams(dimension_semantics=semantics, vmem_limit_bytes=VMEM_LIMIT)


def _rms(x):
    return x * lax.rsqrt(jnp.mean(x * x, axis=-1, keepdims=True) + EPS)


def _ada_kernel(c_ref, w_ref, b_ref, o_ref):
    c = c_ref[...]
    c_act = c * jax.nn.sigmoid(c)
    o_ref[...] = jnp.dot(c_act, w_ref[...], preferred_element_type=F32,
                         precision=lax.Precision.HIGHEST) + b_ref[...]


def _ada_mod(c, w_ada, b_ada):
    B, D = c.shape
    N = w_ada.shape[1]
    tn = 1024
    return pl.pallas_call(
        _ada_kernel,
        out_shape=jax.ShapeDtypeStruct((B, N), F32),
        grid=(N // tn,),
        in_specs=[pl.BlockSpec((B, D), lambda j: (0, 0)),
                  pl.BlockSpec((D, tn), lambda j: (0, j)),
                  pl.BlockSpec((1, tn), lambda j: (0, j))],
        out_specs=pl.BlockSpec((B, tn), lambda j: (0, j)),
        compiler_params=_params(("arbitrary",)),
        name="ada_mod",
    )(c, w_ada, b_ada.reshape(1, N))


def _rope_table_kernel(pos_ref, inv_ref, o_ref):
    ang = inv_ref[...] * pos_ref[0].astype(F32)
    o_ref[0, :ROPE_HALF, :] = jnp.cos(ang)
    o_ref[0, ROPE_HALF:, :] = jnp.sin(ang)


def _rope_table(positions):
    B, S = positions.shape
    inv = ROPE_THETA ** (-jnp.arange(0, MLA_ROPE, 2, dtype=F32) / MLA_ROPE)
    return pl.pallas_call(
        _rope_table_kernel,
        out_shape=jax.ShapeDtypeStruct((B, MLA_ROPE, S), F32),
        grid=(B,),
        in_specs=[pl.BlockSpec((1, 1, S), lambda b: (b, 0, 0)),
                  pl.BlockSpec((ROPE_HALF, 1), lambda b: (0, 0))],
        out_specs=pl.BlockSpec((1, MLA_ROPE, S), lambda b: (b, 0, 0)),
        compiler_params=_params(("arbitrary",)),
        name="rope_table",
    )(positions.reshape(B, 1, S), inv.reshape(ROPE_HALF, 1))


def _swa_bias_kernel(tab_ref, sink_ref, o_ref):
    h = pl.program_id(0)
    half = N_BUCKETS // 2
    max_exact = half // 2
    kj = lax.broadcasted_iota(jnp.int32, (BLOCK, LANES), 0)
    qi = lax.broadcasted_iota(jnp.int32, (BLOCK, LANES), 1)
    for cb in range(5):
        rel = kj - qi + (cb * LANES - 2 * WINDOW)
        n = jnp.abs(rel)
        large = jnp.full_like(n, max_exact)
        for t in T5_LARGE_THRESHOLDS:
            large = large + jnp.where(n >= t, 1, 0)
        bucket = jnp.where(rel > 0, half, 0) + jnp.where(n < max_exact, n, large)
        bias = jnp.zeros((BLOCK, LANES), F32)
        for b in range(N_BUCKETS):
            bias = jnp.where(bucket == b, tab_ref[b, h], bias)
        o_ref[0, cb] = jnp.where(n <= WINDOW, (bias - sink_ref[0, h]) * LOG2_E, NEG_INF)


def _swa_bias(rel_bias, sink):
    return pl.pallas_call(
        _swa_bias_kernel,
        out_shape=jax.ShapeDtypeStruct((SWA_HEADS, 5, BLOCK, LANES), F32),
        grid=(SWA_HEADS,),
        in_specs=[pl.BlockSpec(memory_space=pltpu.SMEM), pl.BlockSpec(memory_space=pltpu.SMEM)],
        out_specs=pl.BlockSpec((1, 5, BLOCK, LANES), lambda h: (h, 0, 0, 0)),
        compiler_params=_params(("arbitrary",)),
        name="swa_bias",
    )(rel_bias, sink.reshape(1, SWA_HEADS))


def _rope_rows(x, tab):
    x1, x2 = x[:ROPE_HALF], x[ROPE_HALF:]
    cos, sin = tab[:ROPE_HALF], tab[ROPE_HALF:]
    return jnp.concatenate([x1 * cos - x2 * sin, x2 * cos + x1 * sin], axis=0)


def _in_proj_kernel(x_ref, nm_ref, sc_ref, sh_ref, tab_ref, win_ref, wqvt_ref, qn_ref, wqn_ref, wqrt_ref, kvn_ref,
                    wk_ref, wv_ref, qm_ref, km_ref, vm_ref, qst_ref, ks_ref, vst_ref, g_ref):
    x = x_ref[0]
    tm = x.shape[0]
    h = (_rms(x) * nm_ref[...] * (1.0 + sc_ref[0]) + sh_ref[0]).astype(BF16)
    tab = tab_ref[0]
    rope_pad = jnp.zeros((MLA_QK_PAD - MLA_NOPE - MLA_ROPE, tm), F32)

    lat = jnp.dot(h, win_ref[:, C_CQ:C_KS], preferred_element_type=F32)
    cq = lat[:, C_CQ:C_CKV]
    ckv = lat[:, C_CKV:C_KR]
    kr_t = lat[:, C_KR:C_KS].T
    one_row = jnp.where(lax.broadcasted_iota(jnp.int32, rope_pad.shape, 0) == 0, 1.0, 0.0)
    k_rope = jnp.concatenate([_rope_rows(kr_t[:MLA_ROPE], tab), one_row], axis=0).T.astype(BF16)
    ones_col = jnp.where(lax.broadcasted_iota(jnp.int32, (tm, LANES), 1) == 0, 1.0, 0.0).astype(BF16)

    cqn = (_rms(cq) * qn_ref[...]).astype(BF16)
    qn = jnp.dot(cqn, wqn_ref[...], preferred_element_type=F32)
    qr_t = lax.dot_general(wqrt_ref[...], cqn, NT_DIMS, preferred_element_type=F32)
    for hh in range(MLA_HEADS):
        c0 = hh * MLA_QK_PAD
        qm_ref[0, :, c0:c0 + MLA_NOPE] = (qn[:, hh * MLA_NOPE:(hh + 1) * MLA_NOPE] * MLA_Q_SCALE).astype(BF16)
        q_rope = _rope_rows(qr_t[hh * MLA_ROPE:(hh + 1) * MLA_ROPE], tab) * MLA_Q_SCALE
        qm_ref[0, :, c0 + MLA_NOPE:c0 + MLA_QK_PAD] = jnp.concatenate([q_rope, rope_pad], axis=0).T.astype(BF16)

    ckvn = (_rms(ckv) * kvn_ref[...]).astype(BF16)
    kn = jnp.dot(ckvn, wk_ref[...], preferred_element_type=F32)
    for hh in range(MLA_HEADS):
        c0 = hh * MLA_QK_PAD
        km_ref[0, :, c0:c0 + MLA_NOPE] = kn[:, hh * MLA_NOPE:(hh + 1) * MLA_NOPE].astype(BF16)
        km_ref[0, :, c0 + MLA_NOPE:c0 + MLA_QK_PAD] = k_rope
    vn = jnp.dot(ckvn, wv_ref[...], preferred_element_type=F32)
    for hh in range(MLA_HEADS):
        c0 = hh * MLA_V_PAD
        vm_ref[0, :, c0:c0 + MLA_V] = vn[:, hh * MLA_V:(hh + 1) * MLA_V].astype(BF16)
        vm_ref[0, :, c0 + MLA_V:c0 + MLA_V_PAD] = ones_col

    n_q = SWA_HEADS * SWA_HEAD_DIM
    qv_t = lax.dot_general(wqvt_ref[...], h, NT_DIMS, preferred_element_type=F32)
    qst_ref[0] = (qv_t[:n_q] * SWA_Q_SCALE).astype(BF16)
    ones_rows = jnp.where(lax.broadcasted_iota(jnp.int32, (SWA_VT_ROWS - SWA_HEAD_DIM, BLOCK), 0) == 0,
                          1.0, 0.0).astype(BF16)
    for n in range(SWA_KV_HEADS):
        vs_t = qv_t[n_q + n * SWA_HEAD_DIM:n_q + (n + 1) * SWA_HEAD_DIM].astype(BF16)
        for j in range(tm // BLOCK):
            vst_ref[0, j, n, :SWA_HEAD_DIM, :] = vs_t[:, j * BLOCK:(j + 1) * BLOCK]
            vst_ref[0, j, n, SWA_HEAD_DIM:, :] = ones_rows
    ks_ref[0] = jnp.dot(h, win_ref[:, C_KS:C_G], preferred_element_type=F32).astype(BF16)
    for j in range(2):
        c0 = C_G + j * D_MODEL
        g = jnp.dot(h, win_ref[:, c0:c0 + D_MODEL], preferred_element_type=F32)
        g_ref[0, :, j * D_MODEL:(j + 1) * D_MODEL] = jax.nn.sigmoid(g).astype(BF16)


def _in_proj(x, norm_mix, sc1, sh1, rope_tab, w_in_p, w_qv_t, q_norm, w_qn, w_qr_t, kv_norm, w_k, w_v, tm):
    B, S, D = x.shape
    kvw = SWA_KV_HEADS * SWA_HEAD_DIM
    tok = lambda w: pl.BlockSpec((1, tm, w), lambda b, i: (b, i, 0))
    tok_t = lambda r: pl.BlockSpec((1, r, tm), lambda b, i: (b, 0, i))
    per_batch = pl.BlockSpec((1, 1, D), lambda b, i: (b, 0, 0))
    sds = lambda *shape: jax.ShapeDtypeStruct(shape, BF16)
    return pl.pallas_call(
        _in_proj_kernel,
        out_shape=[sds(B, S, MLA_HEADS * MLA_QK_PAD), sds(B, S, MLA_HEADS * MLA_QK_PAD),
                   sds(B, S, MLA_HEADS * MLA_V_PAD), sds(B, SWA_HEADS * SWA_HEAD_DIM, S),
                   sds(B, S, kvw), sds(B, S // BLOCK, SWA_KV_HEADS, SWA_VT_ROWS, BLOCK), sds(B, S, 2 * D)],
        grid=(B, S // tm),
        in_specs=[tok(D), _resident((1, D)), per_batch, per_batch, tok_t(MLA_ROPE),
                  _resident(w_in_p.shape), _resident(w_qv_t.shape), _resident((1, Q_LORA)),
                  _resident(w_qn.shape), _resident(w_qr_t.shape), _resident((1, KV_LORA)),
                  _resident(w_k.shape), _resident(w_v.shape)],
        out_specs=[tok(MLA_HEADS * MLA_QK_PAD), tok(MLA_HEADS * MLA_QK_PAD), tok(MLA_HEADS * MLA_V_PAD),
                   tok_t(SWA_HEADS * SWA_HEAD_DIM), tok(kvw),
                   pl.BlockSpec((1, tm // BLOCK, SWA_KV_HEADS, SWA_VT_ROWS, BLOCK), lambda b, i: (b, i, 0, 0, 0)),
                   tok(2 * D)],
        compiler_params=_params(("arbitrary", "arbitrary")),
        name="in_proj",
    )(x, norm_mix.reshape(1, D), sc1, sh1, rope_tab, w_in_p, w_qv_t, q_norm.reshape(1, Q_LORA), w_qn, w_qr_t,
      kv_norm.reshape(1, KV_LORA), w_k, w_v)


def _mla_kernel(q_ref, k_ref, v_ref, o_ref, *, tq, tk, tr):
    S = k_ref.shape[1]
    n_tiles = q_ref.shape[1] // tq
    n_chunks = S // tk
    lane = lax.broadcasted_iota(jnp.int32, (tq, LANES), 1)

    def shifted_q(i):
        q = q_ref[0, i * tq:(i + 1) * tq, :]
        s = lax.dot_general(q, k_ref[0, :tr, :], NT_DIMS, preferred_element_type=F32)
        ref = jnp.max(s, axis=-1, keepdims=True)
        q_hi = jnp.where(lane == MLA_SHIFT_COL - LANES, -ref, q[:, LANES:].astype(F32)).astype(BF16)
        return jnp.concatenate([q[:, :LANES], q_hi], axis=1)

    def probs(q, c):
        s = lax.dot_general(q, k_ref[0, c * tk:(c + 1) * tk, :], NT_DIMS, preferred_element_type=F32)
        return jnp.exp2(s).astype(BF16)

    items = [(i, c) for i in range(n_tiles) for c in range(n_chunks)]
    q_next = shifted_q(0)
    p_next = probs(q_next, 0)
    o_sum = jnp.zeros((tq, MLA_V), F32)
    for idx, (i, c) in enumerate(items):
        if c == 0:
            q_cur = q_next
            if i + 1 < n_tiles:
                q_next = shifted_q(i + 1)
        p = p_next
        if idx + 1 < len(items):
            i_n, c_n = items[idx + 1]
            p_next = probs(q_cur if i_n == i else q_next, c_n)
        pv = jnp.dot(p, v_ref[0, c * tk:(c + 1) * tk, :], preferred_element_type=F32)
        acc = pv if c == 0 else acc + pv
        if c == n_chunks - 1:
            o = acc[:, :MLA_V] / acc[:, MLA_V:MLA_V + 1]
            o_ref[0, i * tq:(i + 1) * tq, :] = o.astype(BF16)
            o_sum = o_sum + o
    bad = jnp.max(jnp.where(jnp.isfinite(o_sum), 0.0, 1.0))

    @pl.when(bad > 0.0)
    def _recompute_with_running_max():
        def tile_body(i, carry):
            r0 = pl.multiple_of(i * tq, tq)
            q = q_ref[0, pl.ds(r0, tq), :]

            def chunk_body(c, state):
                m, l, acc = state
                c0 = pl.multiple_of(c * tk, tk)
                s = lax.dot_general(q, k_ref[0, pl.ds(c0, tk), :], NT_DIMS, preferred_element_type=F32)
                m_new = jnp.maximum(m, jnp.max(s, axis=-1, keepdims=True))
                p = jnp.exp2(s - m_new)
                alpha = jnp.exp2(m - m_new)
                l = alpha * l + jnp.sum(p, axis=-1, keepdims=True)
                acc = alpha * acc + jnp.dot(p.astype(BF16), v_ref[0, pl.ds(c0, tk), :MLA_V],
                                            preferred_element_type=F32)
                return m_new, l, acc

            init = (jnp.full((tq, 1), -jnp.inf, F32), jnp.zeros((tq, 1), F32), jnp.zeros((tq, MLA_V), F32))
            _, l, acc = lax.fori_loop(0, n_chunks, chunk_body, init)
            o_ref[0, pl.ds(r0, tq), :] = (acc / l).astype(BF16)
            return carry

        lax.fori_loop(0, n_tiles, tile_body, 0)


def _mla_attn(q_mla, k_mla, v_mla, tqs, tq, tk, tr):
    B, S, _ = k_mla.shape
    return pl.pallas_call(
        functools.partial(_mla_kernel, tq=tq, tk=tk, tr=tr),
        out_shape=jax.ShapeDtypeStruct((B, S, MLA_HEADS * MLA_V), BF16),
        grid=(B, MLA_HEADS, S // tqs),
        in_specs=[pl.BlockSpec((1, tqs, MLA_QK_PAD), lambda b, h, i: (b, i, h)),
                  pl.BlockSpec((1, S, MLA_QK_PAD), lambda b, h, i: (b, 0, h)),
                  pl.BlockSpec((1, S, MLA_V_PAD), lambda b, h, i: (b, 0, h))],
        out_specs=pl.BlockSpec((1, tqs, MLA_V), lambda b, h, i: (b, i, h)),
        compiler_params=_params(("arbitrary", "arbitrary", "arbitrary")),
        name="mla_attn",
    )(q_mla, k_mla, v_mla)


def _swa_kernel(qt_ref, k_ref, vt_ref, t_ref, o_ref, *, sub_blocks, unit_heads, lookahead):
    S = k_ref.shape[1]
    span = BLOCK + 2 * WINDOW
    n_win = span // BLOCK
    step = pl.program_id(1)
    units = [(sb, hd0) for sb in range(sub_blocks) for hd0 in range(0, SWA_HEADS, unit_heads)]

    def window(sb):
        q0 = (step * sub_blocks + sb) * BLOCK
        start = pl.multiple_of(jnp.clip(q0 - WINDOW, 0, S - span), BLOCK)
        cb0 = jnp.where(q0 == 0, 2, jnp.where(q0 == S - BLOCK, 0, 1))
        return start, cb0

    def scores(sb, hd0):
        start, cb0 = window(sb)
        n = hd0 // SWA_GROUP
        heads = range(hd0, hd0 + unit_heads)
        qt = jnp.concatenate(
            [qt_ref[0, hd * SWA_HEAD_DIM:(hd + 1) * SWA_HEAD_DIM, sb * BLOCK:(sb + 1) * BLOCK] for hd in heads],
            axis=1)
        kw = k_ref[0, pl.ds(start, span), n * SWA_HEAD_DIM:(n + 1) * SWA_HEAD_DIM]
        bias = jnp.concatenate(
            [jnp.concatenate([t_ref[hd, cb0 + c] for c in range(n_win)], axis=0) for hd in heads], axis=1)
        return jnp.dot(kw, qt, preferred_element_type=F32) + bias

    def values_t(sb, hd0):
        start, _ = window(sb)
        blk0 = start // BLOCK
        return jnp.concatenate([vt_ref[0, blk0 + c, hd0 // SWA_GROUP] for c in range(n_win)], axis=1)

    def store(sb, hd0, o):
        for g in range(unit_heads):
            hd = hd0 + g
            o_ref[0, hd * SWA_HEAD_DIM:(hd + 1) * SWA_HEAD_DIM, sb * BLOCK:(sb + 1) * BLOCK] = (
                o[:, g * BLOCK:(g + 1) * BLOCK].astype(BF16))

    o_sum = jnp.zeros((SWA_HEAD_DIM, unit_heads * BLOCK), F32)
    probs = lambda unit: jnp.exp2(scores(*unit)).astype(BF16)
    ahead = [probs(unit) for unit in units[:lookahead]]
    for u, unit in enumerate(units):
        p = ahead.pop(0)
        if u + lookahead < len(units):
            ahead.append(probs(units[u + lookahead]))
        ov = jnp.dot(values_t(*unit), p, preferred_element_type=F32)
        o = ov[:SWA_HEAD_DIM] / (ov[SWA_HEAD_DIM:SWA_HEAD_DIM + 1] + 1.0)
        store(*unit, o)
        o_sum = o_sum + o
    bad = jnp.max(jnp.where(jnp.isfinite(o_sum), 0.0, 1.0))

    @pl.when(bad > 0.0)
    def _recompute_with_row_max():
        for unit in units:
            s = scores(*unit)
            m = jnp.maximum(jnp.max(s, axis=0, keepdims=True), 0.0)
            p = jnp.exp2(s - m)
            l = jnp.sum(p, axis=0, keepdims=True) + jnp.exp2(-m)
            ov = jnp.dot(values_t(*unit), p.astype(BF16), preferred_element_type=F32)
            store(*unit, ov[:SWA_HEAD_DIM] / l)


def _swa_attn(qs_t, ks, vs_t, bias_tab, sub_blocks, unit_heads, lookahead):
    B, W, S = qs_t.shape
    kvw = ks.shape[2]
    tq = sub_blocks * BLOCK
    return pl.pallas_call(
        functools.partial(_swa_kernel, sub_blocks=sub_blocks, unit_heads=unit_heads, lookahead=lookahead),
        out_shape=jax.ShapeDtypeStruct((B, W, S), BF16),
        grid=(B, S // tq),
        in_specs=[pl.BlockSpec((1, W, tq), lambda b, i: (b, 0, i)),
                  pl.BlockSpec((1, S, kvw), lambda b, i: (b, 0, 0)),
                  pl.BlockSpec((1, S // BLOCK, SWA_KV_HEADS, SWA_VT_ROWS, BLOCK), lambda b, i: (b, 0, 0, 0, 0)),
                  _resident(bias_tab.shape)],
        out_specs=pl.BlockSpec((1, W, tq), lambda b, i: (b, 0, i)),
        compiler_params=_params(("arbitrary", "arbitrary")),
        name="swa_attn",
    )(qs_t, ks, vs_t, bias_tab)


def _out_mlp_kernel(x_ref, oa_ref, obt_ref, g_ref, g1_ref, sc_ref, sh_ref, g2_ref, nmlp_ref, nfin_ref,
                    woa_ref, wob_ref, wout_ref, w1_ref, w2_ref, o_ref, *, tf):
    D = x_ref.shape[2]
    y_a = jnp.dot(oa_ref[0], woa_ref[...], preferred_element_type=F32)
    y_b = lax.dot_general(obt_ref[0], wob_ref[...], TN_DIMS, preferred_element_type=F32)
    merged = g_ref[0, :, :D].astype(F32) * y_a + g_ref[0, :, D:].astype(F32) * y_b
    att = jnp.dot(merged.astype(BF16), wout_ref[...], preferred_element_type=F32)
    x1 = x_ref[0] + g1_ref[0] * att

    h = (_rms(x1) * nmlp_ref[...] * (1.0 + sc_ref[0]) + sh_ref[0]).astype(BF16)
    ff = jnp.zeros_like(x1)
    for c in range(w1_ref.shape[1] // tf):
        a = jnp.dot(h, w1_ref[:, c * tf:(c + 1) * tf], preferred_element_type=F32)
        a = jnp.square(jnp.maximum(a, 0.0)).astype(BF16)
        ff = ff + jnp.dot(a, w2_ref[c * tf:(c + 1) * tf, :], preferred_element_type=F32)
    x2 = x1 + g2_ref[0] * ff
    o_ref[0] = _rms(x2) * nfin_ref[...]


def _out_mlp(x, o_mla, o_swa, gates, g1, sc2, sh2, g2, norm_mlp, norm_final,
             w_o_mla, w_o_swa, w_out, w_ff1, w_ff2, tm, tf):
    B, S, D = x.shape
    tok = lambda w: pl.BlockSpec((1, tm, w), lambda b, i: (b, i, 0))
    per_batch = pl.BlockSpec((1, 1, D), lambda b, i: (b, 0, 0))
    return pl.pallas_call(
        functools.partial(_out_mlp_kernel, tf=tf),
        out_shape=jax.ShapeDtypeStruct((B, S, D), F32),
        grid=(B, S // tm),
        in_specs=[tok(D), tok(D), pl.BlockSpec((1, D, tm), lambda b, i: (b, 0, i)), tok(2 * D),
                  per_batch, per_batch, per_batch, per_batch,
                  _resident((1, D)), _resident((1, D)),
                  _resident(w_o_mla.shape), _resident(w_o_swa.shape), _resident(w_out.shape),
                  _resident(w_ff1.shape), _resident(w_ff2.shape)],
        out_specs=tok(D),
        compiler_params=_params(("arbitrary", "arbitrary")),
        name="out_mlp",
    )(x, o_mla, o_swa, gates, g1, sc2, sh2, g2, norm_mlp.reshape(1, D), norm_final.reshape(1, D),
      w_o_mla, w_o_swa, w_out, w_ff1, w_ff2)


def _pack_w_in(w_in):
    kr1 = Q_LORA + KV_LORA + MLA_ROPE
    qs1 = kr1 + SWA_HEADS * SWA_HEAD_DIM
    ks1 = qs1 + SWA_KV_HEADS * SWA_HEAD_DIM
    vs1 = ks1 + SWA_KV_HEADS * SWA_HEAD_DIM
    pad = jnp.zeros((w_in.shape[0], LANES - MLA_ROPE), w_in.dtype)
    w_tok = jnp.concatenate([w_in[:, :kr1], pad, w_in[:, qs1:ks1], w_in[:, vs1:]], axis=1)
    w_t = jnp.concatenate([w_in[:, kr1:qs1], w_in[:, ks1:vs1]], axis=1).T
    return w_tok.astype(BF16), w_t.astype(BF16)


def _split_heads(w, n_first):
    w3 = w.reshape(w.shape[0], MLA_HEADS, -1)
    first = w3[:, :, :n_first].reshape(w.shape[0], -1)
    second = w3[:, :, n_first:].reshape(w.shape[0], -1)
    return first.astype(BF16), second.astype(BF16)


def kernel(x, c, positions, w_ada, b_ada, norm_mix, w_in, q_norm, w_uq, kv_norm, w_ukv, rel_bias, sink,
           w_o_mla, w_o_swa, w_out, norm_mlp, w_ff1, w_ff2, norm_final):
    B, S, D = x.shape
    assert w_ada.shape[0] == 1, "single-layer block"
    assert D == D_MODEL and S % (4 * BLOCK) == 0 and S >= BLOCK + 2 * WINDOW

    mod = _ada_mod(c, w_ada[0], b_ada[0])
    sh1, sc1, g1, sh2, sc2, g2 = [m.reshape(B, 1, D) for m in jnp.split(mod, N_MOD, axis=-1)]

    rope_tab = _rope_table(positions)
    bias_tab = _swa_bias(rel_bias, sink[0])

    w_k, w_v = _split_heads(w_ukv[0], MLA_NOPE)
    w_qn, w_qr = _split_heads(w_uq[0], MLA_NOPE)
    w_in_tok, w_in_t = _pack_w_in(w_in[0])
    q_mla, k_mla, v_mla, qs_t, ks, vs_t, gates = _in_proj(
        x, norm_mix[0], sc1, sh1, rope_tab, w_in_tok, w_in_t, q_norm[0], w_qn, w_qr.T,
        kv_norm[0], w_k, w_v, tm=512)

    o_mla = _mla_attn(q_mla, k_mla, v_mla, tqs=4096, tq=512, tk=1024, tr=256)
    o_swa = _swa_attn(qs_t, ks, vs_t, bias_tab, sub_blocks=8, unit_heads=2, lookahead=2)

    return _out_mlp(x, o_mla, o_swa, gates, g1, sc2, sh2, g2, norm_mlp[0], norm_final,
                    w_o_mla[0].astype(BF16), w_o_swa[0].astype(BF16), w_out[0].astype(BF16),
                    w_ff1[0].astype(BF16), w_ff2[0].astype(BF16), tm=512, tf=1024)
```

```python
import functools
import math

import jax
import jax.numpy as jnp
from jax import lax
from jax.experimental import pallas as pl
from jax.experimental.pallas import tpu as pltpu

F32 = jnp.float32
BF16 = jnp.bfloat16

D_MODEL = 1024
MLA_HEADS = 8
MLA_NOPE = 128
MLA_ROPE = 64
MLA_V = 128
Q_LORA = 384
KV_LORA = 256
ROPE_THETA = 10000.0
SWA_HEADS = 8
SWA_KV_HEADS = 2
SWA_GROUP = SWA_HEADS // SWA_KV_HEADS
SWA_HEAD_DIM = 128
WINDOW = 128
BLOCK = 128
N_BUCKETS = 32
MAX_DISTANCE = 128
D_FF = 4 * D_MODEL
N_MOD = 6
EPS = 1e-6
NEG_INF = -1e30

ROPE_HALF = MLA_ROPE // 2
LANES = 128
BF16_ROWS = 16
MLA_QK_PAD = 256
MLA_SHIFT_COL = MLA_NOPE + MLA_ROPE
MLA_VT_ROWS = MLA_V + 16
SWA_VT_ROWS = SWA_HEAD_DIM + 16
VMEM_LIMIT = 56 * 1024 * 1024

LOG2_E = math.log2(math.e)
MLA_Q_SCALE = (MLA_NOPE + MLA_ROPE) ** -0.5 * LOG2_E
SWA_Q_SCALE = SWA_HEAD_DIM ** -0.5 * LOG2_E

C_CQ = 0
C_CKV = C_CQ + Q_LORA
C_KR = C_CKV + KV_LORA
C_KS = C_KR + LANES
C_G = C_KS + SWA_KV_HEADS * SWA_HEAD_DIM
C_END = C_G + 2 * D_MODEL

T5_LARGE_THRESHOLDS = (12, 16, 23, 32, 46, 64, 91)

NT_DIMS = (((1,), (1,)), ((), ()))
TN_DIMS = (((0,), (0,)), ((), ()))


def _resident(shape):
    nd = len(shape)
    return pl.BlockSpec(shape, lambda *_: (0,) * nd, pipeline_mode=pl.Buffered(1))


def _params(semantics):
    return pltpu.CompilerParams(dimension_semantics=semantics, vmem_limit_bytes=VMEM_LIMIT)


def _rms(x):
    return x * lax.rsqrt(jnp.mean(x * x, axis=-1, keepdims=True) + EPS)


def _ada_kernel(c_ref, w_ref, b_ref, o_ref):
    c = c_ref[...]
    c_act = c * jax.nn.sigmoid(c)
    o_ref[...] = jnp.dot(c_act, w_ref[...], preferred_element_type=F32,
                         precision=lax.Precision.HIGHEST) + b_ref[...]


def _ada_mod(c, w_ada, b_ada):
    B, D = c.shape
    N = w_ada.shape[1]
    tn = 1024
    return pl.pallas_call(
        _ada_kernel,
        out_shape=jax.ShapeDtypeStruct((B, N), F32),
        grid=(N // tn,),
        in_specs=[pl.BlockSpec((B, D), lambda j: (0, 0)),
                  pl.BlockSpec((D, tn), lambda j: (0, j)),
                  pl.BlockSpec((1, tn), lambda j: (0, j))],
        out_specs=pl.BlockSpec((B, tn), lambda j: (0, j)),
        compiler_params=_params(("arbitrary",)),
        name="ada_mod",
    )(c, w_ada, b_ada.reshape(1, N))


def _rope_table_kernel(pos_ref, inv_ref, o_ref):
    ang = inv_ref[...] * pos_ref[0].astype(F32)
    o_ref[0, :ROPE_HALF, :] = jnp.cos(ang)
    o_ref[0, ROPE_HALF:, :] = jnp.sin(ang)


def _rope_table(positions):
    B, S = positions.shape
    inv = ROPE_THETA ** (-jnp.arange(0, MLA_ROPE, 2, dtype=F32) / MLA_ROPE)
    return pl.pallas_call(
        _rope_table_kernel,
        out_shape=jax.ShapeDtypeStruct((B, MLA_ROPE, S), F32),
        grid=(B,),
        in_specs=[pl.BlockSpec((1, 1, S), lambda b: (b, 0, 0)),
                  pl.BlockSpec((ROPE_HALF, 1), lambda b: (0, 0))],
        out_specs=pl.BlockSpec((1, MLA_ROPE, S), lambda b: (b, 0, 0)),
        compiler_params=_params(("arbitrary",)),
        name="rope_table",
    )(positions.reshape(B, 1, S), inv.reshape(ROPE_HALF, 1))


def _swa_bias_kernel(tab_ref, sink_ref, o_ref):
    h = pl.program_id(0)
    half = N_BUCKETS // 2
    max_exact = half // 2
    kj = lax.broadcasted_iota(jnp.int32, (BLOCK, LANES), 0)
    qi = lax.broadcasted_iota(jnp.int32, (BLOCK, LANES), 1)
    for cb in range(5):
        rel = kj - qi + (cb * LANES - 2 * WINDOW)
        n = jnp.abs(rel)
        large = jnp.full_like(n, max_exact)
        for t in T5_LARGE_THRESHOLDS:
            large = large + jnp.where(n >= t, 1, 0)
        bucket = jnp.where(rel > 0, half, 0) + jnp.where(n < max_exact, n, large)
        bias = jnp.zeros((BLOCK, LANES), F32)
        for b in range(N_BUCKETS):
            bias = jnp.where(bucket == b, tab_ref[b, h], bias)
        o_ref[0, cb] = jnp.where(n <= WINDOW, (bias - sink_ref[0, h]) * LOG2_E, NEG_INF)


def _swa_bias(rel_bias, sink):
    return pl.pallas_call(
        _swa_bias_kernel,
        out_shape=jax.ShapeDtypeStruct((SWA_HEADS, 5, BLOCK, LANES), F32),
        grid=(SWA_HEADS,),
        in_specs=[pl.BlockSpec(memory_space=pltpu.SMEM), pl.BlockSpec(memory_space=pltpu.SMEM)],
        out_specs=pl.BlockSpec((1, 5, BLOCK, LANES), lambda h: (h, 0, 0, 0)),
        compiler_params=_params(("arbitrary",)),
        name="swa_bias",
    )(rel_bias, sink.reshape(1, SWA_HEADS))


def _rope_rows(x, tab):
    x1, x2 = x[:ROPE_HALF], x[ROPE_HALF:]
    cos, sin = tab[:ROPE_HALF], tab[ROPE_HALF:]
    return jnp.concatenate([x1 * cos - x2 * sin, x2 * cos + x1 * sin], axis=0)


def _in_proj_kernel(x_ref, nm_ref, sc_ref, sh_ref, tab_ref, win_ref, wqvt_ref, qn_ref, wuqt_ref, kvn_ref,
                    wk_ref, wvt_ref, qt_ref, km_ref, vt_ref, qst_ref, ks_ref, vst_ref, g_ref):
    x = x_ref[0]
    tm = x.shape[0]
    h = (_rms(x) * nm_ref[...] * (1.0 + sc_ref[0]) + sh_ref[0]).astype(BF16)
    tab = tab_ref[0]
    rope_pad = jnp.zeros((MLA_QK_PAD - MLA_NOPE - MLA_ROPE, tm), F32)

    lat = jnp.dot(h, win_ref[:, C_CQ:C_KS], preferred_element_type=F32)
    cq = lat[:, C_CQ:C_CKV]
    ckv = lat[:, C_CKV:C_KR]
    kr_t = lat[:, C_KR:C_KS].T
    one_row = jnp.where(lax.broadcasted_iota(jnp.int32, rope_pad.shape, 0) == 0, 1.0, 0.0)
    k_rope = jnp.concatenate([_rope_rows(kr_t[:MLA_ROPE], tab), one_row], axis=0).T.astype(BF16)

    cqn = (_rms(cq) * qn_ref[...]).astype(BF16)
    for hh in range(MLA_HEADS):
        r0 = hh * MLA_QK_PAD
        w0 = hh * (MLA_NOPE + MLA_ROPE)
        qh = lax.dot_general(wuqt_ref[w0:w0 + MLA_NOPE + MLA_ROPE, :], cqn, NT_DIMS, preferred_element_type=F32)
        qt_ref[0, 0, r0:r0 + MLA_NOPE, :] = (qh[:MLA_NOPE] * MLA_Q_SCALE).astype(BF16)
        q_rope = jnp.concatenate([_rope_rows(qh[MLA_NOPE:], tab) * MLA_Q_SCALE, rope_pad], axis=0)
        qt_ref[0, 0, r0 + MLA_NOPE:r0 + MLA_QK_PAD, :] = q_rope.astype(BF16)

    ckvn = (_rms(ckv) * kvn_ref[...]).astype(BF16)
    kn = jnp.dot(ckvn, wk_ref[...], preferred_element_type=F32)
    for hh in range(MLA_HEADS):
        c0 = hh * MLA_QK_PAD
        km_ref[0, :, c0:c0 + MLA_NOPE] = kn[:, hh * MLA_NOPE:(hh + 1) * MLA_NOPE].astype(BF16)
        km_ref[0, :, c0 + MLA_NOPE:c0 + MLA_QK_PAD] = k_rope
    vt = lax.dot_general(wvt_ref[...], ckvn, NT_DIMS, preferred_element_type=F32)
    mla_ones_rows = jnp.where(lax.broadcasted_iota(jnp.int32, (MLA_VT_ROWS - MLA_V, tm), 0) == 0,
                              1.0, 0.0).astype(BF16)
    for hh in range(MLA_HEADS):
        r0 = hh * MLA_VT_ROWS
        vt_ref[0, 0, r0:r0 + MLA_V, :] = vt[hh * MLA_V:(hh + 1) * MLA_V].astype(BF16)
        vt_ref[0, 0, r0 + MLA_V:r0 + MLA_VT_ROWS, :] = mla_ones_rows

    n_q = SWA_HEADS * SWA_HEAD_DIM
    qv_t = lax.dot_general(wqvt_ref[...], h, NT_DIMS, preferred_element_type=F32)
    qst_ref[0] = (qv_t[:n_q] * SWA_Q_SCALE).astype(BF16)
    ones_rows = jnp.where(lax.broadcasted_iota(jnp.int32, (SWA_VT_ROWS - SWA_HEAD_DIM, BLOCK), 0) == 0,
                          1.0, 0.0).astype(BF16)
    for n in range(SWA_KV_HEADS):
        vs_t = qv_t[n_q + n * SWA_HEAD_DIM:n_q + (n + 1) * SWA_HEAD_DIM].astype(BF16)
        for j in range(tm // BLOCK):
            vst_ref[0, j, n, :SWA_HEAD_DIM, :] = vs_t[:, j * BLOCK:(j + 1) * BLOCK]
            vst_ref[0, j, n, SWA_HEAD_DIM:, :] = ones_rows
    ks_ref[0] = jnp.dot(h, win_ref[:, C_KS:C_G], preferred_element_type=F32).astype(BF16)
    for j in range(2):
        c0 = C_G + j * D_MODEL
        g = jnp.dot(h, win_ref[:, c0:c0 + D_MODEL], preferred_element_type=F32)
        g_ref[0, :, j * D_MODEL:(j + 1) * D_MODEL] = jax.nn.sigmoid(g).astype(BF16)


def _in_proj(x, norm_mix, sc1, sh1, rope_tab, w_in_p, w_qv_t, q_norm, w_uq_t, kv_norm, w_k, w_v_t, tm):
    B, S, D = x.shape
    kvw = SWA_KV_HEADS * SWA_HEAD_DIM
    tok = lambda w: pl.BlockSpec((1, tm, w), lambda b, i: (b, i, 0))
    tok_t = lambda r: pl.BlockSpec((1, r, tm), lambda b, i: (b, 0, i))
    tile_t = lambda r: pl.BlockSpec((1, 1, r, tm), lambda b, i: (b, i, 0, 0))
    per_batch = pl.BlockSpec((1, 1, D), lambda b, i: (b, 0, 0))
    sds = lambda *shape: jax.ShapeDtypeStruct(shape, BF16)
    return pl.pallas_call(
        _in_proj_kernel,
        out_shape=[sds(B, S // tm, MLA_HEADS * MLA_QK_PAD, tm), sds(B, S, MLA_HEADS * MLA_QK_PAD),
                   sds(B, S // tm, MLA_HEADS * MLA_VT_ROWS, tm), sds(B, SWA_HEADS * SWA_HEAD_DIM, S),
                   sds(B, S, kvw), sds(B, S // BLOCK, SWA_KV_HEADS, SWA_VT_ROWS, BLOCK), sds(B, S, 2 * D)],
        grid=(B, S // tm),
        in_specs=[tok(D), _resident((1, D)), per_batch, per_batch, tok_t(MLA_ROPE),
                  _resident(w_in_p.shape), _resident(w_qv_t.shape), _resident((1, Q_LORA)),
                  _resident(w_uq_t.shape), _resident((1, KV_LORA)), _resident(w_k.shape),
                  _resident(w_v_t.shape)],
        out_specs=[tile_t(MLA_HEADS * MLA_QK_PAD), tok(MLA_HEADS * MLA_QK_PAD), tile_t(MLA_HEADS * MLA_VT_ROWS),
                   tok_t(SWA_HEADS * SWA_HEAD_DIM), tok(kvw),
                   pl.BlockSpec((1, tm // BLOCK, SWA_KV_HEADS, SWA_VT_ROWS, BLOCK), lambda b, i: (b, i, 0, 0, 0)),
                   tok(2 * D)],
        compiler_params=_params(("arbitrary", "arbitrary")),
        name="in_proj",
    )(x, norm_mix.reshape(1, D), sc1, sh1, rope_tab, w_in_p, w_qv_t, q_norm.reshape(1, Q_LORA), w_uq_t,
      kv_norm.reshape(1, KV_LORA), w_k, w_v_t)


def _mla_kernel(qt_ref, k_ref, vt_ref, o_ref, *, tk, tr, lookahead):
    n_tiles, _, tq = qt_ref.shape[1:]
    S = k_ref.shape[1]
    n_chunks = S // tk
    tiles_per_chunk = tk // tq
    g0 = MLA_SHIFT_COL // BF16_ROWS * BF16_ROWS
    row = lax.broadcasted_iota(jnp.int32, (BF16_ROWS, tq), 0)

    def shifted_q(i):
        qt = qt_ref[0, i]
        s = jnp.dot(k_ref[0, :tr, :], qt, preferred_element_type=F32)
        ref = jnp.max(s, axis=0, keepdims=True)
        grp = jnp.where(row == MLA_SHIFT_COL - g0, -ref, qt[g0:g0 + BF16_ROWS].astype(F32)).astype(BF16)
        return jnp.concatenate([qt[:g0], grp, qt[g0 + BF16_ROWS:]], axis=0)

    def probs(item, qts):
        i, c = item
        s = jnp.dot(k_ref[0, c * tk:(c + 1) * tk, :], qts[i], preferred_element_type=F32)
        return jnp.exp2(s).astype(BF16)

    def values_t(c):
        return jnp.concatenate([vt_ref[0, c * tiles_per_chunk + j] for j in range(tiles_per_chunk)], axis=1)

    items = [(i, c) for i in range(n_tiles) for c in range(n_chunks)]
    qts = {0: shifted_q(0)}
    ahead = [probs(item, qts) for item in items[:lookahead]]
    o_sum = jnp.zeros((MLA_V, tq), F32)
    for idx, (i, c) in enumerate(items):
        if c == 0 and i + 1 < n_tiles:
            qts[i + 1] = shifted_q(i + 1)
        p = ahead.pop(0)
        if idx + lookahead < len(items):
            ahead.append(probs(items[idx + lookahead], qts))
        pv = jnp.dot(values_t(c), p, preferred_element_type=F32)
        acc = pv if c == 0 else acc + pv
        if c == n_chunks - 1:
            o = acc[:MLA_V] / acc[MLA_V:MLA_V + 1]
            o_ref[0, i] = o.astype(BF16)
            o_sum = o_sum + o
            qts.pop(i)
    bad = jnp.max(jnp.where(jnp.isfinite(o_sum), 0.0, 1.0))

    @pl.when(bad > 0.0)
    def _recompute_with_running_max():
        def tile_body(i, carry):
            qt = qt_ref[0, i]

            def key_tile_body(j, state):
                m, l, acc = state
                k = k_ref[0, pl.ds(pl.multiple_of(j * tq, tq), tq), :]
                s = jnp.dot(k, qt, preferred_element_type=F32)
                m_new = jnp.maximum(m, jnp.max(s, axis=0, keepdims=True))
                p = jnp.exp2(s - m_new)
                alpha = jnp.exp2(m - m_new)
                l = alpha * l + jnp.sum(p, axis=0, keepdims=True)
                acc = alpha * acc + jnp.dot(vt_ref[0, j, :MLA_V, :], p.astype(BF16), preferred_element_type=F32)
                return m_new, l, acc

            init = (jnp.full((1, tq), -jnp.inf, F32), jnp.zeros((1, tq), F32), jnp.zeros((MLA_V, tq), F32))
            _, l, acc = lax.fori_loop(0, n_tiles, key_tile_body, init)
            o_ref[0, i] = (acc / l).astype(BF16)
            return carry

        lax.fori_loop(0, n_tiles, tile_body, 0)


def _mla_attn(qt_mla, k_mla, vt_mla, tk, tr, lookahead):
    B, n_tiles, _, tq = qt_mla.shape
    S = k_mla.shape[1]
    return pl.pallas_call(
        functools.partial(_mla_kernel, tk=tk, tr=tr, lookahead=lookahead),
        out_shape=jax.ShapeDtypeStruct((B, n_tiles, MLA_HEADS * MLA_V, tq), BF16),
        grid=(B, MLA_HEADS),
        in_specs=[pl.BlockSpec((1, n_tiles, MLA_QK_PAD, tq), lambda b, h: (b, 0, h, 0)),
                  pl.BlockSpec((1, S, MLA_QK_PAD), lambda b, h: (b, 0, h)),
                  pl.BlockSpec((1, n_tiles, MLA_VT_ROWS, tq), lambda b, h: (b, 0, h, 0))],
        out_specs=pl.BlockSpec((1, n_tiles, MLA_V, tq), lambda b, h: (b, 0, h, 0)),
        compiler_params=_params(("arbitrary", "arbitrary")),
        name="mla_attn",
    )(qt_mla, k_mla, vt_mla)


def _swa_kernel(qt_ref, k_ref, vt_ref, t_ref, o_ref, *, sub_blocks, unit_heads, lookahead):
    S = k_ref.shape[1]
    span = BLOCK + 2 * WINDOW
    n_win = span // BLOCK
    step = pl.program_id(1)
    units = [(sb, hd0) for sb in range(sub_blocks) for hd0 in range(0, SWA_HEADS, unit_heads)]

    def window(sb):
        q0 = (step * sub_blocks + sb) * BLOCK
        start = pl.multiple_of(jnp.clip(q0 - WINDOW, 0, S - span), BLOCK)
        cb0 = jnp.where(q0 == 0, 2, jnp.where(q0 == S - BLOCK, 0, 1))
        return start, cb0

    def scores(sb, hd0):
        start, cb0 = window(sb)
        n = hd0 // SWA_GROUP
        heads = range(hd0, hd0 + unit_heads)
        qt = jnp.concatenate(
            [qt_ref[0, hd * SWA_HEAD_DIM:(hd + 1) * SWA_HEAD_DIM, sb * BLOCK:(sb + 1) * BLOCK] for hd in heads],
            axis=1)
        kw = k_ref[0, pl.ds(start, span), n * SWA_HEAD_DIM:(n + 1) * SWA_HEAD_DIM]
        bias = jnp.concatenate(
            [jnp.concatenate([t_ref[hd, cb0 + c] for c in range(n_win)], axis=0) for hd in heads], axis=1)
        return jnp.dot(kw, qt, preferred_element_type=F32) + bias

    def values_t(sb, hd0):
        start, _ = window(sb)
        blk0 = start // BLOCK
        return jnp.concatenate([vt_ref[0, blk0 + c, hd0 // SWA_GROUP] for c in range(n_win)], axis=1)

    def store(sb, hd0, o):
        for g in range(unit_heads):
            hd = hd0 + g
            o_ref[0, hd * SWA_HEAD_DIM:(hd + 1) * SWA_HEAD_DIM, sb * BLOCK:(sb + 1) * BLOCK] = (
                o[:, g * BLOCK:(g + 1) * BLOCK].astype(BF16))

    o_sum = jnp.zeros((SWA_HEAD_DIM, unit_heads * BLOCK), F32)
    probs = lambda unit: jnp.exp2(scores(*unit)).astype(BF16)
    ahead = [probs(unit) for unit in units[:lookahead]]
    for u, unit in enumerate(units):
        p = ahead.pop(0)
        if u + lookahead < len(units):
            ahead.append(probs(units[u + lookahead]))
        ov = jnp.dot(values_t(*unit), p, preferred_element_type=F32)
        o = ov[:SWA_HEAD_DIM] / (ov[SWA_HEAD_DIM:SWA_HEAD_DIM + 1] + 1.0)
        store(*unit, o)
        o_sum = o_sum + o
    bad = jnp.max(jnp.where(jnp.isfinite(o_sum), 0.0, 1.0))

    @pl.when(bad > 0.0)
    def _recompute_with_row_max():
        for unit in units:
            s = scores(*unit)
            m = jnp.maximum(jnp.max(s, axis=0, keepdims=True), 0.0)
            p = jnp.exp2(s - m)
            l = jnp.sum(p, axis=0, keepdims=True) + jnp.exp2(-m)
            ov = jnp.dot(values_t(*unit), p.astype(BF16), preferred_element_type=F32)
            store(*unit, ov[:SWA_HEAD_DIM] / l)


def _swa_attn(qs_t, ks, vs_t, bias_tab, sub_blocks, unit_heads, lookahead):
    B, W, S = qs_t.shape
    kvw = ks.shape[2]
    tq = sub_blocks * BLOCK
    return pl.pallas_call(
        functools.partial(_swa_kernel, sub_blocks=sub_blocks, unit_heads=unit_heads, lookahead=lookahead),
        out_shape=jax.ShapeDtypeStruct((B, W, S), BF16),
        grid=(B, S // tq),
        in_specs=[pl.BlockSpec((1, W, tq), lambda b, i: (b, 0, i)),
                  pl.BlockSpec((1, S, kvw), lambda b, i: (b, 0, 0)),
                  pl.BlockSpec((1, S // BLOCK, SWA_KV_HEADS, SWA_VT_ROWS, BLOCK), lambda b, i: (b, 0, 0, 0, 0)),
                  _resident(bias_tab.shape)],
        out_specs=pl.BlockSpec((1, W, tq), lambda b, i: (b, 0, i)),
        compiler_params=_params(("arbitrary", "arbitrary")),
        name="swa_attn",
    )(qs_t, ks, vs_t, bias_tab)


def _out_mlp_kernel(x_ref, oat_ref, obt_ref, g_ref, g1_ref, sc_ref, sh_ref, g2_ref, nmlp_ref, nfin_ref,
                    woa_ref, wob_ref, wout_ref, w1_ref, w2_ref, o_ref, *, tf):
    D = x_ref.shape[2]
    y_a = lax.dot_general(oat_ref[0, 0], woa_ref[...], TN_DIMS, preferred_element_type=F32)
    y_b = lax.dot_general(obt_ref[0], wob_ref[...], TN_DIMS, preferred_element_type=F32)
    merged = g_ref[0, :, :D].astype(F32) * y_a + g_ref[0, :, D:].astype(F32) * y_b
    att = jnp.dot(merged.astype(BF16), wout_ref[...], preferred_element_type=F32)
    x1 = x_ref[0] + g1_ref[0] * att

    h = (_rms(x1) * nmlp_ref[...] * (1.0 + sc_ref[0]) + sh_ref[0]).astype(BF16)
    ff = jnp.zeros_like(x1)
    for c in range(w1_ref.shape[1] // tf):
        a = jnp.dot(h, w1_ref[:, c * tf:(c + 1) * tf], preferred_element_type=F32)
        a = jnp.square(jnp.maximum(a, 0.0)).astype(BF16)
        ff = ff + jnp.dot(a, w2_ref[c * tf:(c + 1) * tf, :], preferred_element_type=F32)
    x2 = x1 + g2_ref[0] * ff
    o_ref[0] = _rms(x2) * nfin_ref[...]


def _out_mlp(x, o_mla, o_swa, gates, g1, sc2, sh2, g2, norm_mlp, norm_final,
             w_o_mla, w_o_swa, w_out, w_ff1, w_ff2, tm, tf):
    B, S, D = x.shape
    tok = lambda w: pl.BlockSpec((1, tm, w), lambda b, i: (b, i, 0))
    per_batch = pl.BlockSpec((1, 1, D), lambda b, i: (b, 0, 0))
    return pl.pallas_call(
        functools.partial(_out_mlp_kernel, tf=tf),
        out_shape=jax.ShapeDtypeStruct((B, S, D), F32),
        grid=(B, S // tm),
        in_specs=[tok(D), pl.BlockSpec((1, 1, D, tm), lambda b, i: (b, i, 0, 0)),
                  pl.BlockSpec((1, D, tm), lambda b, i: (b, 0, i)), tok(2 * D),
                  per_batch, per_batch, per_batch, per_batch,
                  _resident((1, D)), _resident((1, D)),
                  _resident(w_o_mla.shape), _resident(w_o_swa.shape), _resident(w_out.shape),
                  _resident(w_ff1.shape), _resident(w_ff2.shape)],
        out_specs=tok(D),
        compiler_params=_params(("arbitrary", "arbitrary")),
        name="out_mlp",
    )(x, o_mla, o_swa, gates, g1, sc2, sh2, g2, norm_mlp.reshape(1, D), norm_final.reshape(1, D),
      w_o_mla, w_o_swa, w_out, w_ff1, w_ff2)


def _pack_w_in(w_in):
    kr1 = Q_LORA + KV_LORA + MLA_ROPE
    qs1 = kr1 + SWA_HEADS * SWA_HEAD_DIM
    ks1 = qs1 + SWA_KV_HEADS * SWA_HEAD_DIM
    vs1 = ks1 + SWA_KV_HEADS * SWA_HEAD_DIM
    pad = jnp.zeros((w_in.shape[0], LANES - MLA_ROPE), w_in.dtype)
    w_tok = jnp.concatenate([w_in[:, :kr1], pad, w_in[:, qs1:ks1], w_in[:, vs1:]], axis=1)
    w_t = jnp.concatenate([w_in[:, kr1:qs1], w_in[:, ks1:vs1]], axis=1).T
    return w_tok.astype(BF16), w_t.astype(BF16)


def _split_heads(w, n_first):
    w3 = w.reshape(w.shape[0], MLA_HEADS, -1)
    first = w3[:, :, :n_first].reshape(w.shape[0], -1)
    second = w3[:, :, n_first:].reshape(w.shape[0], -1)
    return first.astype(BF16), second.astype(BF16)


def kernel(x, c, positions, w_ada, b_ada, norm_mix, w_in, q_norm, w_uq, kv_norm, w_ukv, rel_bias, sink,
           w_o_mla, w_o_swa, w_out, norm_mlp, w_ff1, w_ff2, norm_final):
    B, S, D = x.shape
    assert w_ada.shape[0] == 1, "single-layer block"
    assert D == D_MODEL and S % (4 * BLOCK) == 0 and S >= BLOCK + 2 * WINDOW

    mod = _ada_mod(c, w_ada[0], b_ada[0])
    sh1, sc1, g1, sh2, sc2, g2 = [m.reshape(B, 1, D) for m in jnp.split(mod, N_MOD, axis=-1)]

    rope_tab = _rope_table(positions)
    bias_tab = _swa_bias(rel_bias, sink[0])

    tm = 512
    w_k, w_v = _split_heads(w_ukv[0], MLA_NOPE)
    w_in_tok, w_in_t = _pack_w_in(w_in[0])
    qt_mla, k_mla, vt_mla, qs_t, ks, vs_t, gates = _in_proj(
        x, norm_mix[0], sc1, sh1, rope_tab, w_in_tok, w_in_t, q_norm[0], w_uq[0].T.astype(BF16),
        kv_norm[0], w_k, w_v.T, tm=tm)

    ot_mla = _mla_attn(qt_mla, k_mla, vt_mla, tk=1024, tr=256, lookahead=1)
    ot_swa = _swa_attn(qs_t, ks, vs_t, bias_tab, sub_blocks=8, unit_heads=2, lookahead=2)

    return _out_mlp(x, ot_mla, ot_swa, gates, g1, sc2, sh2, g2, norm_mlp[0], norm_final,
                    w_o_mla[0].astype(BF16), w_o_swa[0].astype(BF16), w_out[0].astype(BF16),
                    w_ff1[0].astype(BF16), w_ff2[0].astype(BF16), tm=tm, tf=1024)
```

```python
import functools
import math

import jax
import jax.numpy as jnp
from jax import lax
from jax.experimental import pallas as pl
from jax.experimental.pallas import tpu as pltpu

F32 = jnp.float32
BF16 = jnp.bfloat16

D_MODEL = 1024
MLA_HEADS = 8
MLA_NOPE = 128
MLA_ROPE = 64
MLA_V = 128
Q_LORA = 384
KV_LORA = 256
ROPE_THETA = 10000.0
SWA_HEADS = 8
SWA_KV_HEADS = 2
SWA_GROUP = SWA_HEADS // SWA_KV_HEADS
SWA_HEAD_DIM = 128
WINDOW = 128
BLOCK = 128
N_BUCKETS = 32
MAX_DISTANCE = 128
D_FF = 4 * D_MODEL
N_MOD = 6
EPS = 1e-6
NEG_INF = -1e30

ROPE_HALF = MLA_ROPE // 2
LANES = 128
BF16_ROWS = 16
MLA_QK_PAD = 256
MLA_SHIFT_COL = MLA_NOPE + MLA_ROPE
MLA_VT_ROWS = MLA_V + 16
SWA_VT_ROWS = SWA_HEAD_DIM + 16
VMEM_LIMIT = 56 * 1024 * 1024

LOG2_E = math.log2(math.e)
MLA_Q_SCALE = (MLA_NOPE + MLA_ROPE) ** -0.5 * LOG2_E
SWA_Q_SCALE = SWA_HEAD_DIM ** -0.5 * LOG2_E

C_CQ = 0
C_CKV = C_CQ + Q_LORA
C_KR = C_CKV + KV_LORA
C_KS = C_KR + LANES
C_G = C_KS + SWA_KV_HEADS * SWA_HEAD_DIM
C_END = C_G + 2 * D_MODEL

T5_LARGE_THRESHOLDS = (12, 16, 23, 32, 46, 64, 91)

NT_DIMS = (((1,), (1,)), ((), ()))
TN_DIMS = (((0,), (0,)), ((), ()))


def _resident(shape):
    nd = len(shape)
    return pl.BlockSpec(shape, lambda *_: (0,) * nd, pipeline_mode=pl.Buffered(1))


def _params(semantics):
    return pltpu.CompilerParams(dimension_semantics=semantics, vmem_limit_bytes=VMEM_LIMIT)


def _rms(x):
    return x * lax.rsqrt(jnp.mean(x * x, axis=-1, keepdims=True) + EPS)


def _ada_kernel(c_ref, w_ref, b_ref, o_ref):
    c = c_ref[...]
    c_act = c * jax.nn.sigmoid(c)
    o_ref[...] = jnp.dot(c_act, w_ref[...], preferred_element_type=F32,
                         precision=lax.Precision.HIGHEST) + b_ref[...]


def _ada_mod(c, w_ada, b_ada):
    B, D = c.shape
    N = w_ada.shape[1]
    tn = 1024
    return pl.pallas_call(
        _ada_kernel,
        out_shape=jax.ShapeDtypeStruct((B, N), F32),
        grid=(N // tn,),
        in_specs=[pl.BlockSpec((B, D), lambda j: (0, 0)),
                  pl.BlockSpec((D, tn), lambda j: (0, j)),
                  pl.BlockSpec((1, tn), lambda j: (0, j))],
        out_specs=pl.BlockSpec((B, tn), lambda j: (0, j)),
        compiler_params=_params(("arbitrary",)),
        name="ada_mod",
    )(c, w_ada, b_ada.reshape(1, N))


def _swa_bias_kernel(tab_ref, sink_ref, o_ref):
    h = pl.program_id(0)
    half = N_BUCKETS // 2
    max_exact = half // 2
    kj = lax.broadcasted_iota(jnp.int32, (BLOCK, LANES), 0)
    qi = lax.broadcasted_iota(jnp.int32, (BLOCK, LANES), 1)
    for cb in range(5):
        rel = kj - qi + (cb * LANES - 2 * WINDOW)
        n = jnp.abs(rel)
        large = jnp.full_like(n, max_exact)
        for t in T5_LARGE_THRESHOLDS:
            large = large + jnp.where(n >= t, 1, 0)
        bucket = jnp.where(rel > 0, half, 0) + jnp.where(n < max_exact, n, large)
        bias = jnp.zeros((BLOCK, LANES), F32)
        for b in range(N_BUCKETS):
            bias = jnp.where(bucket == b, tab_ref[b, h], bias)
        o_ref[0, cb] = jnp.where(n <= WINDOW, (bias - sink_ref[0, h]) * LOG2_E, NEG_INF)


def _swa_bias(rel_bias, sink):
    return pl.pallas_call(
        _swa_bias_kernel,
        out_shape=jax.ShapeDtypeStruct((SWA_HEADS, 5, BLOCK, LANES), F32),
        grid=(SWA_HEADS,),
        in_specs=[pl.BlockSpec(memory_space=pltpu.SMEM), pl.BlockSpec(memory_space=pltpu.SMEM)],
        out_specs=pl.BlockSpec((1, 5, BLOCK, LANES), lambda h: (h, 0, 0, 0)),
        compiler_params=_params(("arbitrary",)),
        name="swa_bias",
    )(rel_bias, sink.reshape(1, SWA_HEADS))


def _rope_rows(x, tab):
    x1, x2 = x[:ROPE_HALF], x[ROPE_HALF:]
    cos, sin = tab[:ROPE_HALF], tab[ROPE_HALF:]
    return jnp.concatenate([x1 * cos - x2 * sin, x2 * cos + x1 * sin], axis=0)


def _in_proj_kernel(x_ref, nm_ref, sc_ref, sh_ref, pos_ref, inv_ref, win_ref, wqvt_ref, qn_ref, wuqt_ref, kvn_ref,
                    wk_ref, wvt_ref, qt_ref, km_ref, vt_ref, qst_ref, ks_ref, vst_ref, g_ref, *, n_groups):
    tm = x_ref.shape[1]
    tg = tm // n_groups
    n_q = SWA_HEADS * SWA_HEAD_DIM
    rope_pad = jnp.zeros((MLA_QK_PAD - MLA_NOPE - MLA_ROPE, tg), F32)
    one_row = jnp.where(lax.broadcasted_iota(jnp.int32, rope_pad.shape, 0) == 0, 1.0, 0.0)
    mla_ones_rows = jnp.where(lax.broadcasted_iota(jnp.int32, (MLA_VT_ROWS - MLA_V, tg), 0) == 0,
                              1.0, 0.0).astype(BF16)
    swa_ones_rows = jnp.where(lax.broadcasted_iota(jnp.int32, (SWA_VT_ROWS - SWA_HEAD_DIM, BLOCK), 0) == 0,
                              1.0, 0.0).astype(BF16)

    def token_group(r0):
        rows = slice(r0, r0 + tg)
        h = (_rms(x_ref[0, rows, :]) * nm_ref[...] * (1.0 + sc_ref[0]) + sh_ref[0]).astype(BF16)
        ang = inv_ref[...] * pos_ref[0, :, rows].astype(F32)
        tab = jnp.concatenate([jnp.cos(ang), jnp.sin(ang)], axis=0)

        def gate(j):
            c0 = C_G + j * D_MODEL
            g = jnp.dot(h, win_ref[:, c0:c0 + D_MODEL], preferred_element_type=F32)
            g_ref[0, rows, j * D_MODEL:(j + 1) * D_MODEL] = jax.nn.sigmoid(g).astype(BF16)

        lat = jnp.dot(h, win_ref[:, C_CQ:C_KS], preferred_element_type=F32)
        gate(0)
        cq = lat[:, C_CQ:C_CKV]
        ckv = lat[:, C_CKV:C_KR]
        kr_t = lat[:, C_KR:C_KS].T
        k_rope = jnp.concatenate([_rope_rows(kr_t[:MLA_ROPE], tab), one_row], axis=0).T.astype(BF16)

        cqn = (_rms(cq) * qn_ref[...]).astype(BF16)
        q_t = lax.dot_general(wuqt_ref[...], cqn, NT_DIMS, preferred_element_type=F32)
        for hh in range(MLA_HEADS):
            q0 = hh * MLA_QK_PAD
            qh = q_t[hh * (MLA_NOPE + MLA_ROPE):(hh + 1) * (MLA_NOPE + MLA_ROPE)]
            qt_ref[0, 0, q0:q0 + MLA_NOPE, rows] = (qh[:MLA_NOPE] * MLA_Q_SCALE).astype(BF16)
            q_rope = jnp.concatenate([_rope_rows(qh[MLA_NOPE:], tab) * MLA_Q_SCALE, rope_pad], axis=0)
            qt_ref[0, 0, q0 + MLA_NOPE:q0 + MLA_QK_PAD, rows] = q_rope.astype(BF16)

        ckvn = (_rms(ckv) * kvn_ref[...]).astype(BF16)
        kn = jnp.dot(ckvn, wk_ref[...], preferred_element_type=F32)
        for hh in range(MLA_HEADS):
            c0 = hh * MLA_QK_PAD
            km_ref[0, rows, c0:c0 + MLA_NOPE] = kn[:, hh * MLA_NOPE:(hh + 1) * MLA_NOPE].astype(BF16)
            km_ref[0, rows, c0 + MLA_NOPE:c0 + MLA_QK_PAD] = k_rope
        vt = lax.dot_general(wvt_ref[...], ckvn, NT_DIMS, preferred_element_type=F32)
        for hh in range(MLA_HEADS):
            v0 = hh * MLA_VT_ROWS
            vt_ref[0, 0, v0:v0 + MLA_V, rows] = vt[hh * MLA_V:(hh + 1) * MLA_V].astype(BF16)
            vt_ref[0, 0, v0 + MLA_V:v0 + MLA_VT_ROWS, rows] = mla_ones_rows

        qv_t = lax.dot_general(wqvt_ref[...], h, NT_DIMS, preferred_element_type=F32)
        qst_ref[0, :, rows] = (qv_t[:n_q] * SWA_Q_SCALE).astype(BF16)
        for n in range(SWA_KV_HEADS):
            vs_t = qv_t[n_q + n * SWA_HEAD_DIM:n_q + (n + 1) * SWA_HEAD_DIM].astype(BF16)
            for j in range(tg // BLOCK):
                blk = r0 // BLOCK + j
                vst_ref[0, blk, n, :SWA_HEAD_DIM, :] = vs_t[:, j * BLOCK:(j + 1) * BLOCK]
                vst_ref[0, blk, n, SWA_HEAD_DIM:, :] = swa_ones_rows
        gate(1)
        ks_ref[0, rows, :] = jnp.dot(h, win_ref[:, C_KS:C_G], preferred_element_type=F32).astype(BF16)

    for r0 in range(0, tm, tg):
        token_group(r0)


def _in_proj(x, norm_mix, sc1, sh1, positions, w_in_p, w_qv_t, q_norm, w_uq_t, kv_norm, w_k, w_v_t, tm, n_groups):
    B, S, D = x.shape
    inv_freq = ROPE_THETA ** (-jnp.arange(0, MLA_ROPE, 2, dtype=F32) / MLA_ROPE)
    kvw = SWA_KV_HEADS * SWA_HEAD_DIM
    tok = lambda w: pl.BlockSpec((1, tm, w), lambda b, i: (b, i, 0))
    tok_t = lambda r: pl.BlockSpec((1, r, tm), lambda b, i: (b, 0, i))
    tile_t = lambda r: pl.BlockSpec((1, 1, r, tm), lambda b, i: (b, i, 0, 0))
    per_batch = pl.BlockSpec((1, 1, D), lambda b, i: (b, 0, 0))
    sds = lambda *shape: jax.ShapeDtypeStruct(shape, BF16)
    return pl.pallas_call(
        functools.partial(_in_proj_kernel, n_groups=n_groups),
        out_shape=[sds(B, S // tm, MLA_HEADS * MLA_QK_PAD, tm), sds(B, S, MLA_HEADS * MLA_QK_PAD),
                   sds(B, S // tm, MLA_HEADS * MLA_VT_ROWS, tm), sds(B, SWA_HEADS * SWA_HEAD_DIM, S),
                   sds(B, S, kvw), sds(B, S // BLOCK, SWA_KV_HEADS, SWA_VT_ROWS, BLOCK), sds(B, S, 2 * D)],
        grid=(B, S // tm),
        in_specs=[tok(D), _resident((1, D)), per_batch, per_batch, tok_t(1), _resident((ROPE_HALF, 1)),
                  _resident(w_in_p.shape), _resident(w_qv_t.shape), _resident((1, Q_LORA)),
                  _resident(w_uq_t.shape), _resident((1, KV_LORA)), _resident(w_k.shape),
                  _resident(w_v_t.shape)],
        out_specs=[tile_t(MLA_HEADS * MLA_QK_PAD), tok(MLA_HEADS * MLA_QK_PAD), tile_t(MLA_HEADS * MLA_VT_ROWS),
                   tok_t(SWA_HEADS * SWA_HEAD_DIM), tok(kvw),
                   pl.BlockSpec((1, tm // BLOCK, SWA_KV_HEADS, SWA_VT_ROWS, BLOCK), lambda b, i: (b, i, 0, 0, 0)),
                   tok(2 * D)],
        compiler_params=_params(("arbitrary", "arbitrary")),
        name="in_proj",
    )(x, norm_mix.reshape(1, D), sc1, sh1, positions.reshape(B, 1, S), inv_freq.reshape(ROPE_HALF, 1),
      w_in_p, w_qv_t, q_norm.reshape(1, Q_LORA), w_uq_t, kv_norm.reshape(1, KV_LORA), w_k, w_v_t)


def _mla_kernel(qt_ref, k_ref, vt_ref, o_ref, *, tk, tr, lookahead):
    n_tiles, _, tq = qt_ref.shape[1:]
    S = k_ref.shape[1]
    n_chunks = S // tk
    tiles_per_chunk = tk // tq
    g0 = MLA_SHIFT_COL // BF16_ROWS * BF16_ROWS
    row = lax.broadcasted_iota(jnp.int32, (BF16_ROWS, tq), 0)

    def shifted_q(i):
        qt = qt_ref[0, i]
        s = jnp.dot(k_ref[0, :tr, :], qt, preferred_element_type=F32)
        ref = jnp.max(s, axis=0, keepdims=True)
        grp = jnp.where(row == MLA_SHIFT_COL - g0, -ref, qt[g0:g0 + BF16_ROWS].astype(F32)).astype(BF16)
        return jnp.concatenate([qt[:g0], grp, qt[g0 + BF16_ROWS:]], axis=0)

    def probs(item, qts):
        i, c = item
        s = jnp.dot(k_ref[0, c * tk:(c + 1) * tk, :], qts[i], preferred_element_type=F32)
        return jnp.exp2(s).astype(BF16)

    def values_t(c):
        return jnp.concatenate([vt_ref[0, c * tiles_per_chunk + j] for j in range(tiles_per_chunk)], axis=1)

    items = [(i, c) for i in range(n_tiles) for c in range(n_chunks)]
    qts = {0: shifted_q(0)}
    ahead = [probs(item, qts) for item in items[:lookahead]]
    o_sum = jnp.zeros((MLA_V, tq), F32)
    for idx, (i, c) in enumerate(items):
        if c == 0 and i + 1 < n_tiles:
            qts[i + 1] = shifted_q(i + 1)
        p = ahead.pop(0)
        if idx + lookahead < len(items):
            ahead.append(probs(items[idx + lookahead], qts))
        pv = jnp.dot(values_t(c), p, preferred_element_type=F32)
        acc = pv if c == 0 else acc + pv
        if c == n_chunks - 1:
            o = acc[:MLA_V] / acc[MLA_V:MLA_V + 1]
            o_ref[0, i] = o.astype(BF16)
            o_sum = o_sum + o
            qts.pop(i)
    bad = jnp.max(jnp.where(jnp.isfinite(o_sum), 0.0, 1.0))

    @pl.when(bad > 0.0)
    def _recompute_with_running_max():
        def tile_body(i, carry):
            qt = qt_ref[0, i]

            def key_tile_body(j, state):
                m, l, acc = state
                k = k_ref[0, pl.ds(pl.multiple_of(j * tq, tq), tq), :]
                s = jnp.dot(k, qt, preferred_element_type=F32)
                m_new = jnp.maximum(m, jnp.max(s, axis=0, keepdims=True))
                p = jnp.exp2(s - m_new)
                alpha = jnp.exp2(m - m_new)
                l = alpha * l + jnp.sum(p, axis=0, keepdims=True)
                acc = alpha * acc + jnp.dot(vt_ref[0, j, :MLA_V, :], p.astype(BF16), preferred_element_type=F32)
                return m_new, l, acc

            init = (jnp.full((1, tq), -jnp.inf, F32), jnp.zeros((1, tq), F32), jnp.zeros((MLA_V, tq), F32))
            _, l, acc = lax.fori_loop(0, n_tiles, key_tile_body, init)
            o_ref[0, i] = (acc / l).astype(BF16)
            return carry

        lax.fori_loop(0, n_tiles, tile_body, 0)


def _mla_attn(qt_mla, k_mla, vt_mla, tk, tr, lookahead):
    B, n_tiles, _, tq = qt_mla.shape
    S = k_mla.shape[1]
    return pl.pallas_call(
        functools.partial(_mla_kernel, tk=tk, tr=tr, lookahead=lookahead),
        out_shape=jax.ShapeDtypeStruct((B, n_tiles, MLA_HEADS * MLA_V, tq), BF16),
        grid=(B, MLA_HEADS),
        in_specs=[pl.BlockSpec((1, n_tiles, MLA_QK_PAD, tq), lambda b, h: (b, 0, h, 0)),
                  pl.BlockSpec((1, S, MLA_QK_PAD), lambda b, h: (b, 0, h)),
                  pl.BlockSpec((1, n_tiles, MLA_VT_ROWS, tq), lambda b, h: (b, 0, h, 0))],
        out_specs=pl.BlockSpec((1, n_tiles, MLA_V, tq), lambda b, h: (b, 0, h, 0)),
        compiler_params=_params(("arbitrary", "arbitrary")),
        name="mla_attn",
    )(qt_mla, k_mla, vt_mla)


def _swa_kernel(qt_ref, k_ref, vt_ref, t_ref, o_ref, *, sub_blocks, unit_heads, lookahead):
    S = k_ref.shape[1]
    span = BLOCK + 2 * WINDOW
    n_win = span // BLOCK
    step = pl.program_id(1)
    units = [(sb, hd0) for sb in range(sub_blocks) for hd0 in range(0, SWA_HEADS, unit_heads)]

    def window(sb):
        q0 = (step * sub_blocks + sb) * BLOCK
        start = pl.multiple_of(jnp.clip(q0 - WINDOW, 0, S - span), BLOCK)
        cb0 = jnp.where(q0 == 0, 2, jnp.where(q0 == S - BLOCK, 0, 1))
        return start, cb0

    def scores(sb, hd0):
        start, cb0 = window(sb)
        n = hd0 // SWA_GROUP
        heads = range(hd0, hd0 + unit_heads)
        qt = jnp.concatenate(
            [qt_ref[0, hd * SWA_HEAD_DIM:(hd + 1) * SWA_HEAD_DIM, sb * BLOCK:(sb + 1) * BLOCK] for hd in heads],
            axis=1)
        kw = k_ref[0, pl.ds(start, span), n * SWA_HEAD_DIM:(n + 1) * SWA_HEAD_DIM]
        bias = jnp.concatenate(
            [jnp.concatenate([t_ref[hd, cb0 + c] for c in range(n_win)], axis=0) for hd in heads], axis=1)
        return jnp.dot(kw, qt, preferred_element_type=F32) + bias

    def values_t(sb, hd0):
        start, _ = window(sb)
        blk0 = start // BLOCK
        return jnp.concatenate([vt_ref[0, blk0 + c, hd0 // SWA_GROUP] for c in range(n_win)], axis=1)

    def store(sb, hd0, o):
        for g in range(unit_heads):
            hd = hd0 + g
            o_ref[0, hd * SWA_HEAD_DIM:(hd + 1) * SWA_HEAD_DIM, sb * BLOCK:(sb + 1) * BLOCK] = (
                o[:, g * BLOCK:(g + 1) * BLOCK].astype(BF16))

    o_sum = jnp.zeros((SWA_HEAD_DIM, unit_heads * BLOCK), F32)
    probs = lambda unit: jnp.exp2(scores(*unit)).astype(BF16)
    ahead = [probs(unit) for unit in units[:lookahead]]
    for u, unit in enumerate(units):
        p = ahead.pop(0)
        if u + lookahead < len(units):
            ahead.append(probs(units[u + lookahead]))
        ov = jnp.dot(values_t(*unit), p, preferred_element_type=F32)
        o = ov[:SWA_HEAD_DIM] / (ov[SWA_HEAD_DIM:SWA_HEAD_DIM + 1] + 1.0)
        store(*unit, o)
        o_sum = o_sum + o
    bad = jnp.max(jnp.where(jnp.isfinite(o_sum), 0.0, 1.0))

    @pl.when(bad > 0.0)
    def _recompute_with_row_max():
        for unit in units:
            s = scores(*unit)
            m = jnp.maximum(jnp.max(s, axis=0, keepdims=True), 0.0)
            p = jnp.exp2(s - m)
            l = jnp.sum(p, axis=0, keepdims=True) + jnp.exp2(-m)
            ov = jnp.dot(values_t(*unit), p.astype(BF16), preferred_element_type=F32)
            store(*unit, ov[:SWA_HEAD_DIM] / l)


def _swa_attn(qs_t, ks, vs_t, bias_tab, sub_blocks, unit_heads, lookahead):
    B, W, S = qs_t.shape
    kvw = ks.shape[2]
    tq = sub_blocks * BLOCK
    return pl.pallas_call(
        functools.partial(_swa_kernel, sub_blocks=sub_blocks, unit_heads=unit_heads, lookahead=lookahead),
        out_shape=jax.ShapeDtypeStruct((B, W, S), BF16),
        grid=(B, S // tq),
        in_specs=[pl.BlockSpec((1, W, tq), lambda b, i: (b, 0, i)),
                  pl.BlockSpec((1, S, kvw), lambda b, i: (b, 0, 0)),
                  pl.BlockSpec((1, S // BLOCK, SWA_KV_HEADS, SWA_VT_ROWS, BLOCK), lambda b, i: (b, 0, 0, 0, 0)),
                  _resident(bias_tab.shape)],
        out_specs=pl.BlockSpec((1, W, tq), lambda b, i: (b, 0, i)),
        compiler_params=_params(("arbitrary", "arbitrary")),
        name="swa_attn",
    )(qs_t, ks, vs_t, bias_tab)


def _out_mlp_kernel(x_ref, oat_ref, obt_ref, g_ref, g1_ref, sc_ref, sh_ref, g2_ref, nmlp_ref, nfin_ref,
                    woa_ref, wob_ref, wout_ref, w1_ref, w2_ref, o_ref, *, tf, n_groups):
    tm, D = x_ref.shape[1:]
    groups = [slice(r, r + tm // n_groups) for r in range(0, tm, tm // n_groups)]

    def attn_proj(rows):
        y_a = lax.dot_general(oat_ref[0, 0, :, rows], woa_ref[...], TN_DIMS, preferred_element_type=F32)
        y_b = lax.dot_general(obt_ref[0, :, rows], wob_ref[...], TN_DIMS, preferred_element_type=F32)
        return y_a, y_b

    def residual(rows, y):
        merged = g_ref[0, rows, :D].astype(F32) * y[0] + g_ref[0, rows, D:].astype(F32) * y[1]
        att = jnp.dot(merged.astype(BF16), wout_ref[...], preferred_element_type=F32)
        return x_ref[0, rows, :] + g1_ref[0] * att

    def mlp(rows, x1):
        h = (_rms(x1) * nmlp_ref[...] * (1.0 + sc_ref[0]) + sh_ref[0]).astype(BF16)
        ff = jnp.zeros_like(x1)
        for c in range(w1_ref.shape[1] // tf):
            a = jnp.dot(h, w1_ref[:, c * tf:(c + 1) * tf], preferred_element_type=F32)
            a = jnp.square(jnp.maximum(a, 0.0)).astype(BF16)
            ff = ff + jnp.dot(a, w2_ref[c * tf:(c + 1) * tf, :], preferred_element_type=F32)
        x2 = x1 + g2_ref[0] * ff
        o_ref[0, rows, :] = _rms(x2) * nfin_ref[...]

    ys = [attn_proj(rows) for rows in groups]
    x1s = [residual(rows, y) for rows, y in zip(groups, ys)]
    for rows, x1 in zip(groups, x1s):
        mlp(rows, x1)


def _out_mlp(x, o_mla, o_swa, gates, g1, sc2, sh2, g2, norm_mlp, norm_final,
             w_o_mla, w_o_swa, w_out, w_ff1, w_ff2, tm, tf, n_groups):
    B, S, D = x.shape
    tok = lambda w: pl.BlockSpec((1, tm, w), lambda b, i: (b, i, 0))
    per_batch = pl.BlockSpec((1, 1, D), lambda b, i: (b, 0, 0))
    return pl.pallas_call(
        functools.partial(_out_mlp_kernel, tf=tf, n_groups=n_groups),
        out_shape=jax.ShapeDtypeStruct((B, S, D), F32),
        grid=(B, S // tm),
        in_specs=[tok(D), pl.BlockSpec((1, 1, D, tm), lambda b, i: (b, i, 0, 0)),
                  pl.BlockSpec((1, D, tm), lambda b, i: (b, 0, i)), tok(2 * D),
                  per_batch, per_batch, per_batch, per_batch,
                  _resident((1, D)), _resident((1, D)),
                  _resident(w_o_mla.shape), _resident(w_o_swa.shape), _resident(w_out.shape),
                  _resident(w_ff1.shape), _resident(w_ff2.shape)],
        out_specs=tok(D),
        compiler_params=_params(("arbitrary", "arbitrary")),
        name="out_mlp",
    )(x, o_mla, o_swa, gates, g1, sc2, sh2, g2, norm_mlp.reshape(1, D), norm_final.reshape(1, D),
      w_o_mla, w_o_swa, w_out, w_ff1, w_ff2)


def _pack_w_in(w_in):
    kr1 = Q_LORA + KV_LORA + MLA_ROPE
    qs1 = kr1 + SWA_HEADS * SWA_HEAD_DIM
    ks1 = qs1 + SWA_KV_HEADS * SWA_HEAD_DIM
    vs1 = ks1 + SWA_KV_HEADS * SWA_HEAD_DIM
    pad = jnp.zeros((w_in.shape[0], LANES - MLA_ROPE), w_in.dtype)
    w_tok = jnp.concatenate([w_in[:, :kr1], pad, w_in[:, qs1:ks1], w_in[:, vs1:]], axis=1)
    w_t = jnp.concatenate([w_in[:, kr1:qs1], w_in[:, ks1:vs1]], axis=1).T
    return w_tok.astype(BF16), w_t.astype(BF16)


def _split_heads(w, n_first):
    w3 = w.reshape(w.shape[0], MLA_HEADS, -1)
    first = w3[:, :, :n_first].reshape(w.shape[0], -1)
    second = w3[:, :, n_first:].reshape(w.shape[0], -1)
    return first.astype(BF16), second.astype(BF16)


def kernel(x, c, positions, w_ada, b_ada, norm_mix, w_in, q_norm, w_uq, kv_norm, w_ukv, rel_bias, sink,
           w_o_mla, w_o_swa, w_out, norm_mlp, w_ff1, w_ff2, norm_final):
    B, S, D = x.shape
    assert w_ada.shape[0] == 1, "single-layer block"
    assert D == D_MODEL and S % (4 * BLOCK) == 0 and S >= BLOCK + 2 * WINDOW

    mod = _ada_mod(c, w_ada[0], b_ada[0])
    sh1, sc1, g1, sh2, sc2, g2 = [m.reshape(B, 1, D) for m in jnp.split(mod, N_MOD, axis=-1)]

    bias_tab = _swa_bias(rel_bias, sink[0])

    tm = 512
    w_k, w_v = _split_heads(w_ukv[0], MLA_NOPE)
    w_in_tok, w_in_t = _pack_w_in(w_in[0])
    qt_mla, k_mla, vt_mla, qs_t, ks, vs_t, gates = _in_proj(
        x, norm_mix[0], sc1, sh1, positions, w_in_tok, w_in_t, q_norm[0], w_uq[0].T.astype(BF16),
        kv_norm[0], w_k, w_v.T, tm=tm, n_groups=1)

    ot_mla = _mla_attn(qt_mla, k_mla, vt_mla, tk=1024, tr=256, lookahead=1)
    ot_swa = _swa_attn(qs_t, ks, vs_t, bias_tab, sub_blocks=8, unit_heads=2, lookahead=2)

    return _out_mlp(x, ot_mla, ot_swa, gates, g1, sc2, sh2, g2, norm_mlp[0], norm_final,
                    w_o_mla[0].astype(BF16), w_o_swa[0].astype(BF16), w_out[0].astype(BF16),
                    w_ff1[0].astype(BF16), w_ff2[0].astype(BF16), tm=tm, tf=1024, n_groups=2)
```

```python
import functools
import math

import jax
import jax.numpy as jnp
from jax import lax
from jax.experimental import pallas as pl
from jax.experimental.pallas import tpu as pltpu

F32 = jnp.float32
BF16 = jnp.bfloat16

D_MODEL = 1024
MLA_HEADS = 8
MLA_NOPE = 128
MLA_ROPE = 64
MLA_V = 128
Q_LORA = 384
KV_LORA = 256
ROPE_THETA = 10000.0
SWA_HEADS = 8
SWA_KV_HEADS = 2
SWA_GROUP = SWA_HEADS // SWA_KV_HEADS
SWA_HEAD_DIM = 128
WINDOW = 128
BLOCK = 128
N_BUCKETS = 32
N_MOD = 6
EPS = 1e-6
NEG_INF = -1e30

ROPE_HALF = MLA_ROPE // 2
LANES = 128
BF16_ROWS = 16
MLA_QK_PAD = 256
MLA_SHIFT_COL = MLA_NOPE + MLA_ROPE
MLA_VT_ROWS = MLA_V + 16
SWA_VT_ROWS = SWA_HEAD_DIM + 16
VMEM_LIMIT = 56 * 1024 * 1024

LOG2_E = math.log2(math.e)
MLA_Q_SCALE = (MLA_NOPE + MLA_ROPE) ** -0.5 * LOG2_E
SWA_Q_SCALE = SWA_HEAD_DIM ** -0.5 * LOG2_E

C_CQ = 0
C_CKV = C_CQ + Q_LORA
C_KR = C_CKV + KV_LORA
C_KS = C_KR + LANES
C_G = C_KS + SWA_KV_HEADS * SWA_HEAD_DIM
GATE_CHUNK = 2 * D_MODEL // 4

T5_LARGE_THRESHOLDS = (12, 16, 23, 32, 46, 64, 91)

NT_DIMS = (((1,), (1,)), ((), ()))
TN_DIMS = (((0,), (0,)), ((), ()))


def _resident(shape):
    nd = len(shape)
    return pl.BlockSpec(shape, lambda *_: (0,) * nd, pipeline_mode=pl.Buffered(1))


def _params(semantics):
    return pltpu.CompilerParams(dimension_semantics=semantics, vmem_limit_bytes=VMEM_LIMIT)


def _rms(x):
    return x * lax.rsqrt(jnp.mean(x * x, axis=-1, keepdims=True) + EPS)


def _ada_kernel(c_ref, w_ref, b_ref, o_ref):
    c = c_ref[...]
    c_act = c * jax.nn.sigmoid(c)
    o_ref[...] = jnp.dot(c_act, w_ref[...], preferred_element_type=F32,
                         precision=lax.Precision.HIGHEST) + b_ref[...]


def _ada_mod(c, w_ada, b_ada):
    B, D = c.shape
    N = w_ada.shape[1]
    tn = 1024
    return pl.pallas_call(
        _ada_kernel,
        out_shape=jax.ShapeDtypeStruct((B, N), F32),
        grid=(N // tn,),
        in_specs=[pl.BlockSpec((B, D), lambda j: (0, 0)),
                  pl.BlockSpec((D, tn), lambda j: (0, j)),
                  pl.BlockSpec((1, tn), lambda j: (0, j))],
        out_specs=pl.BlockSpec((B, tn), lambda j: (0, j)),
        compiler_params=_params(("arbitrary",)),
        name="ada_mod",
    )(c, w_ada, b_ada.reshape(1, N))


def _swa_bias_kernel(tab_ref, sink_ref, o_ref):
    h = pl.program_id(0)
    half = N_BUCKETS // 2
    max_exact = half // 2
    kj = lax.broadcasted_iota(jnp.int32, (BLOCK, LANES), 0)
    qi = lax.broadcasted_iota(jnp.int32, (BLOCK, LANES), 1)
    for cb in range(5):
        rel = kj - qi + (cb * LANES - 2 * WINDOW)
        n = jnp.abs(rel)
        large = jnp.full_like(n, max_exact)
        for t in T5_LARGE_THRESHOLDS:
            large = large + jnp.where(n >= t, 1, 0)
        bucket = jnp.where(rel > 0, half, 0) + jnp.where(n < max_exact, n, large)
        bias = jnp.zeros((BLOCK, LANES), F32)
        for b in range(N_BUCKETS):
            bias = jnp.where(bucket == b, tab_ref[b, h], bias)
        o_ref[0, cb] = jnp.where(n <= WINDOW, (bias - sink_ref[0, h]) * LOG2_E, NEG_INF)


def _swa_bias(rel_bias, sink):
    return pl.pallas_call(
        _swa_bias_kernel,
        out_shape=jax.ShapeDtypeStruct((SWA_HEADS, 5, BLOCK, LANES), F32),
        grid=(SWA_HEADS,),
        in_specs=[pl.BlockSpec(memory_space=pltpu.SMEM), pl.BlockSpec(memory_space=pltpu.SMEM)],
        out_specs=pl.BlockSpec((1, 5, BLOCK, LANES), lambda h: (h, 0, 0, 0)),
        compiler_params=_params(("arbitrary",)),
        name="swa_bias",
    )(rel_bias, sink.reshape(1, SWA_HEADS))


def _rope_rows(x, tab):
    x1, x2 = x[:ROPE_HALF], x[ROPE_HALF:]
    cos, sin = tab[:ROPE_HALF], tab[ROPE_HALF:]
    return jnp.concatenate([x1 * cos - x2 * sin, x2 * cos + x1 * sin], axis=0)


def _in_proj_kernel(x_ref, nm_ref, sc_ref, sh_ref, pos_ref, inv_ref, win_ref, wqvt_ref, qn_ref, wuqt_ref, kvn_ref,
                    wk_ref, wvt_ref, qt_ref, km_ref, vt_ref, qst_ref, ks_ref, vst_ref, g_ref):
    tm = x_ref.shape[1]
    n_q = SWA_HEADS * SWA_HEAD_DIM
    rope_pad = jnp.zeros((MLA_QK_PAD - MLA_NOPE - MLA_ROPE, tm), F32)
    one_row = jnp.where(lax.broadcasted_iota(jnp.int32, rope_pad.shape, 0) == 0, 1.0, 0.0)
    mla_ones_rows = jnp.where(lax.broadcasted_iota(jnp.int32, (MLA_VT_ROWS - MLA_V, tm), 0) == 0,
                              1.0, 0.0).astype(BF16)
    swa_ones_rows = jnp.where(lax.broadcasted_iota(jnp.int32, (SWA_VT_ROWS - SWA_HEAD_DIM, BLOCK), 0) == 0,
                              1.0, 0.0).astype(BF16)

    h = (_rms(x_ref[0]) * nm_ref[...] * (1.0 + sc_ref[0]) + sh_ref[0]).astype(BF16)
    ang = inv_ref[...] * pos_ref[0].astype(F32)
    tab = jnp.concatenate([jnp.cos(ang), jnp.sin(ang)], axis=0)

    def gate(j):
        g = jnp.dot(h, win_ref[:, C_G + j * GATE_CHUNK:C_G + (j + 1) * GATE_CHUNK], preferred_element_type=F32)
        g_ref[0, :, j * GATE_CHUNK:(j + 1) * GATE_CHUNK] = (0.5 * jnp.tanh(0.5 * g) + 0.5).astype(BF16)

    lat = jnp.dot(h, win_ref[:, C_CQ:C_KS], preferred_element_type=F32)
    gate(0)
    gate(1)
    cq = lat[:, C_CQ:C_CKV]
    ckv = lat[:, C_CKV:C_KR]
    kr_t = lat[:, C_KR:C_KS].T
    k_rope = jnp.concatenate([_rope_rows(kr_t[:MLA_ROPE], tab), one_row], axis=0).T.astype(BF16)

    cqn = (_rms(cq) * qn_ref[...]).astype(BF16)
    q_t = lax.dot_general(wuqt_ref[...], cqn, NT_DIMS, preferred_element_type=F32)
    for hh in range(MLA_HEADS):
        q0 = hh * MLA_QK_PAD
        qh = q_t[hh * (MLA_NOPE + MLA_ROPE):(hh + 1) * (MLA_NOPE + MLA_ROPE)]
        qt_ref[0, 0, q0:q0 + MLA_NOPE, :] = (qh[:MLA_NOPE] * MLA_Q_SCALE).astype(BF16)
        q_rope = jnp.concatenate([_rope_rows(qh[MLA_NOPE:], tab) * MLA_Q_SCALE, rope_pad], axis=0)
        qt_ref[0, 0, q0 + MLA_NOPE:q0 + MLA_QK_PAD, :] = q_rope.astype(BF16)

    ckvn = (_rms(ckv) * kvn_ref[...]).astype(BF16)
    kn = jnp.dot(ckvn, wk_ref[...], preferred_element_type=F32)
    for hh in range(MLA_HEADS):
        c0 = hh * MLA_QK_PAD
        km_ref[0, :, c0:c0 + MLA_NOPE] = kn[:, hh * MLA_NOPE:(hh + 1) * MLA_NOPE].astype(BF16)
        km_ref[0, :, c0 + MLA_NOPE:c0 + MLA_QK_PAD] = k_rope
    vt = lax.dot_general(wvt_ref[...], ckvn, NT_DIMS, preferred_element_type=F32)
    for hh in range(MLA_HEADS):
        v0 = hh * MLA_VT_ROWS
        vt_ref[0, 0, v0:v0 + MLA_V, :] = vt[hh * MLA_V:(hh + 1) * MLA_V].astype(BF16)
        vt_ref[0, 0, v0 + MLA_V:v0 + MLA_VT_ROWS, :] = mla_ones_rows

    qv_t = lax.dot_general(wqvt_ref[...], h, NT_DIMS, preferred_element_type=F32)
    qst_ref[0] = (qv_t[:n_q] * SWA_Q_SCALE).astype(BF16)
    for n in range(SWA_KV_HEADS):
        vs_t = qv_t[n_q + n * SWA_HEAD_DIM:n_q + (n + 1) * SWA_HEAD_DIM].astype(BF16)
        for j in range(tm // BLOCK):
            vst_ref[0, j, n, :SWA_HEAD_DIM, :] = vs_t[:, j * BLOCK:(j + 1) * BLOCK]
            vst_ref[0, j, n, SWA_HEAD_DIM:, :] = swa_ones_rows
    gate(2)
    gate(3)
    ks_ref[0] = jnp.dot(h, win_ref[:, C_KS:C_G], preferred_element_type=F32).astype(BF16)


def _in_proj(x, norm_mix, sc1, sh1, positions, w_in_p, w_qv_t, q_norm, w_uq_t, kv_norm, w_k, w_v_t, tm):
    B, S, D = x.shape
    inv_freq = ROPE_THETA ** (-jnp.arange(0, MLA_ROPE, 2, dtype=F32) / MLA_ROPE)
    kvw = SWA_KV_HEADS * SWA_HEAD_DIM
    tok = lambda w: pl.BlockSpec((1, tm, w), lambda b, i: (b, i, 0))
    tok_t = lambda r: pl.BlockSpec((1, r, tm), lambda b, i: (b, 0, i))
    tile_t = lambda r: pl.BlockSpec((1, 1, r, tm), lambda b, i: (b, i, 0, 0))
    per_batch = pl.BlockSpec((1, 1, D), lambda b, i: (b, 0, 0))
    sds = lambda *shape: jax.ShapeDtypeStruct(shape, BF16)
    return pl.pallas_call(
        _in_proj_kernel,
        out_shape=[sds(B, S // tm, MLA_HEADS * MLA_QK_PAD, tm), sds(B, S, MLA_HEADS * MLA_QK_PAD),
                   sds(B, S // tm, MLA_HEADS * MLA_VT_ROWS, tm), sds(B, SWA_HEADS * SWA_HEAD_DIM, S),
                   sds(B, S, kvw), sds(B, S // BLOCK, SWA_KV_HEADS, SWA_VT_ROWS, BLOCK), sds(B, S, 2 * D)],
        grid=(B, S // tm),
        in_specs=[tok(D), _resident((1, D)), per_batch, per_batch, tok_t(1), _resident((ROPE_HALF, 1)),
                  _resident(w_in_p.shape), _resident(w_qv_t.shape), _resident((1, Q_LORA)),
                  _resident(w_uq_t.shape), _resident((1, KV_LORA)), _resident(w_k.shape),
                  _resident(w_v_t.shape)],
        out_specs=[tile_t(MLA_HEADS * MLA_QK_PAD), tok(MLA_HEADS * MLA_QK_PAD), tile_t(MLA_HEADS * MLA_VT_ROWS),
                   tok_t(SWA_HEADS * SWA_HEAD_DIM), tok(kvw),
                   pl.BlockSpec((1, tm // BLOCK, SWA_KV_HEADS, SWA_VT_ROWS, BLOCK), lambda b, i: (b, i, 0, 0, 0)),
                   tok(2 * D)],
        compiler_params=_params(("arbitrary", "arbitrary")),
        name="in_proj",
    )(x, norm_mix.reshape(1, D), sc1, sh1, positions.reshape(B, 1, S), inv_freq.reshape(ROPE_HALF, 1),
      w_in_p, w_qv_t, q_norm.reshape(1, Q_LORA), w_uq_t, kv_norm.reshape(1, KV_LORA), w_k, w_v_t)


def _mla_kernel(qt_ref, k_ref, vt_ref, o_ref, *, tk, tr, lookahead):
    n_tiles, _, tq = qt_ref.shape[1:]
    S = k_ref.shape[1]
    n_chunks = S // tk
    tiles_per_chunk = tk // tq
    g0 = MLA_SHIFT_COL // BF16_ROWS * BF16_ROWS
    row = lax.broadcasted_iota(jnp.int32, (BF16_ROWS, tq), 0)

    def shifted_q(i):
        qt = qt_ref[0, i]
        s = jnp.dot(k_ref[0, :tr, :], qt, preferred_element_type=F32)
        ref = jnp.max(s, axis=0, keepdims=True)
        grp = jnp.where(row == MLA_SHIFT_COL - g0, -ref, qt[g0:g0 + BF16_ROWS].astype(F32)).astype(BF16)
        return jnp.concatenate([qt[:g0], grp, qt[g0 + BF16_ROWS:]], axis=0)

    def probs(item, qts):
        i, c = item
        s = jnp.dot(k_ref[0, c * tk:(c + 1) * tk, :], qts[i], preferred_element_type=F32)
        return jnp.exp2(s).astype(BF16)

    def values_t(c):
        return jnp.concatenate([vt_ref[0, c * tiles_per_chunk + j] for j in range(tiles_per_chunk)], axis=1)

    items = [(i, c) for i in range(n_tiles) for c in range(n_chunks)]
    qts = {0: shifted_q(0)}
    ahead = [probs(item, qts) for item in items[:lookahead]]
    o_sum = jnp.zeros((MLA_V, tq), F32)
    for idx, (i, c) in enumerate(items):
        if c == 0 and i + 1 < n_tiles:
            qts[i + 1] = shifted_q(i + 1)
        p = ahead.pop(0)
        if idx + lookahead < len(items):
            ahead.append(probs(items[idx + lookahead], qts))
        pv = jnp.dot(values_t(c), p, preferred_element_type=F32)
        acc = pv if c == 0 else acc + pv
        if c == n_chunks - 1:
            o = acc[:MLA_V] / acc[MLA_V:MLA_V + 1]
            o_ref[0, i] = o.astype(BF16)
            o_sum = o_sum + o
            qts.pop(i)
    bad = jnp.max(jnp.where(jnp.isfinite(o_sum), 0.0, 1.0))

    @pl.when(bad > 0.0)
    def _recompute_with_running_max():
        def tile_body(i, carry):
            qt = qt_ref[0, i]

            def key_tile_body(j, state):
                m, l, acc = state
                k = k_ref[0, pl.ds(pl.multiple_of(j * tq, tq), tq), :]
                s = jnp.dot(k, qt, preferred_element_type=F32)
                m_new = jnp.maximum(m, jnp.max(s, axis=0, keepdims=True))
                p = jnp.exp2(s - m_new)
                alpha = jnp.exp2(m - m_new)
                l = alpha * l + jnp.sum(p, axis=0, keepdims=True)
                acc = alpha * acc + jnp.dot(vt_ref[0, j, :MLA_V, :], p.astype(BF16), preferred_element_type=F32)
                return m_new, l, acc

            init = (jnp.full((1, tq), -jnp.inf, F32), jnp.zeros((1, tq), F32), jnp.zeros((MLA_V, tq), F32))
            _, l, acc = lax.fori_loop(0, n_tiles, key_tile_body, init)
            o_ref[0, i] = (acc / l).astype(BF16)
            return carry

        lax.fori_loop(0, n_tiles, tile_body, 0)


def _mla_attn(qt_mla, k_mla, vt_mla, tk, tr, lookahead):
    B, n_tiles, _, tq = qt_mla.shape
    S = k_mla.shape[1]
    return pl.pallas_call(
        functools.partial(_mla_kernel, tk=tk, tr=tr, lookahead=lookahead),
        out_shape=jax.ShapeDtypeStruct((B, n_tiles, MLA_HEADS * MLA_V, tq), BF16),
        grid=(B, MLA_HEADS),
        in_specs=[pl.BlockSpec((1, n_tiles, MLA_QK_PAD, tq), lambda b, h: (b, 0, h, 0)),
                  pl.BlockSpec((1, S, MLA_QK_PAD), lambda b, h: (b, 0, h)),
                  pl.BlockSpec((1, n_tiles, MLA_VT_ROWS, tq), lambda b, h: (b, 0, h, 0))],
        out_specs=pl.BlockSpec((1, n_tiles, MLA_V, tq), lambda b, h: (b, 0, h, 0)),
        compiler_params=_params(("arbitrary", "arbitrary")),
        name="mla_attn",
    )(qt_mla, k_mla, vt_mla)


def _swa_kernel(qt_ref, k_ref, vt_ref, t_ref, o_ref, *, sub_blocks, unit_heads, lookahead):
    S = k_ref.shape[1]
    span = BLOCK + 2 * WINDOW
    n_win = span // BLOCK
    step = pl.program_id(1)
    units = [(sb, hd0) for sb in range(sub_blocks) for hd0 in range(0, SWA_HEADS, unit_heads)]

    def window(sb):
        q0 = (step * sub_blocks + sb) * BLOCK
        start = pl.multiple_of(jnp.clip(q0 - WINDOW, 0, S - span), BLOCK)
        cb0 = jnp.where(q0 == 0, 2, jnp.where(q0 == S - BLOCK, 0, 1))
        return start, cb0

    def scores(sb, hd0):
        start, cb0 = window(sb)
        n = hd0 // SWA_GROUP
        heads = range(hd0, hd0 + unit_heads)
        qt = jnp.concatenate(
            [qt_ref[0, hd * SWA_HEAD_DIM:(hd + 1) * SWA_HEAD_DIM, sb * BLOCK:(sb + 1) * BLOCK] for hd in heads],
            axis=1)
        kw = k_ref[0, pl.ds(start, span), n * SWA_HEAD_DIM:(n + 1) * SWA_HEAD_DIM]
        bias = jnp.concatenate(
            [jnp.concatenate([t_ref[hd, cb0 + c] for c in range(n_win)], axis=0) for hd in heads], axis=1)
        return jnp.dot(kw, qt, preferred_element_type=F32) + bias

    def values_t(sb, hd0):
        start, _ = window(sb)
        blk0 = start // BLOCK
        return jnp.concatenate([vt_ref[0, blk0 + c, hd0 // SWA_GROUP] for c in range(n_win)], axis=1)

    def store(sb, hd0, o):
        for g in range(unit_heads):
            hd = hd0 + g
            o_ref[0, hd * SWA_HEAD_DIM:(hd + 1) * SWA_HEAD_DIM, sb * BLOCK:(sb + 1) * BLOCK] = (
                o[:, g * BLOCK:(g + 1) * BLOCK].astype(BF16))

    o_sum = jnp.zeros((SWA_HEAD_DIM, unit_heads * BLOCK), F32)
    probs = lambda unit: jnp.exp2(scores(*unit)).astype(BF16)
    ahead = [probs(unit) for unit in units[:lookahead]]
    for u, unit in enumerate(units):
        p = ahead.pop(0)
        if u + lookahead < len(units):
            ahead.append(probs(units[u + lookahead]))
        ov = jnp.dot(values_t(*unit), p, preferred_element_type=F32)
        o = ov[:SWA_HEAD_DIM] / (ov[SWA_HEAD_DIM:SWA_HEAD_DIM + 1] + 1.0)
        store(*unit, o)
        o_sum = o_sum + o
    bad = jnp.max(jnp.where(jnp.isfinite(o_sum), 0.0, 1.0))

    @pl.when(bad > 0.0)
    def _recompute_with_row_max():
        for unit in units:
            s = scores(*unit)
            m = jnp.maximum(jnp.max(s, axis=0, keepdims=True), 0.0)
            p = jnp.exp2(s - m)
            l = jnp.sum(p, axis=0, keepdims=True) + jnp.exp2(-m)
            ov = jnp.dot(values_t(*unit), p.astype(BF16), preferred_element_type=F32)
            store(*unit, ov[:SWA_HEAD_DIM] / l)


def _swa_attn(qs_t, ks, vs_t, bias_tab, sub_blocks, unit_heads, lookahead):
    B, W, S = qs_t.shape
    kvw = ks.shape[2]
    tq = sub_blocks * BLOCK
    return pl.pallas_call(
        functools.partial(_swa_kernel, sub_blocks=sub_blocks, unit_heads=unit_heads, lookahead=lookahead),
        out_shape=jax.ShapeDtypeStruct((B, W, S), BF16),
        grid=(B, S // tq),
        in_specs=[pl.BlockSpec((1, W, tq), lambda b, i: (b, 0, i)),
                  pl.BlockSpec((1, S, kvw), lambda b, i: (b, 0, 0)),
                  pl.BlockSpec((1, S // BLOCK, SWA_KV_HEADS, SWA_VT_ROWS, BLOCK), lambda b, i: (b, 0, 0, 0, 0)),
                  _resident(bias_tab.shape)],
        out_specs=pl.BlockSpec((1, W, tq), lambda b, i: (b, 0, i)),
        compiler_params=_params(("arbitrary", "arbitrary")),
        name="swa_attn",
    )(qs_t, ks, vs_t, bias_tab)


def _out_mlp_kernel(x_ref, oat_ref, obt_ref, g_ref, g1_ref, sc_ref, sh_ref, g2_ref, nmlp_ref, nfin_ref,
                    woa_ref, wob_ref, wout_ref, w1_ref, w2_ref, o_ref, *, tf, n_groups):
    tm, D = x_ref.shape[1:]
    groups = [slice(r, r + tm // n_groups) for r in range(0, tm, tm // n_groups)]

    def attn_proj(rows):
        y_a = lax.dot_general(oat_ref[0, 0, :, rows], woa_ref[...], TN_DIMS, preferred_element_type=F32)
        y_b = lax.dot_general(obt_ref[0, :, rows], wob_ref[...], TN_DIMS, preferred_element_type=F32)
        return y_a, y_b

    def residual(rows, y):
        merged = g_ref[0, rows, :D].astype(F32) * y[0] + g_ref[0, rows, D:].astype(F32) * y[1]
        att = jnp.dot(merged.astype(BF16), wout_ref[...], preferred_element_type=F32)
        return x_ref[0, rows, :] + g1_ref[0] * att

    def mlp(rows, x1):
        h = (_rms(x1) * nmlp_ref[...] * (1.0 + sc_ref[0]) + sh_ref[0]).astype(BF16)
        ff = jnp.zeros_like(x1)
        for c in range(w1_ref.shape[1] // tf):
            a = jnp.dot(h, w1_ref[:, c * tf:(c + 1) * tf], preferred_element_type=F32)
            a = jnp.square(jnp.maximum(a, 0.0)).astype(BF16)
            ff = ff + jnp.dot(a, w2_ref[c * tf:(c + 1) * tf, :], preferred_element_type=F32)
        x2 = x1 + g2_ref[0] * ff
        o_ref[0, rows, :] = _rms(x2) * nfin_ref[...]

    ys = [attn_proj(rows) for rows in groups]
    x1s = [residual(rows, y) for rows, y in zip(groups, ys)]
    for rows, x1 in zip(groups, x1s):
        mlp(rows, x1)


def _out_mlp(x, o_mla, o_swa, gates, g1, sc2, sh2, g2, norm_mlp, norm_final,
             w_o_mla, w_o_swa, w_out, w_ff1, w_ff2, tm, tf, n_groups):
    B, S, D = x.shape
    tok = lambda w: pl.BlockSpec((1, tm, w), lambda b, i: (b, i, 0))
    per_batch = pl.BlockSpec((1, 1, D), lambda b, i: (b, 0, 0))
    return pl.pallas_call(
        functools.partial(_out_mlp_kernel, tf=tf, n_groups=n_groups),
        out_shape=jax.ShapeDtypeStruct((B, S, D), F32),
        grid=(B, S // tm),
        in_specs=[tok(D), pl.BlockSpec((1, 1, D, tm), lambda b, i: (b, i, 0, 0)),
                  pl.BlockSpec((1, D, tm), lambda b, i: (b, 0, i)), tok(2 * D),
                  per_batch, per_batch, per_batch, per_batch,
                  _resident((1, D)), _resident((1, D)),
                  _resident(w_o_mla.shape), _resident(w_o_swa.shape), _resident(w_out.shape),
                  _resident(w_ff1.shape), _resident(w_ff2.shape)],
        out_specs=tok(D),
        compiler_params=_params(("arbitrary", "arbitrary")),
        name="out_mlp",
    )(x, o_mla, o_swa, gates, g1, sc2, sh2, g2, norm_mlp.reshape(1, D), norm_final.reshape(1, D),
      w_o_mla, w_o_swa, w_out, w_ff1, w_ff2)


def _pack_w_in(w_in):
    kr1 = Q_LORA + KV_LORA + MLA_ROPE
    qs1 = kr1 + SWA_HEADS * SWA_HEAD_DIM
    ks1 = qs1 + SWA_KV_HEADS * SWA_HEAD_DIM
    vs1 = ks1 + SWA_KV_HEADS * SWA_HEAD_DIM
    pad = jnp.zeros((w_in.shape[0], LANES - MLA_ROPE), w_in.dtype)
    w_tok = jnp.concatenate([w_in[:, :kr1], pad, w_in[:, qs1:ks1], w_in[:, vs1:]], axis=1)
    w_t = jnp.concatenate([w_in[:, kr1:qs1], w_in[:, ks1:vs1]], axis=1).T
    return w_tok.astype(BF16), w_t.astype(BF16)


def _split_heads(w, n_first):
    w3 = w.reshape(w.shape[0], MLA_HEADS, -1)
    first = w3[:, :, :n_first].reshape(w.shape[0], -1)
    second = w3[:, :, n_first:].reshape(w.shape[0], -1)
    return first.astype(BF16), second.astype(BF16)


def kernel(x, c, positions, w_ada, b_ada, norm_mix, w_in, q_norm, w_uq, kv_norm, w_ukv, rel_bias, sink,
           w_o_mla, w_o_swa, w_out, norm_mlp, w_ff1, w_ff2, norm_final):
    B, S, D = x.shape
    assert w_ada.shape[0] == 1, "single-layer block"
    assert D == D_MODEL and S % (4 * BLOCK) == 0 and S >= BLOCK + 2 * WINDOW

    mod = _ada_mod(c, w_ada[0], b_ada[0])
    sh1, sc1, g1, sh2, sc2, g2 = [m.reshape(B, 1, D) for m in jnp.split(mod, N_MOD, axis=-1)]

    bias_tab = _swa_bias(rel_bias, sink[0])

    tm = 512
    w_k, w_v = _split_heads(w_ukv[0], MLA_NOPE)
    w_in_tok, w_in_t = _pack_w_in(w_in[0])
    qt_mla, k_mla, vt_mla, qs_t, ks, vs_t, gates = _in_proj(
        x, norm_mix[0], sc1, sh1, positions, w_in_tok, w_in_t, q_norm[0], w_uq[0].T.astype(BF16),
        kv_norm[0], w_k, w_v.T, tm=tm)

    ot_mla = _mla_attn(qt_mla, k_mla, vt_mla, tk=1024, tr=256, lookahead=1)
    ot_swa = _swa_attn(qs_t, ks, vs_t, bias_tab, sub_blocks=16, unit_heads=2, lookahead=2)

    return _out_mlp(x, ot_mla, ot_swa, gates, g1, sc2, sh2, g2, norm_mlp[0], norm_final,
                    w_o_mla[0].astype(BF16), w_o_swa[0].astype(BF16), w_out[0].astype(BF16),
                    w_ff1[0].astype(BF16), w_ff2[0].astype(BF16), tm=tm, tf=1024, n_groups=2)
```

```python
import functools
import math

import jax
import jax.numpy as jnp
from jax import lax
from jax.experimental import pallas as pl
from jax.experimental.pallas import tpu as pltpu

F32 = jnp.float32
BF16 = jnp.bfloat16

D_MODEL = 1024
MLA_HEADS = 8
MLA_NOPE = 128
MLA_ROPE = 64
MLA_V = 128
Q_LORA = 384
KV_LORA = 256
ROPE_THETA = 10000.0
SWA_HEADS = 8
SWA_KV_HEADS = 2
SWA_GROUP = SWA_HEADS // SWA_KV_HEADS
SWA_HEAD_DIM = 128
WINDOW = 128
BLOCK = 128
N_BUCKETS = 32
N_MOD = 6
EPS = 1e-6
NEG_INF = -1e30

ROPE_HALF = MLA_ROPE // 2
LANES = 128
BF16_ROWS = 16
MLA_QK_PAD = 256
MLA_SHIFT_COL = MLA_NOPE + MLA_ROPE
MLA_VT_ROWS = MLA_V + 16
SWA_VT_ROWS = SWA_HEAD_DIM + 16
VMEM_LIMIT = 56 * 1024 * 1024

LOG2_E = math.log2(math.e)
MLA_Q_SCALE = (MLA_NOPE + MLA_ROPE) ** -0.5 * LOG2_E
SWA_Q_SCALE = SWA_HEAD_DIM ** -0.5 * LOG2_E

C_CQ = 0
C_CKV = C_CQ + Q_LORA
C_KR = C_CKV + KV_LORA
C_KS = C_KR + LANES
C_G = C_KS + SWA_KV_HEADS * SWA_HEAD_DIM
GATE_CHUNK = 2 * D_MODEL // 4

T5_LARGE_THRESHOLDS = (12, 16, 23, 32, 46, 64, 91)

NT_DIMS = (((1,), (1,)), ((), ()))
TN_DIMS = (((0,), (0,)), ((), ()))


def _resident(shape):
    nd = len(shape)
    return pl.BlockSpec(shape, lambda *_: (0,) * nd, pipeline_mode=pl.Buffered(1))


def _params(semantics):
    return pltpu.CompilerParams(dimension_semantics=semantics, vmem_limit_bytes=VMEM_LIMIT)


def _rms(x):
    return x * lax.rsqrt(jnp.mean(x * x, axis=-1, keepdims=True) + EPS)


def _ada_kernel(c_ref, w_ref, b_ref, o_ref):
    c = c_ref[...]
    c_act = c * jax.nn.sigmoid(c)
    o_ref[...] = jnp.dot(c_act, w_ref[...], preferred_element_type=F32,
                         precision=lax.Precision.HIGHEST) + b_ref[...]


def _ada_mod(c, w_ada, b_ada):
    B, D = c.shape
    N = w_ada.shape[1]
    tn = 1024
    return pl.pallas_call(
        _ada_kernel,
        out_shape=jax.ShapeDtypeStruct((B, N), F32),
        grid=(N // tn,),
        in_specs=[pl.BlockSpec((B, D), lambda j: (0, 0)),
                  pl.BlockSpec((D, tn), lambda j: (0, j)),
                  pl.BlockSpec((1, tn), lambda j: (0, j))],
        out_specs=pl.BlockSpec((B, tn), lambda j: (0, j)),
        compiler_params=_params(("arbitrary",)),
        name="ada_mod",
    )(c, w_ada, b_ada.reshape(1, N))


def _swa_bias_kernel(tab_ref, sink_ref, o_ref):
    h = pl.program_id(0)
    half = N_BUCKETS // 2
    max_exact = half // 2
    kj = lax.broadcasted_iota(jnp.int32, (BLOCK, LANES), 0)
    qi = lax.broadcasted_iota(jnp.int32, (BLOCK, LANES), 1)
    for cb in range(5):
        rel = kj - qi + (cb * LANES - 2 * WINDOW)
        n = jnp.abs(rel)
        large = jnp.full_like(n, max_exact)
        for t in T5_LARGE_THRESHOLDS:
            large = large + jnp.where(n >= t, 1, 0)
        bucket = jnp.where(rel > 0, half, 0) + jnp.where(n < max_exact, n, large)
        bias = jnp.zeros((BLOCK, LANES), F32)
        for b in range(N_BUCKETS):
            bias = jnp.where(bucket == b, tab_ref[b, h], bias)
        o_ref[0, cb] = jnp.where(n <= WINDOW, (bias - sink_ref[0, h]) * LOG2_E, NEG_INF)


def _swa_bias(rel_bias, sink):
    return pl.pallas_call(
        _swa_bias_kernel,
        out_shape=jax.ShapeDtypeStruct((SWA_HEADS, 5, BLOCK, LANES), F32),
        grid=(SWA_HEADS,),
        in_specs=[pl.BlockSpec(memory_space=pltpu.SMEM), pl.BlockSpec(memory_space=pltpu.SMEM)],
        out_specs=pl.BlockSpec((1, 5, BLOCK, LANES), lambda h: (h, 0, 0, 0)),
        compiler_params=_params(("arbitrary",)),
        name="swa_bias",
    )(rel_bias, sink.reshape(1, SWA_HEADS))


def _rope_rows(x, tab):
    x1, x2 = x[:ROPE_HALF], x[ROPE_HALF:]
    cos, sin = tab[:ROPE_HALF], tab[ROPE_HALF:]
    return jnp.concatenate([x1 * cos - x2 * sin, x2 * cos + x1 * sin], axis=0)


def _in_proj_kernel(x_ref, nm_ref, sc_ref, sh_ref, pos_ref, inv_ref, win_ref, wqvt_ref, qn_ref, wuqt_ref, kvn_ref,
                    wk_ref, wvt_ref, qt_ref, km_ref, vt_ref, qst_ref, ks_ref, vst_ref, g_ref):
    tm = x_ref.shape[1]
    n_q = SWA_HEADS * SWA_HEAD_DIM
    rope_pad = jnp.zeros((MLA_QK_PAD - MLA_NOPE - MLA_ROPE, tm), F32)
    one_row = jnp.where(lax.broadcasted_iota(jnp.int32, rope_pad.shape, 0) == 0, 1.0, 0.0)
    mla_ones_rows = jnp.where(lax.broadcasted_iota(jnp.int32, (MLA_VT_ROWS - MLA_V, tm), 0) == 0,
                              1.0, 0.0).astype(BF16)
    swa_ones_rows = jnp.where(lax.broadcasted_iota(jnp.int32, (SWA_VT_ROWS - SWA_HEAD_DIM, BLOCK), 0) == 0,
                              1.0, 0.0).astype(BF16)

    h = (_rms(x_ref[0]) * nm_ref[...] * (1.0 + sc_ref[0]) + sh_ref[0]).astype(BF16)
    ang = inv_ref[...] * pos_ref[0].astype(F32)
    tab = jnp.concatenate([jnp.cos(ang), jnp.sin(ang)], axis=0)

    def gate(j):
        g = jnp.dot(h, win_ref[:, C_G + j * GATE_CHUNK:C_G + (j + 1) * GATE_CHUNK], preferred_element_type=F32)
        g_ref[0, :, j * GATE_CHUNK:(j + 1) * GATE_CHUNK] = (0.5 * jnp.tanh(0.5 * g) + 0.5).astype(BF16)

    lat = jnp.dot(h, win_ref[:, C_CQ:C_KS], preferred_element_type=F32)
    gate(0)
    gate(1)
    cq = lat[:, C_CQ:C_CKV]
    ckv = lat[:, C_CKV:C_KR]
    kr_t = lat[:, C_KR:C_KS].T
    k_rope = jnp.concatenate([_rope_rows(kr_t[:MLA_ROPE], tab), one_row], axis=0).T.astype(BF16)

    cqn = (_rms(cq) * qn_ref[...]).astype(BF16)
    q_t = lax.dot_general(wuqt_ref[...], cqn, NT_DIMS, preferred_element_type=F32)
    for hh in range(MLA_HEADS):
        q0 = hh * MLA_QK_PAD
        qh = q_t[hh * (MLA_NOPE + MLA_ROPE):(hh + 1) * (MLA_NOPE + MLA_ROPE)]
        qt_ref[0, 0, q0:q0 + MLA_NOPE, :] = (qh[:MLA_NOPE] * MLA_Q_SCALE).astype(BF16)
        q_rope = jnp.concatenate([_rope_rows(qh[MLA_NOPE:], tab) * MLA_Q_SCALE, rope_pad], axis=0)
        qt_ref[0, 0, q0 + MLA_NOPE:q0 + MLA_QK_PAD, :] = q_rope.astype(BF16)

    ckvn = (_rms(ckv) * kvn_ref[...]).astype(BF16)
    kn = jnp.dot(ckvn, wk_ref[...], preferred_element_type=F32)
    for hh in range(MLA_HEADS):
        c0 = hh * MLA_QK_PAD
        km_ref[0, :, c0:c0 + MLA_NOPE] = kn[:, hh * MLA_NOPE:(hh + 1) * MLA_NOPE].astype(BF16)
        km_ref[0, :, c0 + MLA_NOPE:c0 + MLA_QK_PAD] = k_rope
    vt = lax.dot_general(wvt_ref[...], ckvn, NT_DIMS, preferred_element_type=F32)
    for hh in range(MLA_HEADS):
        v0 = hh * MLA_VT_ROWS
        vt_ref[0, 0, v0:v0 + MLA_V, :] = vt[hh * MLA_V:(hh + 1) * MLA_V].astype(BF16)
        vt_ref[0, 0, v0 + MLA_V:v0 + MLA_VT_ROWS, :] = mla_ones_rows

    qv_t = lax.dot_general(wqvt_ref[...], h, NT_DIMS, preferred_element_type=F32)
    qst_ref[0] = (qv_t[:n_q] * SWA_Q_SCALE).astype(BF16)
    for n in range(SWA_KV_HEADS):
        vs_t = qv_t[n_q + n * SWA_HEAD_DIM:n_q + (n + 1) * SWA_HEAD_DIM].astype(BF16)
        for j in range(tm // BLOCK):
            vst_ref[0, j, n, :SWA_HEAD_DIM, :] = vs_t[:, j * BLOCK:(j + 1) * BLOCK]
            vst_ref[0, j, n, SWA_HEAD_DIM:, :] = swa_ones_rows
    gate(2)
    gate(3)
    ks_ref[0] = jnp.dot(h, win_ref[:, C_KS:C_G], preferred_element_type=F32).astype(BF16)


def _in_proj(x, norm_mix, sc1, sh1, positions, w_in_p, w_qv_t, q_norm, w_uq_t, kv_norm, w_k, w_v_t, tm):
    B, S, D = x.shape
    inv_freq = ROPE_THETA ** (-jnp.arange(0, MLA_ROPE, 2, dtype=F32) / MLA_ROPE)
    kvw = SWA_KV_HEADS * SWA_HEAD_DIM
    tok = lambda w: pl.BlockSpec((1, tm, w), lambda b, i: (b, i, 0))
    tok_t = lambda r: pl.BlockSpec((1, r, tm), lambda b, i: (b, 0, i))
    tile_t = lambda r: pl.BlockSpec((1, 1, r, tm), lambda b, i: (b, i, 0, 0))
    per_batch = pl.BlockSpec((1, 1, D), lambda b, i: (b, 0, 0))
    sds = lambda *shape: jax.ShapeDtypeStruct(shape, BF16)
    return pl.pallas_call(
        _in_proj_kernel,
        out_shape=[sds(B, S // tm, MLA_HEADS * MLA_QK_PAD, tm), sds(B, S, MLA_HEADS * MLA_QK_PAD),
                   sds(B, S // tm, MLA_HEADS * MLA_VT_ROWS, tm), sds(B, SWA_HEADS * SWA_HEAD_DIM, S),
                   sds(B, S, kvw), sds(B, S // BLOCK, SWA_KV_HEADS, SWA_VT_ROWS, BLOCK), sds(B, S, 2 * D)],
        grid=(B, S // tm),
        in_specs=[tok(D), _resident((1, D)), per_batch, per_batch, tok_t(1), _resident((ROPE_HALF, 1)),
                  _resident(w_in_p.shape), _resident(w_qv_t.shape), _resident((1, Q_LORA)),
                  _resident(w_uq_t.shape), _resident((1, KV_LORA)), _resident(w_k.shape),
                  _resident(w_v_t.shape)],
        out_specs=[tile_t(MLA_HEADS * MLA_QK_PAD), tok(MLA_HEADS * MLA_QK_PAD), tile_t(MLA_HEADS * MLA_VT_ROWS),
                   tok_t(SWA_HEADS * SWA_HEAD_DIM), tok(kvw),
                   pl.BlockSpec((1, tm // BLOCK, SWA_KV_HEADS, SWA_VT_ROWS, BLOCK), lambda b, i: (b, i, 0, 0, 0)),
                   tok(2 * D)],
        compiler_params=_params(("arbitrary", "arbitrary")),
        name="in_proj",
    )(x, norm_mix.reshape(1, D), sc1, sh1, positions.reshape(B, 1, S), inv_freq.reshape(ROPE_HALF, 1),
      w_in_p, w_qv_t, q_norm.reshape(1, Q_LORA), w_uq_t, kv_norm.reshape(1, KV_LORA), w_k, w_v_t)


def _mla_kernel(qt_ref, k_ref, vt_ref, o_ref, *, tk, tr, lookahead):
    n_tiles, _, tq = qt_ref.shape[1:]
    S = k_ref.shape[1]
    n_heads = k_ref.shape[2] // MLA_QK_PAD
    n_chunks = S // tk
    tiles_per_chunk = tk // tq
    g0 = MLA_SHIFT_COL // BF16_ROWS * BF16_ROWS
    row = lax.broadcasted_iota(jnp.int32, (BF16_ROWS, tq), 0)
    qk = lambda hd: slice(hd * MLA_QK_PAD, (hd + 1) * MLA_QK_PAD)

    def shifted_q(hd, i):
        qt = qt_ref[0, i, qk(hd), :]
        s = jnp.dot(k_ref[0, :tr, qk(hd)], qt, preferred_element_type=F32)
        ref = jnp.max(s, axis=0, keepdims=True)
        grp = jnp.where(row == MLA_SHIFT_COL - g0, -ref, qt[g0:g0 + BF16_ROWS].astype(F32)).astype(BF16)
        return jnp.concatenate([qt[:g0], grp, qt[g0 + BF16_ROWS:]], axis=0)

    def probs(item, qts):
        hd, i, c = item
        s = jnp.dot(k_ref[0, c * tk:(c + 1) * tk, qk(hd)], qts[hd, i], preferred_element_type=F32)
        return jnp.exp2(s).astype(BF16)

    def values_t(hd, c):
        rows = slice(hd * MLA_VT_ROWS, (hd + 1) * MLA_VT_ROWS)
        return jnp.concatenate([vt_ref[0, c * tiles_per_chunk + j, rows, :] for j in range(tiles_per_chunk)], axis=1)

    tiles = [(hd, i) for hd in range(n_heads) for i in range(n_tiles)]
    items = [(hd, i, c) for hd, i in tiles for c in range(n_chunks)]
    qts = {tiles[0]: shifted_q(*tiles[0])}
    ahead = [probs(item, qts) for item in items[:lookahead]]
    o_sum = jnp.zeros((MLA_V, tq), F32)
    for idx, (hd, i, c) in enumerate(items):
        t = idx // n_chunks
        if c == 0 and t + 1 < len(tiles):
            qts[tiles[t + 1]] = shifted_q(*tiles[t + 1])
        p = ahead.pop(0)
        if idx + lookahead < len(items):
            ahead.append(probs(items[idx + lookahead], qts))
        pv = jnp.dot(values_t(hd, c), p, preferred_element_type=F32)
        acc = pv if c == 0 else acc + pv
        if c == n_chunks - 1:
            o = acc[:MLA_V] / acc[MLA_V:MLA_V + 1]
            o_ref[0, i, hd * MLA_V:(hd + 1) * MLA_V, :] = o.astype(BF16)
            o_sum = o_sum + o
            qts.pop((hd, i))
    bad = jnp.max(jnp.where(jnp.isfinite(o_sum), 0.0, 1.0))

    @pl.when(bad > 0.0)
    def _recompute_with_running_max():
        for hd in range(n_heads):
            def tile_body(i, carry, hd=hd):
                qt = qt_ref[0, i, qk(hd), :]

                def key_tile_body(j, state):
                    m, l, acc = state
                    k = k_ref[0, pl.ds(pl.multiple_of(j * tq, tq), tq), qk(hd)]
                    s = jnp.dot(k, qt, preferred_element_type=F32)
                    m_new = jnp.maximum(m, jnp.max(s, axis=0, keepdims=True))
                    p = jnp.exp2(s - m_new)
                    alpha = jnp.exp2(m - m_new)
                    l = alpha * l + jnp.sum(p, axis=0, keepdims=True)
                    vt = vt_ref[0, j, hd * MLA_VT_ROWS:hd * MLA_VT_ROWS + MLA_V, :]
                    acc = alpha * acc + jnp.dot(vt, p.astype(BF16), preferred_element_type=F32)
                    return m_new, l, acc

                init = (jnp.full((1, tq), -jnp.inf, F32), jnp.zeros((1, tq), F32), jnp.zeros((MLA_V, tq), F32))
                _, l, acc = lax.fori_loop(0, n_tiles, key_tile_body, init)
                o_ref[0, i, hd * MLA_V:(hd + 1) * MLA_V, :] = (acc / l).astype(BF16)
                return carry

            lax.fori_loop(0, n_tiles, tile_body, 0)


def _mla_attn(qt_mla, k_mla, vt_mla, heads_per_step, tk, tr, lookahead):
    B, n_tiles, _, tq = qt_mla.shape
    S = k_mla.shape[1]
    hps = heads_per_step
    return pl.pallas_call(
        functools.partial(_mla_kernel, tk=tk, tr=tr, lookahead=lookahead),
        out_shape=jax.ShapeDtypeStruct((B, n_tiles, MLA_HEADS * MLA_V, tq), BF16),
        grid=(B, MLA_HEADS // hps),
        in_specs=[pl.BlockSpec((1, n_tiles, hps * MLA_QK_PAD, tq), lambda b, h: (b, 0, h, 0)),
                  pl.BlockSpec((1, S, hps * MLA_QK_PAD), lambda b, h: (b, 0, h)),
                  pl.BlockSpec((1, n_tiles, hps * MLA_VT_ROWS, tq), lambda b, h: (b, 0, h, 0))],
        out_specs=pl.BlockSpec((1, n_tiles, hps * MLA_V, tq), lambda b, h: (b, 0, h, 0)),
        compiler_params=_params(("arbitrary", "arbitrary")),
        name="mla_attn",
    )(qt_mla, k_mla, vt_mla)


def _swa_kernel(qt_ref, k_ref, vt_ref, t_ref, o_ref, *, sub_blocks, unit_heads, lookahead):
    S = k_ref.shape[1]
    span = BLOCK + 2 * WINDOW
    n_win = span // BLOCK
    step = pl.program_id(1)
    units = [(sb, hd0) for sb in range(sub_blocks) for hd0 in range(0, SWA_HEADS, unit_heads)]

    def window(sb):
        q0 = (step * sub_blocks + sb) * BLOCK
        start = pl.multiple_of(jnp.clip(q0 - WINDOW, 0, S - span), BLOCK)
        cb0 = jnp.where(q0 == 0, 2, jnp.where(q0 == S - BLOCK, 0, 1))
        return start, cb0

    def scores(sb, hd0):
        start, cb0 = window(sb)
        n = hd0 // SWA_GROUP
        heads = range(hd0, hd0 + unit_heads)
        qt = jnp.concatenate(
            [qt_ref[0, hd * SWA_HEAD_DIM:(hd + 1) * SWA_HEAD_DIM, sb * BLOCK:(sb + 1) * BLOCK] for hd in heads],
            axis=1)
        kw = k_ref[0, pl.ds(start, span), n * SWA_HEAD_DIM:(n + 1) * SWA_HEAD_DIM]
        bias = jnp.concatenate(
            [jnp.concatenate([t_ref[hd, cb0 + c] for c in range(n_win)], axis=0) for hd in heads], axis=1)
        return jnp.dot(kw, qt, preferred_element_type=F32) + bias

    def values_t(sb, hd0):
        start, _ = window(sb)
        blk0 = start // BLOCK
        return jnp.concatenate([vt_ref[0, blk0 + c, hd0 // SWA_GROUP] for c in range(n_win)], axis=1)

    def store(sb, hd0, o):
        for g in range(unit_heads):
            hd = hd0 + g
            o_ref[0, hd * SWA_HEAD_DIM:(hd + 1) * SWA_HEAD_DIM, sb * BLOCK:(sb + 1) * BLOCK] = (
                o[:, g * BLOCK:(g + 1) * BLOCK].astype(BF16))

    o_sum = jnp.zeros((SWA_HEAD_DIM, unit_heads * BLOCK), F32)
    probs = lambda unit: jnp.exp2(scores(*unit)).astype(BF16)
    ahead = [probs(unit) for unit in units[:lookahead]]
    for u, unit in enumerate(units):
        p = ahead.pop(0)
        if u + lookahead < len(units):
            ahead.append(probs(units[u + lookahead]))
        ov = jnp.dot(values_t(*unit), p, preferred_element_type=F32)
        o = ov[:SWA_HEAD_DIM] / (ov[SWA_HEAD_DIM:SWA_HEAD_DIM + 1] + 1.0)
        store(*unit, o)
        o_sum = o_sum + o
    bad = jnp.max(jnp.where(jnp.isfinite(o_sum), 0.0, 1.0))

    @pl.when(bad > 0.0)
    def _recompute_with_row_max():
        for unit in units:
            s = scores(*unit)
            m = jnp.maximum(jnp.max(s, axis=0, keepdims=True), 0.0)
            p = jnp.exp2(s - m)
            l = jnp.sum(p, axis=0, keepdims=True) + jnp.exp2(-m)
            ov = jnp.dot(values_t(*unit), p.astype(BF16), preferred_element_type=F32)
            store(*unit, ov[:SWA_HEAD_DIM] / l)


def _swa_attn(qs_t, ks, vs_t, bias_tab, sub_blocks, unit_heads, lookahead):
    B, W, S = qs_t.shape
    kvw = ks.shape[2]
    tq = sub_blocks * BLOCK
    return pl.pallas_call(
        functools.partial(_swa_kernel, sub_blocks=sub_blocks, unit_heads=unit_heads, lookahead=lookahead),
        out_shape=jax.ShapeDtypeStruct((B, W, S), BF16),
        grid=(B, S // tq),
        in_specs=[pl.BlockSpec((1, W, tq), lambda b, i: (b, 0, i)),
                  pl.BlockSpec((1, S, kvw), lambda b, i: (b, 0, 0)),
                  pl.BlockSpec((1, S // BLOCK, SWA_KV_HEADS, SWA_VT_ROWS, BLOCK), lambda b, i: (b, 0, 0, 0, 0)),
                  _resident(bias_tab.shape)],
        out_specs=pl.BlockSpec((1, W, tq), lambda b, i: (b, 0, i)),
        compiler_params=_params(("arbitrary", "arbitrary")),
        name="swa_attn",
    )(qs_t, ks, vs_t, bias_tab)


def _out_mlp_kernel(x_ref, oat_ref, obt_ref, g_ref, g1_ref, sc_ref, sh_ref, g2_ref, nmlp_ref, nfin_ref,
                    woa_ref, wob_ref, wout_ref, w1_ref, w2_ref, o_ref, *, tf, n_groups):
    tm, D = x_ref.shape[1:]
    groups = [slice(r, r + tm // n_groups) for r in range(0, tm, tm // n_groups)]

    def attn_proj(rows):
        y_a = lax.dot_general(oat_ref[0, 0, :, rows], woa_ref[...], TN_DIMS, preferred_element_type=F32)
        y_b = lax.dot_general(obt_ref[0, :, rows], wob_ref[...], TN_DIMS, preferred_element_type=F32)
        return y_a, y_b

    def residual(rows, y):
        merged = g_ref[0, rows, :D].astype(F32) * y[0] + g_ref[0, rows, D:].astype(F32) * y[1]
        att = jnp.dot(merged.astype(BF16), wout_ref[...], preferred_element_type=F32)
        return x_ref[0, rows, :] + g1_ref[0] * att

    def mlp(rows, x1):
        h = (_rms(x1) * nmlp_ref[...] * (1.0 + sc_ref[0]) + sh_ref[0]).astype(BF16)
        ff = jnp.zeros_like(x1)
        for c in range(w1_ref.shape[1] // tf):
            a = jnp.dot(h, w1_ref[:, c * tf:(c + 1) * tf], preferred_element_type=F32)
            a = jnp.square(jnp.maximum(a, 0.0)).astype(BF16)
            ff = ff + jnp.dot(a, w2_ref[c * tf:(c + 1) * tf, :], preferred_element_type=F32)
        x2 = x1 + g2_ref[0] * ff
        o_ref[0, rows, :] = _rms(x2) * nfin_ref[...]

    ys = [attn_proj(rows) for rows in groups]
    x1s = [residual(rows, y) for rows, y in zip(groups, ys)]
    for rows, x1 in zip(groups, x1s):
        mlp(rows, x1)


def _out_mlp(x, o_mla, o_swa, gates, g1, sc2, sh2, g2, norm_mlp, norm_final,
             w_o_mla, w_o_swa, w_out, w_ff1, w_ff2, tm, tf, n_groups):
    B, S, D = x.shape
    tok = lambda w: pl.BlockSpec((1, tm, w), lambda b, i: (b, i, 0))
    per_batch = pl.BlockSpec((1, 1, D), lambda b, i: (b, 0, 0))
    return pl.pallas_call(
        functools.partial(_out_mlp_kernel, tf=tf, n_groups=n_groups),
        out_shape=jax.ShapeDtypeStruct((B, S, D), F32),
        grid=(B, S // tm),
        in_specs=[tok(D), pl.BlockSpec((1, 1, D, tm), lambda b, i: (b, i, 0, 0)),
                  pl.BlockSpec((1, D, tm), lambda b, i: (b, 0, i)), tok(2 * D),
                  per_batch, per_batch, per_batch, per_batch,
                  _resident((1, D)), _resident((1, D)),
                  _resident(w_o_mla.shape), _resident(w_o_swa.shape), _resident(w_out.shape),
                  _resident(w_ff1.shape), _resident(w_ff2.shape)],
        out_specs=tok(D),
        compiler_params=_params(("arbitrary", "arbitrary")),
        name="out_mlp",
    )(x, o_mla, o_swa, gates, g1, sc2, sh2, g2, norm_mlp.reshape(1, D), norm_final.reshape(1, D),
      w_o_mla, w_o_swa, w_out, w_ff1, w_ff2)


def _pack_w_in(w_in):
    kr1 = Q_LORA + KV_LORA + MLA_ROPE
    qs1 = kr1 + SWA_HEADS * SWA_HEAD_DIM
    ks1 = qs1 + SWA_KV_HEADS * SWA_HEAD_DIM
    vs1 = ks1 + SWA_KV_HEADS * SWA_HEAD_DIM
    pad = jnp.zeros((w_in.shape[0], LANES - MLA_ROPE), w_in.dtype)
    w_tok = jnp.concatenate([w_in[:, :kr1], pad, w_in[:, qs1:ks1], w_in[:, vs1:]], axis=1)
    w_t = jnp.concatenate([w_in[:, kr1:qs1], w_in[:, ks1:vs1]], axis=1).T
    return w_tok.astype(BF16), w_t.astype(BF16)


def _split_heads(w, n_first):
    w3 = w.reshape(w.shape[0], MLA_HEADS, -1)
    first = w3[:, :, :n_first].reshape(w.shape[0], -1)
    second = w3[:, :, n_first:].reshape(w.shape[0], -1)
    return first.astype(BF16), second.astype(BF16)


def kernel(x, c, positions, w_ada, b_ada, norm_mix, w_in, q_norm, w_uq, kv_norm, w_ukv, rel_bias, sink,
           w_o_mla, w_o_swa, w_out, norm_mlp, w_ff1, w_ff2, norm_final):
    B, S, D = x.shape
    assert w_ada.shape[0] == 1, "single-layer block"
    assert D == D_MODEL and S % (4 * BLOCK) == 0 and S >= BLOCK + 2 * WINDOW

    mod = _ada_mod(c, w_ada[0], b_ada[0])
    sh1, sc1, g1, sh2, sc2, g2 = [m.reshape(B, 1, D) for m in jnp.split(mod, N_MOD, axis=-1)]

    bias_tab = _swa_bias(rel_bias, sink[0])

    tm = 512
    w_k, w_v = _split_heads(w_ukv[0], MLA_NOPE)
    w_in_tok, w_in_t = _pack_w_in(w_in[0])
    qt_mla, k_mla, vt_mla, qs_t, ks, vs_t, gates = _in_proj(
        x, norm_mix[0], sc1, sh1, positions, w_in_tok, w_in_t, q_norm[0], w_uq[0].T.astype(BF16),
        kv_norm[0], w_k, w_v.T, tm=tm)

    ot_mla = _mla_attn(qt_mla, k_mla, vt_mla, heads_per_step=2, tk=1024, tr=256, lookahead=1)
    ot_swa = _swa_attn(qs_t, ks, vs_t, bias_tab, sub_blocks=8, unit_heads=2, lookahead=2)

    return _out_mlp(x, ot_mla, ot_swa, gates, g1, sc2, sh2, g2, norm_mlp[0], norm_final,
                    w_o_mla[0].astype(BF16), w_o_swa[0].astype(BF16), w_out[0].astype(BF16),
                    w_ff1[0].astype(BF16), w_ff2[0].astype(BF16), tm=tm, tf=1024, n_groups=2)
```

```python
import functools
import math

import jax
import jax.numpy as jnp
from jax import lax
from jax.experimental import pallas as pl
from jax.experimental.pallas import tpu as pltpu

F32 = jnp.float32
BF16 = jnp.bfloat16

D_MODEL = 1024
MLA_HEADS = 8
MLA_NOPE = 128
MLA_ROPE = 64
MLA_V = 128
Q_LORA = 384
KV_LORA = 256
ROPE_THETA = 10000.0
SWA_HEADS = 8
SWA_KV_HEADS = 2
SWA_GROUP = SWA_HEADS // SWA_KV_HEADS
SWA_HEAD_DIM = 128
WINDOW = 128
BLOCK = 128
N_BUCKETS = 32
N_MOD = 6
EPS = 1e-6
NEG_INF = -1e30

ROPE_HALF = MLA_ROPE // 2
LANES = 128
BF16_ROWS = 16
MLA_QK_PAD = 256
MLA_SHIFT_COL = MLA_NOPE + MLA_ROPE
MLA_VT_ROWS = MLA_V + 16
SWA_VT_ROWS = SWA_HEAD_DIM + 16
VMEM_LIMIT = 56 * 1024 * 1024

LOG2_E = math.log2(math.e)
MLA_Q_SCALE = (MLA_NOPE + MLA_ROPE) ** -0.5 * LOG2_E
SWA_Q_SCALE = SWA_HEAD_DIM ** -0.5 * LOG2_E

C_CQ = 0
C_CKV = C_CQ + Q_LORA
C_KR = C_CKV + KV_LORA
C_KS = C_KR + LANES
C_G = C_KS + SWA_KV_HEADS * SWA_HEAD_DIM
GATE_CHUNK = 2 * D_MODEL // 4

T5_LARGE_THRESHOLDS = (12, 16, 23, 32, 46, 64, 91)

NT_DIMS = (((1,), (1,)), ((), ()))
TN_DIMS = (((0,), (0,)), ((), ()))


def _resident(shape):
    nd = len(shape)
    return pl.BlockSpec(shape, lambda *_: (0,) * nd, pipeline_mode=pl.Buffered(1))


def _params(semantics):
    return pltpu.CompilerParams(dimension_semantics=semantics, vmem_limit_bytes=VMEM_LIMIT)


def _rms(x):
    return x * lax.rsqrt(jnp.mean(x * x, axis=-1, keepdims=True) + EPS)


MOD_SH1, MOD_SC1, MOD_G1, MOD_SH2, MOD_SC2, MOD_G2 = range(N_MOD)


def _mod_row(mod_ref, chunk):
    return mod_ref[pl.ds(pl.program_id(0), 1), chunk * D_MODEL:(chunk + 1) * D_MODEL]


def _ada_kernel(c_ref, w_ref, b_ref, o_ref):
    c = c_ref[...]
    c_act = c * jax.nn.sigmoid(c)
    o_ref[...] = jnp.dot(c_act, w_ref[...], preferred_element_type=F32,
                         precision=lax.Precision.HIGHEST) + b_ref[...]


def _ada_mod(c, w_ada, b_ada):
    B, D = c.shape
    N = w_ada.shape[1]
    tn = 1024
    return pl.pallas_call(
        _ada_kernel,
        out_shape=jax.ShapeDtypeStruct((B, N), F32),
        grid=(N // tn,),
        in_specs=[pl.BlockSpec((B, D), lambda j: (0, 0)),
                  pl.BlockSpec((D, tn), lambda j: (0, j)),
                  pl.BlockSpec((1, tn), lambda j: (0, j))],
        out_specs=pl.BlockSpec((B, tn), lambda j: (0, j)),
        compiler_params=_params(("arbitrary",)),
        name="ada_mod",
    )(c, w_ada, b_ada.reshape(1, N))


def _swa_bias_kernel(tab_ref, sink_ref, o_ref):
    h = pl.program_id(0)
    half = N_BUCKETS // 2
    max_exact = half // 2
    kj = lax.broadcasted_iota(jnp.int32, (BLOCK, LANES), 0)
    qi = lax.broadcasted_iota(jnp.int32, (BLOCK, LANES), 1)
    for cb in range(5):
        rel = kj - qi + (cb * LANES - 2 * WINDOW)
        n = jnp.abs(rel)
        large = jnp.full_like(n, max_exact)
        for t in T5_LARGE_THRESHOLDS:
            large = large + jnp.where(n >= t, 1, 0)
        bucket = jnp.where(rel > 0, half, 0) + jnp.where(n < max_exact, n, large)
        bias = jnp.zeros((BLOCK, LANES), F32)
        for b in range(N_BUCKETS):
            bias = jnp.where(bucket == b, tab_ref[b, h], bias)
        o_ref[0, cb] = jnp.where(n <= WINDOW, (bias - sink_ref[0, h]) * LOG2_E, NEG_INF)


def _swa_bias(rel_bias, sink):
    return pl.pallas_call(
        _swa_bias_kernel,
        out_shape=jax.ShapeDtypeStruct((SWA_HEADS, 5, BLOCK, LANES), F32),
        grid=(SWA_HEADS,),
        in_specs=[pl.BlockSpec(memory_space=pltpu.SMEM), pl.BlockSpec(memory_space=pltpu.SMEM)],
        out_specs=pl.BlockSpec((1, 5, BLOCK, LANES), lambda h: (h, 0, 0, 0)),
        compiler_params=_params(("arbitrary",)),
        name="swa_bias",
    )(rel_bias, sink.reshape(1, SWA_HEADS))


def _rope_rows(x, tab):
    x1, x2 = x[:ROPE_HALF], x[ROPE_HALF:]
    cos, sin = tab[:ROPE_HALF], tab[ROPE_HALF:]
    return jnp.concatenate([x1 * cos - x2 * sin, x2 * cos + x1 * sin], axis=0)


def _in_proj_kernel(x_ref, nm_ref, mod_ref, pos_ref, inv_ref, win_ref, wqvt_ref, qn_ref, wuqt_ref, kvn_ref,
                    wk_ref, wvt_ref, qt_ref, km_ref, vt_ref, qst_ref, ks_ref, vst_ref, g_ref):
    tm = x_ref.shape[1]
    n_q = SWA_HEADS * SWA_HEAD_DIM
    rope_pad = jnp.zeros((MLA_QK_PAD - MLA_NOPE - MLA_ROPE, tm), F32)
    one_row = jnp.where(lax.broadcasted_iota(jnp.int32, rope_pad.shape, 0) == 0, 1.0, 0.0)
    mla_ones_rows = jnp.where(lax.broadcasted_iota(jnp.int32, (MLA_VT_ROWS - MLA_V, tm), 0) == 0,
                              1.0, 0.0).astype(BF16)
    swa_ones_rows = jnp.where(lax.broadcasted_iota(jnp.int32, (SWA_VT_ROWS - SWA_HEAD_DIM, BLOCK), 0) == 0,
                              1.0, 0.0).astype(BF16)

    h = (_rms(x_ref[0]) * nm_ref[...] * (1.0 + _mod_row(mod_ref, MOD_SC1)) + _mod_row(mod_ref, MOD_SH1)).astype(BF16)
    ang = inv_ref[...] * pos_ref[0].astype(F32)
    tab = jnp.concatenate([jnp.cos(ang), jnp.sin(ang)], axis=0)

    def gate(j):
        g = jnp.dot(h, win_ref[:, C_G + j * GATE_CHUNK:C_G + (j + 1) * GATE_CHUNK], preferred_element_type=F32)
        g_ref[0, :, j * GATE_CHUNK:(j + 1) * GATE_CHUNK] = (0.5 * jnp.tanh(0.5 * g) + 0.5).astype(BF16)

    lat = jnp.dot(h, win_ref[:, C_CQ:C_KS], preferred_element_type=F32)
    gate(0)
    gate(1)
    cq = lat[:, C_CQ:C_CKV]
    ckv = lat[:, C_CKV:C_KR]
    kr_t = lat[:, C_KR:C_KS].T
    k_rope = jnp.concatenate([_rope_rows(kr_t[:MLA_ROPE], tab), one_row], axis=0).T.astype(BF16)

    cqn = (_rms(cq) * qn_ref[...]).astype(BF16)
    q_t = lax.dot_general(wuqt_ref[...], cqn, NT_DIMS, preferred_element_type=F32)
    for hh in range(MLA_HEADS):
        q0 = hh * MLA_QK_PAD
        qh = q_t[hh * (MLA_NOPE + MLA_ROPE):(hh + 1) * (MLA_NOPE + MLA_ROPE)]
        qt_ref[0, 0, q0:q0 + MLA_NOPE, :] = (qh[:MLA_NOPE] * MLA_Q_SCALE).astype(BF16)
        q_rope = jnp.concatenate([_rope_rows(qh[MLA_NOPE:], tab) * MLA_Q_SCALE, rope_pad], axis=0)
        qt_ref[0, 0, q0 + MLA_NOPE:q0 + MLA_QK_PAD, :] = q_rope.astype(BF16)

    ckvn = (_rms(ckv) * kvn_ref[...]).astype(BF16)
    kn = jnp.dot(ckvn, wk_ref[...], preferred_element_type=F32)
    for hh in range(MLA_HEADS):
        c0 = hh * MLA_QK_PAD
        km_ref[0, :, c0:c0 + MLA_NOPE] = kn[:, hh * MLA_NOPE:(hh + 1) * MLA_NOPE].astype(BF16)
        km_ref[0, :, c0 + MLA_NOPE:c0 + MLA_QK_PAD] = k_rope
    vt = lax.dot_general(wvt_ref[...], ckvn, NT_DIMS, preferred_element_type=F32)
    for hh in range(MLA_HEADS):
        v0 = hh * MLA_VT_ROWS
        vt_ref[0, 0, v0:v0 + MLA_V, :] = vt[hh * MLA_V:(hh + 1) * MLA_V].astype(BF16)
        vt_ref[0, 0, v0 + MLA_V:v0 + MLA_VT_ROWS, :] = mla_ones_rows

    qv_t = lax.dot_general(wqvt_ref[...], h, NT_DIMS, preferred_element_type=F32)
    qst_ref[0] = (qv_t[:n_q] * SWA_Q_SCALE).astype(BF16)
    for n in range(SWA_KV_HEADS):
        vs_t = qv_t[n_q + n * SWA_HEAD_DIM:n_q + (n + 1) * SWA_HEAD_DIM].astype(BF16)
        for j in range(tm // BLOCK):
            vst_ref[0, j, n, :SWA_HEAD_DIM, :] = vs_t[:, j * BLOCK:(j + 1) * BLOCK]
            vst_ref[0, j, n, SWA_HEAD_DIM:, :] = swa_ones_rows
    gate(2)
    gate(3)
    ks_ref[0] = jnp.dot(h, win_ref[:, C_KS:C_G], preferred_element_type=F32).astype(BF16)


def _in_proj(x, norm_mix, mod, positions, w_in_p, w_qv_t, q_norm, w_uq_t, kv_norm, w_k, w_v_t, tm):
    B, S, D = x.shape
    inv_freq = ROPE_THETA ** (-jnp.arange(0, MLA_ROPE, 2, dtype=F32) / MLA_ROPE)
    kvw = SWA_KV_HEADS * SWA_HEAD_DIM
    tok = lambda w: pl.BlockSpec((1, tm, w), lambda b, i: (b, i, 0))
    tok_t = lambda r: pl.BlockSpec((1, r, tm), lambda b, i: (b, 0, i))
    tile_t = lambda r: pl.BlockSpec((1, 1, r, tm), lambda b, i: (b, i, 0, 0))
    sds = lambda *shape: jax.ShapeDtypeStruct(shape, BF16)
    return pl.pallas_call(
        _in_proj_kernel,
        out_shape=[sds(B, S // tm, MLA_HEADS * MLA_QK_PAD, tm), sds(B, S, MLA_HEADS * MLA_QK_PAD),
                   sds(B, S // tm, MLA_HEADS * MLA_VT_ROWS, tm), sds(B, SWA_HEADS * SWA_HEAD_DIM, S),
                   sds(B, S, kvw), sds(B, S // BLOCK, SWA_KV_HEADS, SWA_VT_ROWS, BLOCK), sds(B, S, 2 * D)],
        grid=(B, S // tm),
        in_specs=[tok(D), _resident((1, D)), _resident(mod.shape), tok_t(1), _resident((ROPE_HALF, 1)),
                  _resident(w_in_p.shape), _resident(w_qv_t.shape), _resident((1, Q_LORA)),
                  _resident(w_uq_t.shape), _resident((1, KV_LORA)), _resident(w_k.shape),
                  _resident(w_v_t.shape)],
        out_specs=[tile_t(MLA_HEADS * MLA_QK_PAD), tok(MLA_HEADS * MLA_QK_PAD), tile_t(MLA_HEADS * MLA_VT_ROWS),
                   tok_t(SWA_HEADS * SWA_HEAD_DIM), tok(kvw),
                   pl.BlockSpec((1, tm // BLOCK, SWA_KV_HEADS, SWA_VT_ROWS, BLOCK), lambda b, i: (b, i, 0, 0, 0)),
                   tok(2 * D)],
        compiler_params=_params(("arbitrary", "arbitrary")),
        name="in_proj",
    )(x, norm_mix.reshape(1, D), mod, positions.reshape(B, 1, S), inv_freq.reshape(ROPE_HALF, 1),
      w_in_p, w_qv_t, q_norm.reshape(1, Q_LORA), w_uq_t, kv_norm.reshape(1, KV_LORA), w_k, w_v_t)


def _mla_kernel(qt_ref, k_ref, vt_ref, o_ref, *, tk, tr, lookahead):
    n_tiles, _, tq = qt_ref.shape[1:]
    S = k_ref.shape[1]
    n_heads = k_ref.shape[2] // MLA_QK_PAD
    n_chunks = S // tk
    tiles_per_chunk = tk // tq
    g0 = MLA_SHIFT_COL // BF16_ROWS * BF16_ROWS
    row = lax.broadcasted_iota(jnp.int32, (BF16_ROWS, tq), 0)
    qk = lambda hd: slice(hd * MLA_QK_PAD, (hd + 1) * MLA_QK_PAD)

    def shifted_q(hd, i):
        qt = qt_ref[0, i, qk(hd), :]
        s = jnp.dot(k_ref[0, :tr, qk(hd)], qt, preferred_element_type=F32)
        ref = jnp.max(s, axis=0, keepdims=True)
        grp = jnp.where(row == MLA_SHIFT_COL - g0, -ref, qt[g0:g0 + BF16_ROWS].astype(F32)).astype(BF16)
        return jnp.concatenate([qt[:g0], grp, qt[g0 + BF16_ROWS:]], axis=0)

    def probs(item, qts):
        hd, i, c = item
        s = jnp.dot(k_ref[0, c * tk:(c + 1) * tk, qk(hd)], qts[hd, i], preferred_element_type=F32)
        return jnp.exp2(s).astype(BF16)

    def values_t(hd, c):
        rows = slice(hd * MLA_VT_ROWS, (hd + 1) * MLA_VT_ROWS)
        return jnp.concatenate([vt_ref[0, c * tiles_per_chunk + j, rows, :] for j in range(tiles_per_chunk)], axis=1)

    tiles = [(hd, i) for hd in range(n_heads) for i in range(n_tiles)]
    items = [(hd, i, c) for hd, i in tiles for c in range(n_chunks)]
    qts = {tiles[0]: shifted_q(*tiles[0])}
    ahead = [probs(item, qts) for item in items[:lookahead]]
    o_sum = jnp.zeros((MLA_V, tq), F32)
    for idx, (hd, i, c) in enumerate(items):
        t = idx // n_chunks
        if c == 0 and t + 1 < len(tiles):
            qts[tiles[t + 1]] = shifted_q(*tiles[t + 1])
        p = ahead.pop(0)
        if idx + lookahead < len(items):
            ahead.append(probs(items[idx + lookahead], qts))
        pv = jnp.dot(values_t(hd, c), p, preferred_element_type=F32)
        acc = pv if c == 0 else acc + pv
        if c == n_chunks - 1:
            o = acc[:MLA_V] / acc[MLA_V:MLA_V + 1]
            o_ref[0, i, hd * MLA_V:(hd + 1) * MLA_V, :] = o.astype(BF16)
            o_sum = o_sum + o
            qts.pop((hd, i))
    bad = jnp.max(jnp.where(jnp.isfinite(o_sum), 0.0, 1.0))

    @pl.when(bad > 0.0)
    def _recompute_with_running_max():
        for hd in range(n_heads):
            def tile_body(i, carry, hd=hd):
                qt = qt_ref[0, i, qk(hd), :]

                def key_tile_body(j, state):
                    m, l, acc = state
                    k = k_ref[0, pl.ds(pl.multiple_of(j * tq, tq), tq), qk(hd)]
                    s = jnp.dot(k, qt, preferred_element_type=F32)
                    m_new = jnp.maximum(m, jnp.max(s, axis=0, keepdims=True))
                    p = jnp.exp2(s - m_new)
                    alpha = jnp.exp2(m - m_new)
                    l = alpha * l + jnp.sum(p, axis=0, keepdims=True)
                    vt = vt_ref[0, j, hd * MLA_VT_ROWS:hd * MLA_VT_ROWS + MLA_V, :]
                    acc = alpha * acc + jnp.dot(vt, p.astype(BF16), preferred_element_type=F32)
                    return m_new, l, acc

                init = (jnp.full((1, tq), -jnp.inf, F32), jnp.zeros((1, tq), F32), jnp.zeros((MLA_V, tq), F32))
                _, l, acc = lax.fori_loop(0, n_tiles, key_tile_body, init)
                o_ref[0, i, hd * MLA_V:(hd + 1) * MLA_V, :] = (acc / l).astype(BF16)
                return carry

            lax.fori_loop(0, n_tiles, tile_body, 0)


def _mla_attn(qt_mla, k_mla, vt_mla, heads_per_step, tk, tr, lookahead):
    B, n_tiles, _, tq = qt_mla.shape
    S = k_mla.shape[1]
    hps = heads_per_step
    return pl.pallas_call(
        functools.partial(_mla_kernel, tk=tk, tr=tr, lookahead=lookahead),
        out_shape=jax.ShapeDtypeStruct((B, n_tiles, MLA_HEADS * MLA_V, tq), BF16),
        grid=(B, MLA_HEADS // hps),
        in_specs=[pl.BlockSpec((1, n_tiles, hps * MLA_QK_PAD, tq), lambda b, h: (b, 0, h, 0)),
                  pl.BlockSpec((1, S, hps * MLA_QK_PAD), lambda b, h: (b, 0, h)),
                  pl.BlockSpec((1, n_tiles, hps * MLA_VT_ROWS, tq), lambda b, h: (b, 0, h, 0))],
        out_specs=pl.BlockSpec((1, n_tiles, hps * MLA_V, tq), lambda b, h: (b, 0, h, 0)),
        compiler_params=_params(("arbitrary", "arbitrary")),
        name="mla_attn",
    )(qt_mla, k_mla, vt_mla)


def _swa_kernel(qt_ref, k_ref, vt_ref, t_ref, o_ref, *, sub_blocks, unit_heads, lookahead):
    S = k_ref.shape[1]
    span = BLOCK + 2 * WINDOW
    n_win = span // BLOCK
    step = pl.program_id(1)
    units = [(sb, hd0) for sb in range(sub_blocks) for hd0 in range(0, SWA_HEADS, unit_heads)]

    def window(sb):
        q0 = (step * sub_blocks + sb) * BLOCK
        start = pl.multiple_of(jnp.clip(q0 - WINDOW, 0, S - span), BLOCK)
        cb0 = jnp.where(q0 == 0, 2, jnp.where(q0 == S - BLOCK, 0, 1))
        return start, cb0

    def scores(sb, hd0):
        start, cb0 = window(sb)
        n = hd0 // SWA_GROUP
        heads = range(hd0, hd0 + unit_heads)
        qt = jnp.concatenate(
            [qt_ref[0, hd * SWA_HEAD_DIM:(hd + 1) * SWA_HEAD_DIM, sb * BLOCK:(sb + 1) * BLOCK] for hd in heads],
            axis=1)
        kw = k_ref[0, pl.ds(start, span), n * SWA_HEAD_DIM:(n + 1) * SWA_HEAD_DIM]
        bias = jnp.concatenate(
            [jnp.concatenate([t_ref[hd, cb0 + c] for c in range(n_win)], axis=0) for hd in heads], axis=1)
        return jnp.dot(kw, qt, preferred_element_type=F32) + bias

    def values_t(sb, hd0):
        start, _ = window(sb)
        blk0 = start // BLOCK
        return jnp.concatenate([vt_ref[0, blk0 + c, hd0 // SWA_GROUP] for c in range(n_win)], axis=1)

    def store(sb, hd0, o):
        for g in range(unit_heads):
            hd = hd0 + g
            o_ref[0, hd * SWA_HEAD_DIM:(hd + 1) * SWA_HEAD_DIM, sb * BLOCK:(sb + 1) * BLOCK] = (
                o[:, g * BLOCK:(g + 1) * BLOCK].astype(BF16))

    o_sum = jnp.zeros((SWA_HEAD_DIM, unit_heads * BLOCK), F32)
    probs = lambda unit: jnp.exp2(scores(*unit)).astype(BF16)
    ahead = [probs(unit) for unit in units[:lookahead]]
    for u, unit in enumerate(units):
        p = ahead.pop(0)
        if u + lookahead < len(units):
            ahead.append(probs(units[u + lookahead]))
        ov = jnp.dot(values_t(*unit), p, preferred_element_type=F32)
        o = ov[:SWA_HEAD_DIM] / (ov[SWA_HEAD_DIM:SWA_HEAD_DIM + 1] + 1.0)
        store(*unit, o)
        o_sum = o_sum + o
    bad = jnp.max(jnp.where(jnp.isfinite(o_sum), 0.0, 1.0))

    @pl.when(bad > 0.0)
    def _recompute_with_row_max():
        for unit in units:
            s = scores(*unit)
            m = jnp.maximum(jnp.max(s, axis=0, keepdims=True), 0.0)
            p = jnp.exp2(s - m)
            l = jnp.sum(p, axis=0, keepdims=True) + jnp.exp2(-m)
            ov = jnp.dot(values_t(*unit), p.astype(BF16), preferred_element_type=F32)
            store(*unit, ov[:SWA_HEAD_DIM] / l)


def _swa_attn(qs_t, ks, vs_t, bias_tab, sub_blocks, unit_heads, lookahead):
    B, W, S = qs_t.shape
    kvw = ks.shape[2]
    tq = sub_blocks * BLOCK
    return pl.pallas_call(
        functools.partial(_swa_kernel, sub_blocks=sub_blocks, unit_heads=unit_heads, lookahead=lookahead),
        out_shape=jax.ShapeDtypeStruct((B, W, S), BF16),
        grid=(B, S // tq),
        in_specs=[pl.BlockSpec((1, W, tq), lambda b, i: (b, 0, i)),
                  pl.BlockSpec((1, S, kvw), lambda b, i: (b, 0, 0)),
                  pl.BlockSpec((1, S // BLOCK, SWA_KV_HEADS, SWA_VT_ROWS, BLOCK), lambda b, i: (b, 0, 0, 0, 0)),
                  _resident(bias_tab.shape)],
        out_specs=pl.BlockSpec((1, W, tq), lambda b, i: (b, 0, i)),
        compiler_params=_params(("arbitrary", "arbitrary")),
        name="swa_attn",
    )(qs_t, ks, vs_t, bias_tab)


def _out_mlp_kernel(x_ref, oat_ref, obt_ref, g_ref, mod_ref, nmlp_ref, nfin_ref,
                    woa_ref, wob_ref, wout_ref, w1_ref, w2_ref, o_ref, *, tf, n_groups):
    tm, D = x_ref.shape[1:]
    groups = [slice(r, r + tm // n_groups) for r in range(0, tm, tm // n_groups)]

    def attn_proj(rows):
        y_a = lax.dot_general(oat_ref[0, 0, :, rows], woa_ref[...], TN_DIMS, preferred_element_type=F32)
        y_b = lax.dot_general(obt_ref[0, :, rows], wob_ref[...], TN_DIMS, preferred_element_type=F32)
        return y_a, y_b

    def residual(rows, y):
        merged = g_ref[0, rows, :D].astype(F32) * y[0] + g_ref[0, rows, D:].astype(F32) * y[1]
        att = jnp.dot(merged.astype(BF16), wout_ref[...], preferred_element_type=F32)
        return x_ref[0, rows, :] + _mod_row(mod_ref, MOD_G1) * att

    def mlp(rows, x1):
        h = (_rms(x1) * nmlp_ref[...] * (1.0 + _mod_row(mod_ref, MOD_SC2)) + _mod_row(mod_ref, MOD_SH2)).astype(BF16)
        ff = jnp.zeros_like(x1)
        for c in range(w1_ref.shape[1] // tf):
            a = jnp.dot(h, w1_ref[:, c * tf:(c + 1) * tf], preferred_element_type=F32)
            a = jnp.square(jnp.maximum(a, 0.0)).astype(BF16)
            ff = ff + jnp.dot(a, w2_ref[c * tf:(c + 1) * tf, :], preferred_element_type=F32)
        x2 = x1 + _mod_row(mod_ref, MOD_G2) * ff
        o_ref[0, rows, :] = _rms(x2) * nfin_ref[...]

    ys = [attn_proj(rows) for rows in groups]
    x1s = [residual(rows, y) for rows, y in zip(groups, ys)]
    for rows, x1 in zip(groups, x1s):
        mlp(rows, x1)


def _out_mlp(x, o_mla, o_swa, gates, mod, norm_mlp, norm_final,
             w_o_mla, w_o_swa, w_out, w_ff1, w_ff2, tm, tf, n_groups):
    B, S, D = x.shape
    tok = lambda w: pl.BlockSpec((1, tm, w), lambda b, i: (b, i, 0))
    return pl.pallas_call(
        functools.partial(_out_mlp_kernel, tf=tf, n_groups=n_groups),
        out_shape=jax.ShapeDtypeStruct((B, S, D), F32),
        grid=(B, S // tm),
        in_specs=[tok(D), pl.BlockSpec((1, 1, D, tm), lambda b, i: (b, i, 0, 0)),
                  pl.BlockSpec((1, D, tm), lambda b, i: (b, 0, i)), tok(2 * D),
                  _resident(mod.shape), _resident((1, D)), _resident((1, D)),
                  _resident(w_o_mla.shape), _resident(w_o_swa.shape), _resident(w_out.shape),
                  _resident(w_ff1.shape), _resident(w_ff2.shape)],
        out_specs=tok(D),
        compiler_params=_params(("arbitrary", "arbitrary")),
        name="out_mlp",
    )(x, o_mla, o_swa, gates, mod, norm_mlp.reshape(1, D), norm_final.reshape(1, D),
      w_o_mla, w_o_swa, w_out, w_ff1, w_ff2)


def _pack_w_in(w_in):
    kr1 = Q_LORA + KV_LORA + MLA_ROPE
    qs1 = kr1 + SWA_HEADS * SWA_HEAD_DIM
    ks1 = qs1 + SWA_KV_HEADS * SWA_HEAD_DIM
    vs1 = ks1 + SWA_KV_HEADS * SWA_HEAD_DIM
    pad = jnp.zeros((w_in.shape[0], LANES - MLA_ROPE), w_in.dtype)
    w_tok = jnp.concatenate([w_in[:, :kr1], pad, w_in[:, qs1:ks1], w_in[:, vs1:]], axis=1)
    w_t = jnp.concatenate([w_in[:, kr1:qs1], w_in[:, ks1:vs1]], axis=1).T
    return w_tok.astype(BF16), w_t.astype(BF16)


def _split_heads(w, n_first):
    w3 = w.reshape(w.shape[0], MLA_HEADS, -1)
    first = w3[:, :, :n_first].reshape(w.shape[0], -1)
    second = w3[:, :, n_first:].reshape(w.shape[0], -1)
    return first.astype(BF16), second.astype(BF16)


def kernel(x, c, positions, w_ada, b_ada, norm_mix, w_in, q_norm, w_uq, kv_norm, w_ukv, rel_bias, sink,
           w_o_mla, w_o_swa, w_out, norm_mlp, w_ff1, w_ff2, norm_final):
    B, S, D = x.shape
    assert w_ada.shape[0] == 1, "single-layer block"
    assert D == D_MODEL and S % (4 * BLOCK) == 0 and S >= BLOCK + 2 * WINDOW

    mod = _ada_mod(c, w_ada[0], b_ada[0])
    bias_tab = _swa_bias(rel_bias, sink[0])

    tm = 512
    w_k, w_v = _split_heads(w_ukv[0], MLA_NOPE)
    w_in_tok, w_in_t = _pack_w_in(w_in[0])
    qt_mla, k_mla, vt_mla, qs_t, ks, vs_t, gates = _in_proj(
        x, norm_mix[0], mod, positions, w_in_tok, w_in_t, q_norm[0], w_uq[0].T.astype(BF16),
        kv_norm[0], w_k, w_v.T, tm=tm)

    ot_mla = _mla_attn(qt_mla, k_mla, vt_mla, heads_per_step=2, tk=1024, tr=256, lookahead=1)
    ot_swa = _swa_attn(qs_t, ks, vs_t, bias_tab, sub_blocks=8, unit_heads=2, lookahead=2)

    return _out_mlp(x, ot_mla, ot_swa, gates, mod, norm_mlp[0], norm_final,
                    w_o_mla[0].astype(BF16), w_o_swa[0].astype(BF16), w_out[0].astype(BF16),
                    w_ff1[0].astype(BF16), w_ff2[0].astype(BF16), tm=tm, tf=1024, n_groups=2)
```

```python
import functools
import math

import jax
import jax.numpy as jnp
from jax import lax
from jax.experimental import pallas as pl
from jax.experimental.pallas import tpu as pltpu

F32 = jnp.float32
BF16 = jnp.bfloat16

D_MODEL = 1024
MLA_HEADS = 8
MLA_NOPE = 128
MLA_ROPE = 64
MLA_V = 128
Q_LORA = 384
KV_LORA = 256
ROPE_THETA = 10000.0
SWA_HEADS = 8
SWA_KV_HEADS = 2
SWA_GROUP = SWA_HEADS // SWA_KV_HEADS
SWA_HEAD_DIM = 128
WINDOW = 128
BLOCK = 128
N_BUCKETS = 32
N_MOD = 6
EPS = 1e-6
NEG_INF = -1e30

ROPE_HALF = MLA_ROPE // 2
LANES = 128
BF16_ROWS = 16
MLA_QK_PAD = 256
MLA_SHIFT_COL = MLA_NOPE + MLA_ROPE
MLA_VT_ROWS = MLA_V + 16
SWA_VT_ROWS = SWA_HEAD_DIM + 16
VMEM_LIMIT = 56 * 1024 * 1024

LOG2_E = math.log2(math.e)
MLA_Q_SCALE = (MLA_NOPE + MLA_ROPE) ** -0.5 * LOG2_E
SWA_Q_SCALE = SWA_HEAD_DIM ** -0.5 * LOG2_E

GATE_CHUNK = 2 * D_MODEL // 4

T5_LARGE_THRESHOLDS = (12, 16, 23, 32, 46, 64, 91)

NT_DIMS = (((1,), (1,)), ((), ()))
TN_DIMS = (((0,), (0,)), ((), ()))


def _resident(shape):
    nd = len(shape)
    return pl.BlockSpec(shape, lambda *_: (0,) * nd, pipeline_mode=pl.Buffered(1))


def _params(semantics):
    return pltpu.CompilerParams(dimension_semantics=semantics, vmem_limit_bytes=VMEM_LIMIT)


def _rms(x):
    return x * lax.rsqrt(jnp.mean(x * x, axis=-1, keepdims=True) + EPS)


MOD_SH1, MOD_SC1, MOD_G1, MOD_SH2, MOD_SC2, MOD_G2 = range(N_MOD)


def _mod_row(mod_ref, chunk):
    return mod_ref[pl.ds(pl.program_id(0), 1), chunk * D_MODEL:(chunk + 1) * D_MODEL]


def _ada_kernel(c_ref, w_ref, b_ref, o_ref):
    c = c_ref[...]
    c_act = c * jax.nn.sigmoid(c)
    o_ref[...] = jnp.dot(c_act, w_ref[...], preferred_element_type=F32,
                         precision=lax.Precision.HIGHEST) + b_ref[...]


def _ada_mod(c, w_ada, b_ada):
    B, D = c.shape
    N = w_ada.shape[1]
    tn = 1024
    return pl.pallas_call(
        _ada_kernel,
        out_shape=jax.ShapeDtypeStruct((B, N), F32),
        grid=(N // tn,),
        in_specs=[pl.BlockSpec((B, D), lambda j: (0, 0)),
                  pl.BlockSpec((D, tn), lambda j: (0, j)),
                  pl.BlockSpec((1, tn), lambda j: (0, j))],
        out_specs=pl.BlockSpec((B, tn), lambda j: (0, j)),
        compiler_params=_params(("arbitrary",)),
        name="ada_mod",
    )(c, w_ada, b_ada.reshape(1, N))


def _swa_bias_kernel(tab_ref, sink_ref, o_ref):
    h = pl.program_id(0)
    half = N_BUCKETS // 2
    max_exact = half // 2
    kj = lax.broadcasted_iota(jnp.int32, (BLOCK, LANES), 0)
    qi = lax.broadcasted_iota(jnp.int32, (BLOCK, LANES), 1)
    for cb in range(5):
        rel = kj - qi + (cb * LANES - 2 * WINDOW)
        n = jnp.abs(rel)
        large = jnp.full_like(n, max_exact)
        for t in T5_LARGE_THRESHOLDS:
            large = large + jnp.where(n >= t, 1, 0)
        bucket = jnp.where(rel > 0, half, 0) + jnp.where(n < max_exact, n, large)
        bias = jnp.zeros((BLOCK, LANES), F32)
        for b in range(N_BUCKETS):
            bias = jnp.where(bucket == b, tab_ref[b, h], bias)
        o_ref[0, cb] = jnp.where(n <= WINDOW, (bias - sink_ref[0, h]) * LOG2_E, NEG_INF)


def _swa_bias(rel_bias, sink):
    return pl.pallas_call(
        _swa_bias_kernel,
        out_shape=jax.ShapeDtypeStruct((SWA_HEADS, 5, BLOCK, LANES), F32),
        grid=(SWA_HEADS,),
        in_specs=[pl.BlockSpec(memory_space=pltpu.SMEM), pl.BlockSpec(memory_space=pltpu.SMEM)],
        out_specs=pl.BlockSpec((1, 5, BLOCK, LANES), lambda h: (h, 0, 0, 0)),
        compiler_params=_params(("arbitrary",)),
        name="swa_bias",
    )(rel_bias, sink.reshape(1, SWA_HEADS))


def _rope_rows(x, tab):
    x1, x2 = x[:ROPE_HALF], x[ROPE_HALF:]
    cos, sin = tab[:ROPE_HALF], tab[ROPE_HALF:]
    return jnp.concatenate([x1 * cos - x2 * sin, x2 * cos + x1 * sin], axis=0)


def _in_proj_kernel(x_ref, nm_ref, mod_ref, pos_ref, inv_ref, wlat_ref, wks_ref, wg_ref, wqvkt_ref, qn_ref, wuqt_ref, kvn_ref,
                    wk_ref, wvt_ref, qt_ref, km_ref, vt_ref, qst_ref, ks_ref, vst_ref, g_ref):
    tm = x_ref.shape[1]
    n_q = SWA_HEADS * SWA_HEAD_DIM
    rope_pad = jnp.zeros((MLA_QK_PAD - MLA_NOPE - MLA_ROPE, tm), F32)
    one_row = jnp.where(lax.broadcasted_iota(jnp.int32, rope_pad.shape, 0) == 0, 1.0, 0.0)
    mla_ones_rows = jnp.where(lax.broadcasted_iota(jnp.int32, (MLA_VT_ROWS - MLA_V, tm), 0) == 0,
                              1.0, 0.0).astype(BF16)
    swa_ones_rows = jnp.where(lax.broadcasted_iota(jnp.int32, (SWA_VT_ROWS - SWA_HEAD_DIM, BLOCK), 0) == 0,
                              1.0, 0.0).astype(BF16)

    h = (_rms(x_ref[0]) * nm_ref[...] * (1.0 + _mod_row(mod_ref, MOD_SC1)) + _mod_row(mod_ref, MOD_SH1)).astype(BF16)
    ang = inv_ref[...] * pos_ref[0].astype(F32)
    tab = jnp.concatenate([jnp.cos(ang), jnp.sin(ang)], axis=0)

    def gate(j):
        g = jnp.dot(h, wg_ref[:, j * GATE_CHUNK:(j + 1) * GATE_CHUNK], preferred_element_type=F32)
        g_ref[0, :, j * GATE_CHUNK:(j + 1) * GATE_CHUNK] = (0.5 * jnp.tanh(0.5 * g) + 0.5).astype(BF16)

    lat = jnp.dot(h, wlat_ref[...], preferred_element_type=F32)
    qvk_t = lax.dot_general(wqvkt_ref[...], h, NT_DIMS, preferred_element_type=F32)
    qst_ref[0] = (qvk_t[:n_q] * SWA_Q_SCALE).astype(BF16)
    for n in range(SWA_KV_HEADS):
        vs_t = qvk_t[n_q + n * SWA_HEAD_DIM:n_q + (n + 1) * SWA_HEAD_DIM].astype(BF16)
        for j in range(tm // BLOCK):
            vst_ref[0, j, n, :SWA_HEAD_DIM, :] = vs_t[:, j * BLOCK:(j + 1) * BLOCK]
            vst_ref[0, j, n, SWA_HEAD_DIM:, :] = swa_ones_rows
    gate(0)
    gate(1)
    cq = lat[:, :Q_LORA]
    ckv = lat[:, Q_LORA:]
    kr_t = qvk_t[n_q + SWA_KV_HEADS * SWA_HEAD_DIM:]
    k_rope = jnp.concatenate([_rope_rows(kr_t, tab), one_row], axis=0).T.astype(BF16)

    cqn = (_rms(cq) * qn_ref[...]).astype(BF16)
    q_t = lax.dot_general(wuqt_ref[...], cqn, NT_DIMS, preferred_element_type=F32)
    for hh in range(MLA_HEADS):
        q0 = hh * MLA_QK_PAD
        qh = q_t[hh * (MLA_NOPE + MLA_ROPE):(hh + 1) * (MLA_NOPE + MLA_ROPE)]
        qt_ref[0, 0, q0:q0 + MLA_NOPE, :] = (qh[:MLA_NOPE] * MLA_Q_SCALE).astype(BF16)
        q_rope = jnp.concatenate([_rope_rows(qh[MLA_NOPE:], tab) * MLA_Q_SCALE, rope_pad], axis=0)
        qt_ref[0, 0, q0 + MLA_NOPE:q0 + MLA_QK_PAD, :] = q_rope.astype(BF16)

    ckvn = (_rms(ckv) * kvn_ref[...]).astype(BF16)
    kn = jnp.dot(ckvn, wk_ref[...], preferred_element_type=F32)
    for hh in range(MLA_HEADS):
        c0 = hh * MLA_QK_PAD
        km_ref[0, :, c0:c0 + MLA_NOPE] = kn[:, hh * MLA_NOPE:(hh + 1) * MLA_NOPE].astype(BF16)
        km_ref[0, :, c0 + MLA_NOPE:c0 + MLA_QK_PAD] = k_rope
    vt = lax.dot_general(wvt_ref[...], ckvn, NT_DIMS, preferred_element_type=F32)
    for hh in range(MLA_HEADS):
        v0 = hh * MLA_VT_ROWS
        vt_ref[0, 0, v0:v0 + MLA_V, :] = vt[hh * MLA_V:(hh + 1) * MLA_V].astype(BF16)
        vt_ref[0, 0, v0 + MLA_V:v0 + MLA_VT_ROWS, :] = mla_ones_rows

    gate(2)
    gate(3)
    ks_ref[0] = jnp.dot(h, wks_ref[...], preferred_element_type=F32).astype(BF16)


def _in_proj(x, norm_mix, mod, positions, w_in_parts, q_norm, w_uq_t, kv_norm, w_k, w_v_t, tm):
    B, S, D = x.shape
    inv_freq = ROPE_THETA ** (-jnp.arange(0, MLA_ROPE, 2, dtype=F32) / MLA_ROPE)
    kvw = SWA_KV_HEADS * SWA_HEAD_DIM
    tok = lambda w: pl.BlockSpec((1, tm, w), lambda b, i: (b, i, 0))
    tok_t = lambda r: pl.BlockSpec((1, r, tm), lambda b, i: (b, 0, i))
    tile_t = lambda r: pl.BlockSpec((1, 1, r, tm), lambda b, i: (b, i, 0, 0))
    sds = lambda *shape: jax.ShapeDtypeStruct(shape, BF16)
    return pl.pallas_call(
        _in_proj_kernel,
        out_shape=[sds(B, S // tm, MLA_HEADS * MLA_QK_PAD, tm), sds(B, S, MLA_HEADS * MLA_QK_PAD),
                   sds(B, S // tm, MLA_HEADS * MLA_VT_ROWS, tm), sds(B, SWA_HEADS * SWA_HEAD_DIM, S),
                   sds(B, S, kvw), sds(B, S // BLOCK, SWA_KV_HEADS, SWA_VT_ROWS, BLOCK), sds(B, S, 2 * D)],
        grid=(B, S // tm),
        in_specs=[tok(D), _resident((1, D)), _resident(mod.shape), tok_t(1), _resident((ROPE_HALF, 1)),
                  *[_resident(w.shape) for w in w_in_parts], _resident((1, Q_LORA)),
                  _resident(w_uq_t.shape), _resident((1, KV_LORA)), _resident(w_k.shape),
                  _resident(w_v_t.shape)],
        out_specs=[tile_t(MLA_HEADS * MLA_QK_PAD), tok(MLA_HEADS * MLA_QK_PAD), tile_t(MLA_HEADS * MLA_VT_ROWS),
                   tok_t(SWA_HEADS * SWA_HEAD_DIM), tok(kvw),
                   pl.BlockSpec((1, tm // BLOCK, SWA_KV_HEADS, SWA_VT_ROWS, BLOCK), lambda b, i: (b, i, 0, 0, 0)),
                   tok(2 * D)],
        compiler_params=_params(("arbitrary", "arbitrary")),
        name="in_proj",
    )(x, norm_mix.reshape(1, D), mod, positions.reshape(B, 1, S), inv_freq.reshape(ROPE_HALF, 1),
      *w_in_parts, q_norm.reshape(1, Q_LORA), w_uq_t, kv_norm.reshape(1, KV_LORA), w_k, w_v_t)


def _mla_kernel(qt_ref, k_ref, vt_ref, o_ref, *, tk, tr, lookahead):
    n_tiles, _, tq = qt_ref.shape[1:]
    S = k_ref.shape[1]
    n_heads = k_ref.shape[2] // MLA_QK_PAD
    n_chunks = S // tk
    tiles_per_chunk = tk // tq
    g0 = MLA_SHIFT_COL // BF16_ROWS * BF16_ROWS
    row = lax.broadcasted_iota(jnp.int32, (BF16_ROWS, tq), 0)
    qk = lambda hd: slice(hd * MLA_QK_PAD, (hd + 1) * MLA_QK_PAD)

    def shifted_q(hd, i):
        qt = qt_ref[0, i, qk(hd), :]
        s = jnp.dot(k_ref[0, :tr, qk(hd)], qt, preferred_element_type=F32)
        ref = jnp.max(s, axis=0, keepdims=True)
        grp = jnp.where(row == MLA_SHIFT_COL - g0, -ref, qt[g0:g0 + BF16_ROWS].astype(F32)).astype(BF16)
        return jnp.concatenate([qt[:g0], grp, qt[g0 + BF16_ROWS:]], axis=0)

    def probs(item, qts):
        hd, i, c = item
        s = jnp.dot(k_ref[0, c * tk:(c + 1) * tk, qk(hd)], qts[hd, i], preferred_element_type=F32)
        return jnp.exp2(s).astype(BF16)

    def values_t(hd, c):
        rows = slice(hd * MLA_VT_ROWS, (hd + 1) * MLA_VT_ROWS)
        return jnp.concatenate([vt_ref[0, c * tiles_per_chunk + j, rows, :] for j in range(tiles_per_chunk)], axis=1)

    tiles = [(hd, i) for hd in range(n_heads) for i in range(n_tiles)]
    items = [(hd, i, c) for hd, i in tiles for c in range(n_chunks)]
    qts = {tiles[0]: shifted_q(*tiles[0])}
    ahead = [probs(item, qts) for item in items[:lookahead]]
    o_sum = jnp.zeros((MLA_V, tq), F32)
    for idx, (hd, i, c) in enumerate(items):
        t = idx // n_chunks
        if c == 0 and t + 1 < len(tiles):
            qts[tiles[t + 1]] = shifted_q(*tiles[t + 1])
        p = ahead.pop(0)
        if idx + lookahead < len(items):
            ahead.append(probs(items[idx + lookahead], qts))
        pv = jnp.dot(values_t(hd, c), p, preferred_element_type=F32)
        acc = pv if c == 0 else acc + pv
        if c == n_chunks - 1:
            o = acc[:MLA_V] / acc[MLA_V:MLA_V + 1]
            o_ref[0, i, hd * MLA_V:(hd + 1) * MLA_V, :] = o.astype(BF16)
            o_sum = o_sum + o
            qts.pop((hd, i))
    bad = jnp.max(jnp.where(jnp.isfinite(o_sum), 0.0, 1.0))

    @pl.when(bad > 0.0)
    def _recompute_with_running_max():
        for hd in range(n_heads):
            def tile_body(i, carry, hd=hd):
                qt = qt_ref[0, i, qk(hd), :]

                def key_tile_body(j, state):
                    m, l, acc = state
                    k = k_ref[0, pl.ds(pl.multiple_of(j * tq, tq), tq), qk(hd)]
                    s = jnp.dot(k, qt, preferred_element_type=F32)
                    m_new = jnp.maximum(m, jnp.max(s, axis=0, keepdims=True))
                    p = jnp.exp2(s - m_new)
                    alpha = jnp.exp2(m - m_new)
                    l = alpha * l + jnp.sum(p, axis=0, keepdims=True)
                    vt = vt_ref[0, j, hd * MLA_VT_ROWS:hd * MLA_VT_ROWS + MLA_V, :]
                    acc = alpha * acc + jnp.dot(vt, p.astype(BF16), preferred_element_type=F32)
                    return m_new, l, acc

                init = (jnp.full((1, tq), -jnp.inf, F32), jnp.zeros((1, tq), F32), jnp.zeros((MLA_V, tq), F32))
                _, l, acc = lax.fori_loop(0, n_tiles, key_tile_body, init)
                o_ref[0, i, hd * MLA_V:(hd + 1) * MLA_V, :] = (acc / l).astype(BF16)
                return carry

            lax.fori_loop(0, n_tiles, tile_body, 0)


def _mla_attn(qt_mla, k_mla, vt_mla, heads_per_step, tk, tr, lookahead):
    B, n_tiles, _, tq = qt_mla.shape
    S = k_mla.shape[1]
    hps = heads_per_step
    return pl.pallas_call(
        functools.partial(_mla_kernel, tk=tk, tr=tr, lookahead=lookahead),
        out_shape=jax.ShapeDtypeStruct((B, n_tiles, MLA_HEADS * MLA_V, tq), BF16),
        grid=(B, MLA_HEADS // hps),
        in_specs=[pl.BlockSpec((1, n_tiles, hps * MLA_QK_PAD, tq), lambda b, h: (b, 0, h, 0)),
                  pl.BlockSpec((1, S, hps * MLA_QK_PAD), lambda b, h: (b, 0, h)),
                  pl.BlockSpec((1, n_tiles, hps * MLA_VT_ROWS, tq), lambda b, h: (b, 0, h, 0))],
        out_specs=pl.BlockSpec((1, n_tiles, hps * MLA_V, tq), lambda b, h: (b, 0, h, 0)),
        compiler_params=_params(("arbitrary", "arbitrary")),
        name="mla_attn",
    )(qt_mla, k_mla, vt_mla)


def _swa_kernel(qt_ref, k_ref, vt_ref, t_ref, o_ref, *, sub_blocks, unit_heads, lookahead):
    S = k_ref.shape[1]
    span = BLOCK + 2 * WINDOW
    n_win = span // BLOCK
    step = pl.program_id(1)
    units = [(sb, hd0) for sb in range(sub_blocks) for hd0 in range(0, SWA_HEADS, unit_heads)]

    def window(sb):
        q0 = (step * sub_blocks + sb) * BLOCK
        start = pl.multiple_of(jnp.clip(q0 - WINDOW, 0, S - span), BLOCK)
        cb0 = jnp.where(q0 == 0, 2, jnp.where(q0 == S - BLOCK, 0, 1))
        return start, cb0

    def scores(sb, hd0):
        start, cb0 = window(sb)
        n = hd0 // SWA_GROUP
        heads = range(hd0, hd0 + unit_heads)
        qt = jnp.concatenate(
            [qt_ref[0, hd * SWA_HEAD_DIM:(hd + 1) * SWA_HEAD_DIM, sb * BLOCK:(sb + 1) * BLOCK] for hd in heads],
            axis=1)
        kw = k_ref[0, pl.ds(start, span), n * SWA_HEAD_DIM:(n + 1) * SWA_HEAD_DIM]
        bias = jnp.concatenate(
            [jnp.concatenate([t_ref[hd, cb0 + c] for c in range(n_win)], axis=0) for hd in heads], axis=1)
        return jnp.dot(kw, qt, preferred_element_type=F32) + bias

    def values_t(sb, hd0):
        start, _ = window(sb)
        blk0 = start // BLOCK
        return jnp.concatenate([vt_ref[0, blk0 + c, hd0 // SWA_GROUP] for c in range(n_win)], axis=1)

    def store(sb, hd0, o):
        for g in range(unit_heads):
            hd = hd0 + g
            o_ref[0, hd * SWA_HEAD_DIM:(hd + 1) * SWA_HEAD_DIM, sb * BLOCK:(sb + 1) * BLOCK] = (
                o[:, g * BLOCK:(g + 1) * BLOCK].astype(BF16))

    o_sum = jnp.zeros((SWA_HEAD_DIM, unit_heads * BLOCK), F32)
    probs = lambda unit: jnp.exp2(scores(*unit)).astype(BF16)
    ahead = [probs(unit) for unit in units[:lookahead]]
    for u, unit in enumerate(units):
        p = ahead.pop(0)
        if u + lookahead < len(units):
            ahead.append(probs(units[u + lookahead]))
        ov = jnp.dot(values_t(*unit), p, preferred_element_type=F32)
        o = ov[:SWA_HEAD_DIM] / (ov[SWA_HEAD_DIM:SWA_HEAD_DIM + 1] + 1.0)
        store(*unit, o)
        o_sum = o_sum + o
    bad = jnp.max(jnp.where(jnp.isfinite(o_sum), 0.0, 1.0))

    @pl.when(bad > 0.0)
    def _recompute_with_row_max():
        for unit in units:
            s = scores(*unit)
            m = jnp.maximum(jnp.max(s, axis=0, keepdims=True), 0.0)
            p = jnp.exp2(s - m)
            l = jnp.sum(p, axis=0, keepdims=True) + jnp.exp2(-m)
            ov = jnp.dot(values_t(*unit), p.astype(BF16), preferred_element_type=F32)
            store(*unit, ov[:SWA_HEAD_DIM] / l)


def _swa_attn(qs_t, ks, vs_t, bias_tab, sub_blocks, unit_heads, lookahead):
    B, W, S = qs_t.shape
    kvw = ks.shape[2]
    tq = sub_blocks * BLOCK
    return pl.pallas_call(
        functools.partial(_swa_kernel, sub_blocks=sub_blocks, unit_heads=unit_heads, lookahead=lookahead),
        out_shape=jax.ShapeDtypeStruct((B, W, S), BF16),
        grid=(B, S // tq),
        in_specs=[pl.BlockSpec((1, W, tq), lambda b, i: (b, 0, i)),
                  pl.BlockSpec((1, S, kvw), lambda b, i: (b, 0, 0)),
                  pl.BlockSpec((1, S // BLOCK, SWA_KV_HEADS, SWA_VT_ROWS, BLOCK), lambda b, i: (b, 0, 0, 0, 0)),
                  _resident(bias_tab.shape)],
        out_specs=pl.BlockSpec((1, W, tq), lambda b, i: (b, 0, i)),
        compiler_params=_params(("arbitrary", "arbitrary")),
        name="swa_attn",
    )(qs_t, ks, vs_t, bias_tab)


def _out_mlp_kernel(x_ref, oat_ref, obt_ref, g_ref, mod_ref, nmlp_ref, nfin_ref,
                    woa_ref, wob_ref, wout_ref, w1_ref, w2_ref, o_ref, *, tf, n_groups):
    tm, D = x_ref.shape[1:]
    groups = [slice(r, r + tm // n_groups) for r in range(0, tm, tm // n_groups)]

    def attn_proj(rows):
        y_a = lax.dot_general(oat_ref[0, 0, :, rows], woa_ref[...], TN_DIMS, preferred_element_type=F32)
        y_b = lax.dot_general(obt_ref[0, :, rows], wob_ref[...], TN_DIMS, preferred_element_type=F32)
        return y_a, y_b

    def residual(rows, y):
        merged = g_ref[0, rows, :D].astype(F32) * y[0] + g_ref[0, rows, D:].astype(F32) * y[1]
        att = jnp.dot(merged.astype(BF16), wout_ref[...], preferred_element_type=F32)
        return x_ref[0, rows, :] + _mod_row(mod_ref, MOD_G1) * att

    def mlp(rows, x1):
        h = (_rms(x1) * nmlp_ref[...] * (1.0 + _mod_row(mod_ref, MOD_SC2)) + _mod_row(mod_ref, MOD_SH2)).astype(BF16)
        ff = jnp.zeros_like(x1)
        for c in range(w1_ref.shape[1] // tf):
            a = jnp.dot(h, w1_ref[:, c * tf:(c + 1) * tf], preferred_element_type=F32)
            a = jnp.square(jnp.maximum(a, 0.0)).astype(BF16)
            ff = ff + jnp.dot(a, w2_ref[c * tf:(c + 1) * tf, :], preferred_element_type=F32)
        x2 = x1 + _mod_row(mod_ref, MOD_G2) * ff
        o_ref[0, rows, :] = _rms(x2) * nfin_ref[...]

    ys = [attn_proj(rows) for rows in groups]
    x1s = [residual(rows, y) for rows, y in zip(groups, ys)]
    for rows, x1 in zip(groups, x1s):
        mlp(rows, x1)


def _out_mlp(x, o_mla, o_swa, gates, mod, norm_mlp, norm_final,
             w_o_mla, w_o_swa, w_out, w_ff1, w_ff2, tm, tf, n_groups):
    B, S, D = x.shape
    tok = lambda w: pl.BlockSpec((1, tm, w), lambda b, i: (b, i, 0))
    return pl.pallas_call(
        functools.partial(_out_mlp_kernel, tf=tf, n_groups=n_groups),
        out_shape=jax.ShapeDtypeStruct((B, S, D), F32),
        grid=(B, S // tm),
        in_specs=[tok(D), pl.BlockSpec((1, 1, D, tm), lambda b, i: (b, i, 0, 0)),
                  pl.BlockSpec((1, D, tm), lambda b, i: (b, 0, i)), tok(2 * D),
                  _resident(mod.shape), _resident((1, D)), _resident((1, D)),
                  _resident(w_o_mla.shape), _resident(w_o_swa.shape), _resident(w_out.shape),
                  _resident(w_ff1.shape), _resident(w_ff2.shape)],
        out_specs=tok(D),
        compiler_params=_params(("arbitrary", "arbitrary")),
        name="out_mlp",
    )(x, o_mla, o_swa, gates, mod, norm_mlp.reshape(1, D), norm_final.reshape(1, D),
      w_o_mla, w_o_swa, w_out, w_ff1, w_ff2)


def _split_w_in(w_in):
    kr0 = Q_LORA + KV_LORA
    kr1 = kr0 + MLA_ROPE
    qs1 = kr1 + SWA_HEADS * SWA_HEAD_DIM
    ks1 = qs1 + SWA_KV_HEADS * SWA_HEAD_DIM
    vs1 = ks1 + SWA_KV_HEADS * SWA_HEAD_DIM
    w_t = jnp.concatenate([w_in[:, kr1:qs1], w_in[:, ks1:vs1], w_in[:, kr0:kr1]], axis=1).T
    return [w.astype(BF16) for w in (w_in[:, :kr0], w_in[:, qs1:ks1], w_in[:, vs1:], w_t)]


def _split_heads(w, n_first):
    w3 = w.reshape(w.shape[0], MLA_HEADS, -1)
    first = w3[:, :, :n_first].reshape(w.shape[0], -1)
    second = w3[:, :, n_first:].reshape(w.shape[0], -1)
    return first.astype(BF16), second.astype(BF16)


def kernel(x, c, positions, w_ada, b_ada, norm_mix, w_in, q_norm, w_uq, kv_norm, w_ukv, rel_bias, sink,
           w_o_mla, w_o_swa, w_out, norm_mlp, w_ff1, w_ff2, norm_final):
    B, S, D = x.shape
    assert w_ada.shape[0] == 1, "single-layer block"
    assert D == D_MODEL and S % (4 * BLOCK) == 0 and S >= BLOCK + 2 * WINDOW

    mod = _ada_mod(c, w_ada[0], b_ada[0])
    bias_tab = _swa_bias(rel_bias, sink[0])

    tm = 512
    w_k, w_v = _split_heads(w_ukv[0], MLA_NOPE)
    qt_mla, k_mla, vt_mla, qs_t, ks, vs_t, gates = _in_proj(
        x, norm_mix[0], mod, positions, _split_w_in(w_in[0]), q_norm[0], w_uq[0].T.astype(BF16),
        kv_norm[0], w_k, w_v.T, tm=tm)

    ot_mla = _mla_attn(qt_mla, k_mla, vt_mla, heads_per_step=2, tk=1024, tr=256, lookahead=1)
    ot_swa = _swa_attn(qs_t, ks, vs_t, bias_tab, sub_blocks=8, unit_heads=2, lookahead=2)

    return _out_mlp(x, ot_mla, ot_swa, gates, mod, norm_mlp[0], norm_final,
                    w_o_mla[0].astype(BF16), w_o_swa[0].astype(BF16), w_out[0].astype(BF16),
                    w_ff1[0].astype(BF16), w_ff2[0].astype(BF16), tm=tm, tf=1024, n_groups=2)
```

```python
import functools
import math

import jax
import jax.numpy as jnp
from jax import lax
from jax.experimental import pallas as pl
from jax.experimental.pallas import tpu as pltpu

F32 = jnp.float32
BF16 = jnp.bfloat16

D_MODEL = 1024
MLA_HEADS = 8
MLA_NOPE = 128
MLA_ROPE = 64
MLA_V = 128
Q_LORA = 384
KV_LORA = 256
ROPE_THETA = 10000.0
SWA_HEADS = 8
SWA_KV_HEADS = 2
SWA_GROUP = SWA_HEADS // SWA_KV_HEADS
SWA_HEAD_DIM = 128
WINDOW = 128
BLOCK = 128
N_BUCKETS = 32
N_MOD = 6
EPS = 1e-6
NEG_INF = -1e30

ROPE_HALF = MLA_ROPE // 2
LANES = 128
BF16_ROWS = 16
MLA_QK_PAD = 256
MLA_SHIFT_COL = MLA_NOPE + MLA_ROPE
MLA_VT_ROWS = MLA_V + 16
SWA_VT_ROWS = SWA_HEAD_DIM + 16
MIB = 1024 * 1024
VMEM_MIB = {"ada_mod": 16, "swa_bias": 8, "in_proj": 44, "mla_attn": 40, "swa_attn": 28, "out_mlp": 50}

LOG2_E = math.log2(math.e)
MLA_Q_SCALE = (MLA_NOPE + MLA_ROPE) ** -0.5 * LOG2_E
SWA_Q_SCALE = SWA_HEAD_DIM ** -0.5 * LOG2_E

C_CQ = 0
C_CKV = C_CQ + Q_LORA
C_KR = C_CKV + KV_LORA
C_KS = C_KR + LANES
C_G = C_KS + SWA_KV_HEADS * SWA_HEAD_DIM
GATE_CHUNK = 2 * D_MODEL // 4

T5_LARGE_THRESHOLDS = (12, 16, 23, 32, 46, 64, 91)

NT_DIMS = (((1,), (1,)), ((), ()))
TN_DIMS = (((0,), (0,)), ((), ()))


def _resident(shape):
    nd = len(shape)
    return pl.BlockSpec(shape, lambda *_: (0,) * nd, pipeline_mode=pl.Buffered(1))


def _params(name, semantics):
    return dict(name=name, compiler_params=pltpu.CompilerParams(
        dimension_semantics=semantics, vmem_limit_bytes=VMEM_MIB[name] * MIB))


def _rms(x):
    return x * lax.rsqrt(jnp.mean(x * x, axis=-1, keepdims=True) + EPS)


MOD_SH1, MOD_SC1, MOD_G1, MOD_SH2, MOD_SC2, MOD_G2 = range(N_MOD)


def _mod_row(mod_ref, chunk):
    return mod_ref[pl.ds(pl.program_id(0), 1), chunk * D_MODEL:(chunk + 1) * D_MODEL]


def _ada_kernel(c_ref, w_ref, b_ref, o_ref):
    c = c_ref[...]
    c_act = c * jax.nn.sigmoid(c)
    o_ref[...] = jnp.dot(c_act, w_ref[...], preferred_element_type=F32,
                         precision=lax.Precision.HIGHEST) + b_ref[...]


def _ada_mod(c, w_ada, b_ada):
    B, D = c.shape
    N = w_ada.shape[1]
    tn = 1024
    return pl.pallas_call(
        _ada_kernel,
        out_shape=jax.ShapeDtypeStruct((B, N), F32),
        grid=(N // tn,),
        in_specs=[pl.BlockSpec((B, D), lambda j: (0, 0)),
                  pl.BlockSpec((D, tn), lambda j: (0, j)),
                  pl.BlockSpec((1, tn), lambda j: (0, j))],
        out_specs=pl.BlockSpec((B, tn), lambda j: (0, j)),
        **_params("ada_mod", ("arbitrary",)),
    )(c, w_ada, b_ada.reshape(1, N))


def _swa_bias_kernel(tab_ref, sink_ref, o_ref):
    h = pl.program_id(0)
    half = N_BUCKETS // 2
    max_exact = half // 2
    kj = lax.broadcasted_iota(jnp.int32, (BLOCK, LANES), 0)
    qi = lax.broadcasted_iota(jnp.int32, (BLOCK, LANES), 1)
    for cb in range(5):
        rel = kj - qi + (cb * LANES - 2 * WINDOW)
        n = jnp.abs(rel)
        large = jnp.full_like(n, max_exact)
        for t in T5_LARGE_THRESHOLDS:
            large = large + jnp.where(n >= t, 1, 0)
        bucket = jnp.where(rel > 0, half, 0) + jnp.where(n < max_exact, n, large)
        bias = jnp.zeros((BLOCK, LANES), F32)
        for b in range(N_BUCKETS):
            bias = jnp.where(bucket == b, tab_ref[b, h], bias)
        o_ref[0, cb] = jnp.where(n <= WINDOW, (bias - sink_ref[0, h]) * LOG2_E, NEG_INF)


def _swa_bias(rel_bias, sink):
    return pl.pallas_call(
        _swa_bias_kernel,
        out_shape=jax.ShapeDtypeStruct((SWA_HEADS, 5, BLOCK, LANES), F32),
        grid=(SWA_HEADS,),
        in_specs=[pl.BlockSpec(memory_space=pltpu.SMEM), pl.BlockSpec(memory_space=pltpu.SMEM)],
        out_specs=pl.BlockSpec((1, 5, BLOCK, LANES), lambda h: (h, 0, 0, 0)),
        **_params("swa_bias", ("arbitrary",)),
    )(rel_bias, sink.reshape(1, SWA_HEADS))


def _rope_rows(x, tab):
    x1, x2 = x[:ROPE_HALF], x[ROPE_HALF:]
    cos, sin = tab[:ROPE_HALF], tab[ROPE_HALF:]
    return jnp.concatenate([x1 * cos - x2 * sin, x2 * cos + x1 * sin], axis=0)


def _in_proj_kernel(x_ref, nm_ref, mod_ref, pos_ref, inv_ref, win_ref, wqvt_ref, qn_ref, wuqt_ref, kvn_ref,
                    wk_ref, wvt_ref, qt_ref, km_ref, vt_ref, qst_ref, ks_ref, vst_ref, g_ref):
    tm = x_ref.shape[1]
    n_q = SWA_HEADS * SWA_HEAD_DIM
    rope_pad = jnp.zeros((MLA_QK_PAD - MLA_NOPE - MLA_ROPE, tm), F32)
    one_row = jnp.where(lax.broadcasted_iota(jnp.int32, rope_pad.shape, 0) == 0, 1.0, 0.0)
    mla_ones_rows = jnp.where(lax.broadcasted_iota(jnp.int32, (MLA_VT_ROWS - MLA_V, tm), 0) == 0,
                              1.0, 0.0).astype(BF16)
    swa_ones_rows = jnp.where(lax.broadcasted_iota(jnp.int32, (SWA_VT_ROWS - SWA_HEAD_DIM, BLOCK), 0) == 0,
                              1.0, 0.0).astype(BF16)

    h = (_rms(x_ref[0]) * nm_ref[...] * (1.0 + _mod_row(mod_ref, MOD_SC1)) + _mod_row(mod_ref, MOD_SH1)).astype(BF16)
    ang = inv_ref[...] * pos_ref[0].astype(F32)
    tab = jnp.concatenate([jnp.cos(ang), jnp.sin(ang)], axis=0)

    def gate(j):
        g = jnp.dot(h, win_ref[:, C_G + j * GATE_CHUNK:C_G + (j + 1) * GATE_CHUNK], preferred_element_type=F32)
        g_ref[0, :, j * GATE_CHUNK:(j + 1) * GATE_CHUNK] = (0.5 * jnp.tanh(0.5 * g) + 0.5).astype(BF16)

    lat = jnp.dot(h, win_ref[:, C_CQ:C_KS], preferred_element_type=F32)
    gate(0)
    gate(1)
    cq = lat[:, C_CQ:C_CKV]
    ckv = lat[:, C_CKV:C_KR]
    kr_t = lat[:, C_KR:C_KS].T
    k_rope = jnp.concatenate([_rope_rows(kr_t[:MLA_ROPE], tab), one_row], axis=0).T.astype(BF16)

    cqn = (_rms(cq) * qn_ref[...]).astype(BF16)
    q_t = lax.dot_general(wuqt_ref[...], cqn, NT_DIMS, preferred_element_type=F32)
    for hh in range(MLA_HEADS):
        q0 = hh * MLA_QK_PAD
        qh = q_t[hh * (MLA_NOPE + MLA_ROPE):(hh + 1) * (MLA_NOPE + MLA_ROPE)]
        qt_ref[0, 0, q0:q0 + MLA_NOPE, :] = (qh[:MLA_NOPE] * MLA_Q_SCALE).astype(BF16)
        q_rope = jnp.concatenate([_rope_rows(qh[MLA_NOPE:], tab) * MLA_Q_SCALE, rope_pad], axis=0)
        qt_ref[0, 0, q0 + MLA_NOPE:q0 + MLA_QK_PAD, :] = q_rope.astype(BF16)

    ckvn = (_rms(ckv) * kvn_ref[...]).astype(BF16)
    kn = jnp.dot(ckvn, wk_ref[...], preferred_element_type=F32)
    for hh in range(MLA_HEADS):
        c0 = hh * MLA_QK_PAD
        km_ref[0, :, c0:c0 + MLA_NOPE] = kn[:, hh * MLA_NOPE:(hh + 1) * MLA_NOPE].astype(BF16)
        km_ref[0, :, c0 + MLA_NOPE:c0 + MLA_QK_PAD] = k_rope
    vt = lax.dot_general(wvt_ref[...], ckvn, NT_DIMS, preferred_element_type=F32)
    for hh in range(MLA_HEADS):
        v0 = hh * MLA_VT_ROWS
        vt_ref[0, 0, v0:v0 + MLA_V, :] = vt[hh * MLA_V:(hh + 1) * MLA_V].astype(BF16)
        vt_ref[0, 0, v0 + MLA_V:v0 + MLA_VT_ROWS, :] = mla_ones_rows

    qv_t = lax.dot_general(wqvt_ref[...], h, NT_DIMS, preferred_element_type=F32)
    qst_ref[0] = (qv_t[:n_q] * SWA_Q_SCALE).astype(BF16)
    for n in range(SWA_KV_HEADS):
        vs_t = qv_t[n_q + n * SWA_HEAD_DIM:n_q + (n + 1) * SWA_HEAD_DIM].astype(BF16)
        for j in range(tm // BLOCK):
            vst_ref[0, j, n, :SWA_HEAD_DIM, :] = vs_t[:, j * BLOCK:(j + 1) * BLOCK]
            vst_ref[0, j, n, SWA_HEAD_DIM:, :] = swa_ones_rows
    gate(2)
    gate(3)
    ks_ref[0] = jnp.dot(h, win_ref[:, C_KS:C_G], preferred_element_type=F32).astype(BF16)


def _in_proj(x, norm_mix, mod, positions, w_in_p, w_qv_t, q_norm, w_uq_t, kv_norm, w_k, w_v_t, tm):
    B, S, D = x.shape
    inv_freq = ROPE_THETA ** (-jnp.arange(0, MLA_ROPE, 2, dtype=F32) / MLA_ROPE)
    kvw = SWA_KV_HEADS * SWA_HEAD_DIM
    tok = lambda w: pl.BlockSpec((1, tm, w), lambda b, i: (b, i, 0))
    tok_t = lambda r: pl.BlockSpec((1, r, tm), lambda b, i: (b, 0, i))
    tile_t = lambda r: pl.BlockSpec((1, 1, r, tm), lambda b, i: (b, i, 0, 0))
    sds = lambda *shape: jax.ShapeDtypeStruct(shape, BF16)
    return pl.pallas_call(
        _in_proj_kernel,
        out_shape=[sds(B, S // tm, MLA_HEADS * MLA_QK_PAD, tm), sds(B, S, MLA_HEADS * MLA_QK_PAD),
                   sds(B, S // tm, MLA_HEADS * MLA_VT_ROWS, tm), sds(B, SWA_HEADS * SWA_HEAD_DIM, S),
                   sds(B, S, kvw), sds(B, S // BLOCK, SWA_KV_HEADS, SWA_VT_ROWS, BLOCK), sds(B, S, 2 * D)],
        grid=(B, S // tm),
        in_specs=[tok(D), _resident((1, D)), _resident(mod.shape), tok_t(1), _resident((ROPE_HALF, 1)),
                  _resident(w_in_p.shape), _resident(w_qv_t.shape), _resident((1, Q_LORA)),
                  _resident(w_uq_t.shape), _resident((1, KV_LORA)), _resident(w_k.shape),
                  _resident(w_v_t.shape)],
        out_specs=[tile_t(MLA_HEADS * MLA_QK_PAD), tok(MLA_HEADS * MLA_QK_PAD), tile_t(MLA_HEADS * MLA_VT_ROWS),
                   tok_t(SWA_HEADS * SWA_HEAD_DIM), tok(kvw),
                   pl.BlockSpec((1, tm // BLOCK, SWA_KV_HEADS, SWA_VT_ROWS, BLOCK), lambda b, i: (b, i, 0, 0, 0)),
                   tok(2 * D)],
        **_params("in_proj", ("arbitrary", "arbitrary")),
    )(x, norm_mix.reshape(1, D), mod, positions.reshape(B, 1, S), inv_freq.reshape(ROPE_HALF, 1),
      w_in_p, w_qv_t, q_norm.reshape(1, Q_LORA), w_uq_t, kv_norm.reshape(1, KV_LORA), w_k, w_v_t)


def _mla_kernel(qt_ref, k_ref, vt_ref, o_ref, *, tk, tr, lookahead):
    n_tiles, _, tq = qt_ref.shape[1:]
    S = k_ref.shape[1]
    n_heads = k_ref.shape[2] // MLA_QK_PAD
    n_chunks = S // tk
    tiles_per_chunk = tk // tq
    g0 = MLA_SHIFT_COL // BF16_ROWS * BF16_ROWS
    row = lax.broadcasted_iota(jnp.int32, (BF16_ROWS, tq), 0)
    qk = lambda hd: slice(hd * MLA_QK_PAD, (hd + 1) * MLA_QK_PAD)

    def shifted_q(hd, i):
        qt = qt_ref[0, i, qk(hd), :]
        s = jnp.dot(k_ref[0, :tr, qk(hd)], qt, preferred_element_type=F32)
        ref = jnp.max(s, axis=0, keepdims=True)
        grp = jnp.where(row == MLA_SHIFT_COL - g0, -ref, qt[g0:g0 + BF16_ROWS].astype(F32)).astype(BF16)
        return jnp.concatenate([qt[:g0], grp, qt[g0 + BF16_ROWS:]], axis=0)

    def probs(item, qts):
        hd, i, c = item
        s = jnp.dot(k_ref[0, c * tk:(c + 1) * tk, qk(hd)], qts[hd, i], preferred_element_type=F32)
        return jnp.exp2(s).astype(BF16)

    def values_t(hd, c):
        rows = slice(hd * MLA_VT_ROWS, (hd + 1) * MLA_VT_ROWS)
        return jnp.concatenate([vt_ref[0, c * tiles_per_chunk + j, rows, :] for j in range(tiles_per_chunk)], axis=1)

    tiles = [(hd, i) for hd in range(n_heads) for i in range(n_tiles)]
    items = [(hd, i, c) for hd, i in tiles for c in range(n_chunks)]
    qts = {tiles[0]: shifted_q(*tiles[0])}
    ahead = [probs(item, qts) for item in items[:lookahead]]
    o_sum = jnp.zeros((MLA_V, tq), F32)
    for idx, (hd, i, c) in enumerate(items):
        t = idx // n_chunks
        if c == 0 and t + 1 < len(tiles):
            qts[tiles[t + 1]] = shifted_q(*tiles[t + 1])
        p = ahead.pop(0)
        if idx + lookahead < len(items):
            ahead.append(probs(items[idx + lookahead], qts))
        pv = jnp.dot(values_t(hd, c), p, preferred_element_type=F32)
        acc = pv if c == 0 else acc + pv
        if c == n_chunks - 1:
            o = acc[:MLA_V] / acc[MLA_V:MLA_V + 1]
            o_ref[0, i, hd * MLA_V:(hd + 1) * MLA_V, :] = o.astype(BF16)
            o_sum = o_sum + o
            qts.pop((hd, i))
    bad = jnp.max(jnp.where(jnp.isfinite(o_sum), 0.0, 1.0))

    @pl.when(bad > 0.0)
    def _recompute_with_running_max():
        for hd in range(n_heads):
            def tile_body(i, carry, hd=hd):
                qt = qt_ref[0, i, qk(hd), :]

                def key_tile_body(j, state):
                    m, l, acc = state
                    k = k_ref[0, pl.ds(pl.multiple_of(j * tq, tq), tq), qk(hd)]
                    s = jnp.dot(k, qt, preferred_element_type=F32)
                    m_new = jnp.maximum(m, jnp.max(s, axis=0, keepdims=True))
                    p = jnp.exp2(s - m_new)
                    alpha = jnp.exp2(m - m_new)
                    l = alpha * l + jnp.sum(p, axis=0, keepdims=True)
                    vt = vt_ref[0, j, hd * MLA_VT_ROWS:hd * MLA_VT_ROWS + MLA_V, :]
                    acc = alpha * acc + jnp.dot(vt, p.astype(BF16), preferred_element_type=F32)
                    return m_new, l, acc

                init = (jnp.full((1, tq), -jnp.inf, F32), jnp.zeros((1, tq), F32), jnp.zeros((MLA_V, tq), F32))
                _, l, acc = lax.fori_loop(0, n_tiles, key_tile_body, init)
                o_ref[0, i, hd * MLA_V:(hd + 1) * MLA_V, :] = (acc / l).astype(BF16)
                return carry

            lax.fori_loop(0, n_tiles, tile_body, 0)


def _mla_attn(qt_mla, k_mla, vt_mla, heads_per_step, tk, tr, lookahead):
    B, n_tiles, _, tq = qt_mla.shape
    S = k_mla.shape[1]
    hps = heads_per_step
    return pl.pallas_call(
        functools.partial(_mla_kernel, tk=tk, tr=tr, lookahead=lookahead),
        out_shape=jax.ShapeDtypeStruct((B, n_tiles, MLA_HEADS * MLA_V, tq), BF16),
        grid=(B, MLA_HEADS // hps),
        in_specs=[pl.BlockSpec((1, n_tiles, hps * MLA_QK_PAD, tq), lambda b, h: (b, 0, h, 0)),
                  pl.BlockSpec((1, S, hps * MLA_QK_PAD), lambda b, h: (b, 0, h)),
                  pl.BlockSpec((1, n_tiles, hps * MLA_VT_ROWS, tq), lambda b, h: (b, 0, h, 0))],
        out_specs=pl.BlockSpec((1, n_tiles, hps * MLA_V, tq), lambda b, h: (b, 0, h, 0)),
        **_params("mla_attn", ("arbitrary", "arbitrary")),
    )(qt_mla, k_mla, vt_mla)


def _swa_kernel(qt_ref, k_ref, vt_ref, t_ref, o_ref, *, sub_blocks, unit_heads, lookahead):
    S = k_ref.shape[1]
    span = BLOCK + 2 * WINDOW
    n_win = span // BLOCK
    step = pl.program_id(1)
    units = [(sb, hd0) for sb in range(sub_blocks) for hd0 in range(0, SWA_HEADS, unit_heads)]

    def window(sb):
        q0 = (step * sub_blocks + sb) * BLOCK
        start = pl.multiple_of(jnp.clip(q0 - WINDOW, 0, S - span), BLOCK)
        cb0 = jnp.where(q0 == 0, 2, jnp.where(q0 == S - BLOCK, 0, 1))
        return start, cb0

    def scores(sb, hd0):
        start, cb0 = window(sb)
        n = hd0 // SWA_GROUP
        heads = range(hd0, hd0 + unit_heads)
        qt = jnp.concatenate(
            [qt_ref[0, hd * SWA_HEAD_DIM:(hd + 1) * SWA_HEAD_DIM, sb * BLOCK:(sb + 1) * BLOCK] for hd in heads],
            axis=1)
        kw = k_ref[0, pl.ds(start, span), n * SWA_HEAD_DIM:(n + 1) * SWA_HEAD_DIM]
        bias = jnp.concatenate(
            [jnp.concatenate([t_ref[hd, cb0 + c] for c in range(n_win)], axis=0) for hd in heads], axis=1)
        return jnp.dot(kw, qt, preferred_element_type=F32) + bias

    def values_t(sb, hd0):
        start, _ = window(sb)
        blk0 = start // BLOCK
        return jnp.concatenate([vt_ref[0, blk0 + c, hd0 // SWA_GROUP] for c in range(n_win)], axis=1)

    def store(sb, hd0, o):
        for g in range(unit_heads):
            hd = hd0 + g
            o_ref[0, hd * SWA_HEAD_DIM:(hd + 1) * SWA_HEAD_DIM, sb * BLOCK:(sb + 1) * BLOCK] = (
                o[:, g * BLOCK:(g + 1) * BLOCK].astype(BF16))

    o_sum = jnp.zeros((SWA_HEAD_DIM, unit_heads * BLOCK), F32)
    probs = lambda unit: jnp.exp2(scores(*unit)).astype(BF16)
    ahead = [probs(unit) for unit in units[:lookahead]]
    for u, unit in enumerate(units):
        p = ahead.pop(0)
        if u + lookahead < len(units):
            ahead.append(probs(units[u + lookahead]))
        ov = jnp.dot(values_t(*unit), p, preferred_element_type=F32)
        o = ov[:SWA_HEAD_DIM] / (ov[SWA_HEAD_DIM:SWA_HEAD_DIM + 1] + 1.0)
        store(*unit, o)
        o_sum = o_sum + o
    bad = jnp.max(jnp.where(jnp.isfinite(o_sum), 0.0, 1.0))

    @pl.when(bad > 0.0)
    def _recompute_with_row_max():
        for unit in units:
            s = scores(*unit)
            m = jnp.maximum(jnp.max(s, axis=0, keepdims=True), 0.0)
            p = jnp.exp2(s - m)
            l = jnp.sum(p, axis=0, keepdims=True) + jnp.exp2(-m)
            ov = jnp.dot(values_t(*unit), p.astype(BF16), preferred_element_type=F32)
            store(*unit, ov[:SWA_HEAD_DIM] / l)


def _swa_attn(qs_t, ks, vs_t, bias_tab, sub_blocks, unit_heads, lookahead):
    B, W, S = qs_t.shape
    kvw = ks.shape[2]
    tq = sub_blocks * BLOCK
    return pl.pallas_call(
        functools.partial(_swa_kernel, sub_blocks=sub_blocks, unit_heads=unit_heads, lookahead=lookahead),
        out_shape=jax.ShapeDtypeStruct((B, W, S), BF16),
        grid=(B, S // tq),
        in_specs=[pl.BlockSpec((1, W, tq), lambda b, i: (b, 0, i)),
                  pl.BlockSpec((1, S, kvw), lambda b, i: (b, 0, 0)),
                  pl.BlockSpec((1, S // BLOCK, SWA_KV_HEADS, SWA_VT_ROWS, BLOCK), lambda b, i: (b, 0, 0, 0, 0)),
                  _resident(bias_tab.shape)],
        out_specs=pl.BlockSpec((1, W, tq), lambda b, i: (b, 0, i)),
        **_params("swa_attn", ("arbitrary", "arbitrary")),
    )(qs_t, ks, vs_t, bias_tab)


def _out_mlp_kernel(x_ref, oat_ref, obt_ref, g_ref, mod_ref, nmlp_ref, nfin_ref,
                    woa_ref, wob_ref, wout_ref, w1_ref, w2_ref, o_ref, *, tf, n_groups):
    tm, D = x_ref.shape[1:]
    groups = [slice(r, r + tm // n_groups) for r in range(0, tm, tm // n_groups)]

    def attn_proj(rows):
        y_a = lax.dot_general(oat_ref[0, 0, :, rows], woa_ref[...], TN_DIMS, preferred_element_type=F32)
        y_b = lax.dot_general(obt_ref[0, :, rows], wob_ref[...], TN_DIMS, preferred_element_type=F32)
        return y_a, y_b

    def residual(rows, y):
        merged = g_ref[0, rows, :D].astype(F32) * y[0] + g_ref[0, rows, D:].astype(F32) * y[1]
        att = jnp.dot(merged.astype(BF16), wout_ref[...], preferred_element_type=F32)
        return x_ref[0, rows, :] + _mod_row(mod_ref, MOD_G1) * att

    def mlp(rows, x1):
        h = (_rms(x1) * nmlp_ref[...] * (1.0 + _mod_row(mod_ref, MOD_SC2)) + _mod_row(mod_ref, MOD_SH2)).astype(BF16)
        ff = jnp.zeros_like(x1)
        for c in range(w1_ref.shape[1] // tf):
            a = jnp.dot(h, w1_ref[:, c * tf:(c + 1) * tf], preferred_element_type=F32)
            a = jnp.square(jnp.maximum(a, 0.0)).astype(BF16)
            ff = ff + jnp.dot(a, w2_ref[c * tf:(c + 1) * tf, :], preferred_element_type=F32)
        x2 = x1 + _mod_row(mod_ref, MOD_G2) * ff
        o_ref[0, rows, :] = _rms(x2) * nfin_ref[...]

    ys = [attn_proj(rows) for rows in groups]
    x1s = [residual(rows, y) for rows, y in zip(groups, ys)]
    for rows, x1 in zip(groups, x1s):
        mlp(rows, x1)


def _out_mlp(x, o_mla, o_swa, gates, mod, norm_mlp, norm_final,
             w_o_mla, w_o_swa, w_out, w_ff1, w_ff2, tm, tf, n_groups):
    B, S, D = x.shape
    tok = lambda w: pl.BlockSpec((1, tm, w), lambda b, i: (b, i, 0))
    return pl.pallas_call(
        functools.partial(_out_mlp_kernel, tf=tf, n_groups=n_groups),
        out_shape=jax.ShapeDtypeStruct((B, S, D), F32),
        grid=(B, S // tm),
        in_specs=[tok(D), pl.BlockSpec((1, 1, D, tm), lambda b, i: (b, i, 0, 0)),
                  pl.BlockSpec((1, D, tm), lambda b, i: (b, 0, i)), tok(2 * D),
                  _resident(mod.shape), _resident((1, D)), _resident((1, D)),
                  _resident(w_o_mla.shape), _resident(w_o_swa.shape), _resident(w_out.shape),
                  _resident(w_ff1.shape), _resident(w_ff2.shape)],
        out_specs=tok(D),
        **_params("out_mlp", ("arbitrary", "arbitrary")),
    )(x, o_mla, o_swa, gates, mod, norm_mlp.reshape(1, D), norm_final.reshape(1, D),
      w_o_mla, w_o_swa, w_out, w_ff1, w_ff2)


def _pack_w_in(w_in):
    kr1 = Q_LORA + KV_LORA + MLA_ROPE
    qs1 = kr1 + SWA_HEADS * SWA_HEAD_DIM
    ks1 = qs1 + SWA_KV_HEADS * SWA_HEAD_DIM
    vs1 = ks1 + SWA_KV_HEADS * SWA_HEAD_DIM
    pad = jnp.zeros((w_in.shape[0], LANES - MLA_ROPE), w_in.dtype)
    w_tok = jnp.concatenate([w_in[:, :kr1], pad, w_in[:, qs1:ks1], w_in[:, vs1:]], axis=1)
    w_t = jnp.concatenate([w_in[:, kr1:qs1], w_in[:, ks1:vs1]], axis=1).T
    return w_tok.astype(BF16), w_t.astype(BF16)


def _split_heads(w, n_first):
    w3 = w.reshape(w.shape[0], MLA_HEADS, -1)
    first = w3[:, :, :n_first].reshape(w.shape[0], -1)
    second = w3[:, :, n_first:].reshape(w.shape[0], -1)
    return first.astype(BF16), second.astype(BF16)


def kernel(x, c, positions, w_ada, b_ada, norm_mix, w_in, q_norm, w_uq, kv_norm, w_ukv, rel_bias, sink,
           w_o_mla, w_o_swa, w_out, norm_mlp, w_ff1, w_ff2, norm_final):
    B, S, D = x.shape
    assert w_ada.shape[0] == 1, "single-layer block"
    assert D == D_MODEL and S % (4 * BLOCK) == 0 and S >= BLOCK + 2 * WINDOW

    mod = _ada_mod(c, w_ada[0], b_ada[0])
    bias_tab = _swa_bias(rel_bias, sink[0])

    tm = 512
    w_k, w_v = _split_heads(w_ukv[0], MLA_NOPE)
    w_in_tok, w_in_t = _pack_w_in(w_in[0])
    qt_mla, k_mla, vt_mla, qs_t, ks, vs_t, gates = _in_proj(
        x, norm_mix[0], mod, positions, w_in_tok, w_in_t, q_norm[0], w_uq[0].T.astype(BF16),
        kv_norm[0], w_k, w_v.T, tm=tm)

    ot_mla = _mla_attn(qt_mla, k_mla, vt_mla, heads_per_step=2, tk=1024, tr=256, lookahead=1)
    ot_swa = _swa_attn(qs_t, ks, vs_t, bias_tab, sub_blocks=8, unit_heads=2, lookahead=2)

    return _out_mlp(x, ot_mla, ot_swa, gates, mod, norm_mlp[0], norm_final,
                    w_o_mla[0].astype(BF16), w_o_swa[0].astype(BF16), w_out[0].astype(BF16),
                    w_ff1[0].astype(BF16), w_ff2[0].astype(BF16), tm=tm, tf=1024, n_groups=2)
```

```python
import functools
import math

import jax
import jax.numpy as jnp
from jax import lax
from jax.experimental import pallas as pl
from jax.experimental.pallas import tpu as pltpu

F32 = jnp.float32
BF16 = jnp.bfloat16

D_MODEL = 1024
MLA_HEADS = 8
MLA_NOPE = 128
MLA_ROPE = 64
MLA_V = 128
Q_LORA = 384
KV_LORA = 256
ROPE_THETA = 10000.0
SWA_HEADS = 8
SWA_KV_HEADS = 2
SWA_GROUP = SWA_HEADS // SWA_KV_HEADS
SWA_HEAD_DIM = 128
WINDOW = 128
BLOCK = 128
N_BUCKETS = 32
N_MOD = 6
EPS = 1e-6
NEG_INF = -1e30

ROPE_HALF = MLA_ROPE // 2
LANES = 128
BF16_ROWS = 16
MLA_QK_PAD = 256
MLA_SHIFT_COL = MLA_NOPE + MLA_ROPE
MLA_VT_ROWS = MLA_V + 16
SWA_VT_ROWS = SWA_HEAD_DIM + 16
VMEM_LIMIT = 56 * 1024 * 1024

LOG2_E = math.log2(math.e)
MLA_Q_SCALE = (MLA_NOPE + MLA_ROPE) ** -0.5 * LOG2_E
SWA_Q_SCALE = SWA_HEAD_DIM ** -0.5 * LOG2_E

C_CQ = 0
C_CKV = C_CQ + Q_LORA
C_KR = C_CKV + KV_LORA
C_KS = C_KR + LANES
C_END = C_KS + SWA_KV_HEADS * SWA_HEAD_DIM

T5_LARGE_THRESHOLDS = (12, 16, 23, 32, 46, 64, 91)

NT_DIMS = (((1,), (1,)), ((), ()))
TN_DIMS = (((0,), (0,)), ((), ()))


def _resident(shape):
    nd = len(shape)
    return pl.BlockSpec(shape, lambda *_: (0,) * nd, pipeline_mode=pl.Buffered(1))


def _params(semantics):
    return pltpu.CompilerParams(dimension_semantics=semantics, vmem_limit_bytes=VMEM_LIMIT)


def _rms(x):
    return x * lax.rsqrt(jnp.mean(x * x, axis=-1, keepdims=True) + EPS)


MOD_SH1, MOD_SC1, MOD_G1, MOD_SH2, MOD_SC2, MOD_G2 = range(N_MOD)


def _mod_row(mod_ref, chunk):
    return mod_ref[pl.ds(pl.program_id(0), 1), chunk * D_MODEL:(chunk + 1) * D_MODEL]


def _ada_kernel(c_ref, w_ref, b_ref, o_ref):
    c = c_ref[...]
    c_act = c * jax.nn.sigmoid(c)
    o_ref[...] = jnp.dot(c_act, w_ref[...], preferred_element_type=F32,
                         precision=lax.Precision.HIGHEST) + b_ref[...]


def _ada_mod(c, w_ada, b_ada):
    B, D = c.shape
    N = w_ada.shape[1]
    tn = 1024
    return pl.pallas_call(
        _ada_kernel,
        out_shape=jax.ShapeDtypeStruct((B, N), F32),
        grid=(N // tn,),
        in_specs=[pl.BlockSpec((B, D), lambda j: (0, 0)),
                  pl.BlockSpec((D, tn), lambda j: (0, j)),
                  pl.BlockSpec((1, tn), lambda j: (0, j))],
        out_specs=pl.BlockSpec((B, tn), lambda j: (0, j)),
        compiler_params=_params(("arbitrary",)),
        name="ada_mod",
    )(c, w_ada, b_ada.reshape(1, N))


def _swa_bias_kernel(tab_ref, sink_ref, o_ref):
    h = pl.program_id(0)
    half = N_BUCKETS // 2
    max_exact = half // 2
    kj = lax.broadcasted_iota(jnp.int32, (BLOCK, LANES), 0)
    qi = lax.broadcasted_iota(jnp.int32, (BLOCK, LANES), 1)
    for cb in range(5):
        rel = kj - qi + (cb * LANES - 2 * WINDOW)
        n = jnp.abs(rel)
        large = jnp.full_like(n, max_exact)
        for t in T5_LARGE_THRESHOLDS:
            large = large + jnp.where(n >= t, 1, 0)
        bucket = jnp.where(rel > 0, half, 0) + jnp.where(n < max_exact, n, large)
        bias = jnp.zeros((BLOCK, LANES), F32)
        for b in range(N_BUCKETS):
            bias = jnp.where(bucket == b, tab_ref[b, h], bias)
        o_ref[0, cb] = jnp.where(n <= WINDOW, (bias - sink_ref[0, h]) * LOG2_E, NEG_INF)


def _swa_bias(rel_bias, sink):
    return pl.pallas_call(
        _swa_bias_kernel,
        out_shape=jax.ShapeDtypeStruct((SWA_HEADS, 5, BLOCK, LANES), F32),
        grid=(SWA_HEADS,),
        in_specs=[pl.BlockSpec(memory_space=pltpu.SMEM), pl.BlockSpec(memory_space=pltpu.SMEM)],
        out_specs=pl.BlockSpec((1, 5, BLOCK, LANES), lambda h: (h, 0, 0, 0)),
        compiler_params=_params(("arbitrary",)),
        name="swa_bias",
    )(rel_bias, sink.reshape(1, SWA_HEADS))


def _rope_rows(x, tab):
    x1, x2 = x[:ROPE_HALF], x[ROPE_HALF:]
    cos, sin = tab[:ROPE_HALF], tab[ROPE_HALF:]
    return jnp.concatenate([x1 * cos - x2 * sin, x2 * cos + x1 * sin], axis=0)


def _in_proj_kernel(x_ref, nm_ref, mod_ref, pos_ref, inv_ref, win_ref, wqvt_ref, qn_ref, wuqt_ref, kvn_ref,
                    wk_ref, wvt_ref, qt_ref, km_ref, vt_ref, qst_ref, ks_ref, vst_ref, *, tile):
    n_q = SWA_HEADS * SWA_HEAD_DIM
    rope_pad = jnp.zeros((MLA_QK_PAD - MLA_NOPE - MLA_ROPE, tile), F32)
    one_row = jnp.where(lax.broadcasted_iota(jnp.int32, rope_pad.shape, 0) == 0, 1.0, 0.0)
    mla_ones_rows = jnp.where(lax.broadcasted_iota(jnp.int32, (MLA_VT_ROWS - MLA_V, tile), 0) == 0,
                              1.0, 0.0).astype(BF16)
    swa_ones_rows = jnp.where(lax.broadcasted_iota(jnp.int32, (SWA_VT_ROWS - SWA_HEAD_DIM, BLOCK), 0) == 0,
                              1.0, 0.0).astype(BF16)

    def token_tile(t):
        rows = slice(t * tile, (t + 1) * tile)
        h = (_rms(x_ref[0, rows, :]) * nm_ref[...] * (1.0 + _mod_row(mod_ref, MOD_SC1))
             + _mod_row(mod_ref, MOD_SH1)).astype(BF16)
        ang = inv_ref[...] * pos_ref[0, :, rows].astype(F32)
        tab = jnp.concatenate([jnp.cos(ang), jnp.sin(ang)], axis=0)

        lat = jnp.dot(h, win_ref[:, C_CQ:C_KS], preferred_element_type=F32)
        qv_t = lax.dot_general(wqvt_ref[...], h, NT_DIMS, preferred_element_type=F32)
        qst_ref[0, :, rows] = (qv_t[:n_q] * SWA_Q_SCALE).astype(BF16)
        for n in range(SWA_KV_HEADS):
            vs_t = qv_t[n_q + n * SWA_HEAD_DIM:n_q + (n + 1) * SWA_HEAD_DIM].astype(BF16)
            for j in range(tile // BLOCK):
                blk = t * (tile // BLOCK) + j
                vst_ref[0, blk, n, :SWA_HEAD_DIM, :] = vs_t[:, j * BLOCK:(j + 1) * BLOCK]
                vst_ref[0, blk, n, SWA_HEAD_DIM:, :] = swa_ones_rows
        ks_ref[0, rows, :] = jnp.dot(h, win_ref[:, C_KS:C_END], preferred_element_type=F32).astype(BF16)

        cq = lat[:, C_CQ:C_CKV]
        ckv = lat[:, C_CKV:C_KR]
        kr_t = lat[:, C_KR:C_KS].T
        k_rope = jnp.concatenate([_rope_rows(kr_t[:MLA_ROPE], tab), one_row], axis=0).T.astype(BF16)

        cqn = (_rms(cq) * qn_ref[...]).astype(BF16)
        q_t = lax.dot_general(wuqt_ref[...], cqn, NT_DIMS, preferred_element_type=F32)
        for hh in range(MLA_HEADS):
            q0 = hh * MLA_QK_PAD
            qh = q_t[hh * (MLA_NOPE + MLA_ROPE):(hh + 1) * (MLA_NOPE + MLA_ROPE)]
            qt_ref[0, t, q0:q0 + MLA_NOPE, :] = (qh[:MLA_NOPE] * MLA_Q_SCALE).astype(BF16)
            q_rope = jnp.concatenate([_rope_rows(qh[MLA_NOPE:], tab) * MLA_Q_SCALE, rope_pad], axis=0)
            qt_ref[0, t, q0 + MLA_NOPE:q0 + MLA_QK_PAD, :] = q_rope.astype(BF16)

        ckvn = (_rms(ckv) * kvn_ref[...]).astype(BF16)
        kn = jnp.dot(ckvn, wk_ref[...], preferred_element_type=F32)
        for hh in range(MLA_HEADS):
            c0 = hh * MLA_QK_PAD
            km_ref[0, rows, c0:c0 + MLA_NOPE] = kn[:, hh * MLA_NOPE:(hh + 1) * MLA_NOPE].astype(BF16)
            km_ref[0, rows, c0 + MLA_NOPE:c0 + MLA_QK_PAD] = k_rope
        vt = lax.dot_general(wvt_ref[...], ckvn, NT_DIMS, preferred_element_type=F32)
        for hh in range(MLA_HEADS):
            v0 = hh * MLA_VT_ROWS
            vt_ref[0, t, v0:v0 + MLA_V, :] = vt[hh * MLA_V:(hh + 1) * MLA_V].astype(BF16)
            vt_ref[0, t, v0 + MLA_V:v0 + MLA_VT_ROWS, :] = mla_ones_rows

    for t in range(x_ref.shape[1] // tile):
        token_tile(t)


def _in_proj(x, norm_mix, mod, positions, w_in_p, w_qv_t, q_norm, w_uq_t, kv_norm, w_k, w_v_t, tile, tiles_per_step):
    B, S, D = x.shape
    tm = tile * tiles_per_step
    inv_freq = ROPE_THETA ** (-jnp.arange(0, MLA_ROPE, 2, dtype=F32) / MLA_ROPE)
    kvw = SWA_KV_HEADS * SWA_HEAD_DIM
    tok = lambda w: pl.BlockSpec((1, tm, w), lambda b, i: (b, i, 0))
    tok_t = lambda r: pl.BlockSpec((1, r, tm), lambda b, i: (b, 0, i))
    tile_t = lambda r: pl.BlockSpec((1, tiles_per_step, r, tile), lambda b, i: (b, i, 0, 0))
    sds = lambda *shape: jax.ShapeDtypeStruct(shape, BF16)
    return pl.pallas_call(
        functools.partial(_in_proj_kernel, tile=tile),
        out_shape=[sds(B, S // tile, MLA_HEADS * MLA_QK_PAD, tile), sds(B, S, MLA_HEADS * MLA_QK_PAD),
                   sds(B, S // tile, MLA_HEADS * MLA_VT_ROWS, tile), sds(B, SWA_HEADS * SWA_HEAD_DIM, S),
                   sds(B, S, kvw), sds(B, S // BLOCK, SWA_KV_HEADS, SWA_VT_ROWS, BLOCK)],
        grid=(B, S // tm),
        in_specs=[tok(D), _resident((1, D)), _resident(mod.shape), tok_t(1), _resident((ROPE_HALF, 1)),
                  _resident(w_in_p.shape), _resident(w_qv_t.shape), _resident((1, Q_LORA)),
                  _resident(w_uq_t.shape), _resident((1, KV_LORA)), _resident(w_k.shape),
                  _resident(w_v_t.shape)],
        out_specs=[tile_t(MLA_HEADS * MLA_QK_PAD), tok(MLA_HEADS * MLA_QK_PAD), tile_t(MLA_HEADS * MLA_VT_ROWS),
                   tok_t(SWA_HEADS * SWA_HEAD_DIM), tok(kvw),
                   pl.BlockSpec((1, tm // BLOCK, SWA_KV_HEADS, SWA_VT_ROWS, BLOCK), lambda b, i: (b, i, 0, 0, 0))],
        compiler_params=_params(("arbitrary", "arbitrary")),
        name="in_proj",
    )(x, norm_mix.reshape(1, D), mod, positions.reshape(B, 1, S), inv_freq.reshape(ROPE_HALF, 1),
      w_in_p, w_qv_t, q_norm.reshape(1, Q_LORA), w_uq_t, kv_norm.reshape(1, KV_LORA), w_k, w_v_t)


def _mla_kernel(qt_ref, k_ref, vt_ref, o_ref, *, tk, tr, lookahead):
    n_tiles, _, tq = qt_ref.shape[1:]
    S = k_ref.shape[1]
    n_heads = k_ref.shape[2] // MLA_QK_PAD
    n_chunks = S // tk
    tiles_per_chunk = tk // tq
    g0 = MLA_SHIFT_COL // BF16_ROWS * BF16_ROWS
    row = lax.broadcasted_iota(jnp.int32, (BF16_ROWS, tq), 0)
    qk = lambda hd: slice(hd * MLA_QK_PAD, (hd + 1) * MLA_QK_PAD)

    def shifted_q(hd, i):
        qt = qt_ref[0, i, qk(hd), :]
        s = jnp.dot(k_ref[0, :tr, qk(hd)], qt, preferred_element_type=F32)
        ref = jnp.max(s, axis=0, keepdims=True)
        grp = jnp.where(row == MLA_SHIFT_COL - g0, -ref, qt[g0:g0 + BF16_ROWS].astype(F32)).astype(BF16)
        return jnp.concatenate([qt[:g0], grp, qt[g0 + BF16_ROWS:]], axis=0)

    def probs(item, qts):
        hd, i, c = item
        s = jnp.dot(k_ref[0, c * tk:(c + 1) * tk, qk(hd)], qts[hd, i], preferred_element_type=F32)
        return jnp.exp2(s).astype(BF16)

    def values_t(hd, c):
        rows = slice(hd * MLA_VT_ROWS, (hd + 1) * MLA_VT_ROWS)
        return jnp.concatenate([vt_ref[0, c * tiles_per_chunk + j, rows, :] for j in range(tiles_per_chunk)], axis=1)

    tiles = [(hd, i) for hd in range(n_heads) for i in range(n_tiles)]
    items = [(hd, i, c) for hd, i in tiles for c in range(n_chunks)]
    qts = {tiles[0]: shifted_q(*tiles[0])}
    ahead = [probs(item, qts) for item in items[:lookahead]]
    o_sum = jnp.zeros((MLA_V, tq), F32)
    for idx, (hd, i, c) in enumerate(items):
        t = idx // n_chunks
        if c == 0 and t + 1 < len(tiles):
            qts[tiles[t + 1]] = shifted_q(*tiles[t + 1])
        p = ahead.pop(0)
        if idx + lookahead < len(items):
            ahead.append(probs(items[idx + lookahead], qts))
        pv = jnp.dot(values_t(hd, c), p, preferred_element_type=F32)
        acc = pv if c == 0 else acc + pv
        if c == n_chunks - 1:
            o = acc[:MLA_V] / acc[MLA_V:MLA_V + 1]
            o_ref[0, i, hd * MLA_V:(hd + 1) * MLA_V, :] = o.astype(BF16)
            o_sum = o_sum + o
            qts.pop((hd, i))
    bad = jnp.max(jnp.where(jnp.isfinite(o_sum), 0.0, 1.0))

    @pl.when(bad > 0.0)
    def _recompute_with_running_max():
        for hd in range(n_heads):
            def tile_body(i, carry, hd=hd):
                qt = qt_ref[0, i, qk(hd), :]

                def key_tile_body(j, state):
                    m, l, acc = state
                    k = k_ref[0, pl.ds(pl.multiple_of(j * tq, tq), tq), qk(hd)]
                    s = jnp.dot(k, qt, preferred_element_type=F32)
                    m_new = jnp.maximum(m, jnp.max(s, axis=0, keepdims=True))
                    p = jnp.exp2(s - m_new)
                    alpha = jnp.exp2(m - m_new)
                    l = alpha * l + jnp.sum(p, axis=0, keepdims=True)
                    vt = vt_ref[0, j, hd * MLA_VT_ROWS:hd * MLA_VT_ROWS + MLA_V, :]
                    acc = alpha * acc + jnp.dot(vt, p.astype(BF16), preferred_element_type=F32)
                    return m_new, l, acc

                init = (jnp.full((1, tq), -jnp.inf, F32), jnp.zeros((1, tq), F32), jnp.zeros((MLA_V, tq), F32))
                _, l, acc = lax.fori_loop(0, n_tiles, key_tile_body, init)
                o_ref[0, i, hd * MLA_V:(hd + 1) * MLA_V, :] = (acc / l).astype(BF16)
                return carry

            lax.fori_loop(0, n_tiles, tile_body, 0)


def _mla_attn(qt_mla, k_mla, vt_mla, heads_per_step, tk, tr, lookahead):
    B, n_tiles, _, tq = qt_mla.shape
    S = k_mla.shape[1]
    hps = heads_per_step
    return pl.pallas_call(
        functools.partial(_mla_kernel, tk=tk, tr=tr, lookahead=lookahead),
        out_shape=jax.ShapeDtypeStruct((B, n_tiles, MLA_HEADS * MLA_V, tq), BF16),
        grid=(B, MLA_HEADS // hps),
        in_specs=[pl.BlockSpec((1, n_tiles, hps * MLA_QK_PAD, tq), lambda b, h: (b, 0, h, 0)),
                  pl.BlockSpec((1, S, hps * MLA_QK_PAD), lambda b, h: (b, 0, h)),
                  pl.BlockSpec((1, n_tiles, hps * MLA_VT_ROWS, tq), lambda b, h: (b, 0, h, 0))],
        out_specs=pl.BlockSpec((1, n_tiles, hps * MLA_V, tq), lambda b, h: (b, 0, h, 0)),
        compiler_params=_params(("arbitrary", "arbitrary")),
        name="mla_attn",
    )(qt_mla, k_mla, vt_mla)


def _swa_kernel(qt_ref, k_ref, vt_ref, t_ref, o_ref, *, sub_blocks, unit_heads, lookahead):
    S = k_ref.shape[1]
    span = BLOCK + 2 * WINDOW
    n_win = span // BLOCK
    step = pl.program_id(1)
    units = [(sb, hd0) for sb in range(sub_blocks) for hd0 in range(0, SWA_HEADS, unit_heads)]

    def window(sb):
        q0 = (step * sub_blocks + sb) * BLOCK
        start = pl.multiple_of(jnp.clip(q0 - WINDOW, 0, S - span), BLOCK)
        cb0 = jnp.where(q0 == 0, 2, jnp.where(q0 == S - BLOCK, 0, 1))
        return start, cb0

    def scores(sb, hd0):
        start, cb0 = window(sb)
        n = hd0 // SWA_GROUP
        heads = range(hd0, hd0 + unit_heads)
        qt = jnp.concatenate(
            [qt_ref[0, hd * SWA_HEAD_DIM:(hd + 1) * SWA_HEAD_DIM, sb * BLOCK:(sb + 1) * BLOCK] for hd in heads],
            axis=1)
        kw = k_ref[0, pl.ds(start, span), n * SWA_HEAD_DIM:(n + 1) * SWA_HEAD_DIM]
        bias = jnp.concatenate(
            [jnp.concatenate([t_ref[hd, cb0 + c] for c in range(n_win)], axis=0) for hd in heads], axis=1)
        return jnp.dot(kw, qt, preferred_element_type=F32) + bias

    def values_t(sb, hd0):
        start, _ = window(sb)
        blk0 = start // BLOCK
        return jnp.concatenate([vt_ref[0, blk0 + c, hd0 // SWA_GROUP] for c in range(n_win)], axis=1)

    def store(sb, hd0, o):
        for g in range(unit_heads):
            hd = hd0 + g
            o_ref[0, hd * SWA_HEAD_DIM:(hd + 1) * SWA_HEAD_DIM, sb * BLOCK:(sb + 1) * BLOCK] = (
                o[:, g * BLOCK:(g + 1) * BLOCK].astype(BF16))

    o_sum = jnp.zeros((SWA_HEAD_DIM, unit_heads * BLOCK), F32)
    probs = lambda unit: jnp.exp2(scores(*unit)).astype(BF16)
    ahead = [probs(unit) for unit in units[:lookahead]]
    for u, unit in enumerate(units):
        p = ahead.pop(0)
        if u + lookahead < len(units):
            ahead.append(probs(units[u + lookahead]))
        ov = jnp.dot(values_t(*unit), p, preferred_element_type=F32)
        o = ov[:SWA_HEAD_DIM] / (ov[SWA_HEAD_DIM:SWA_HEAD_DIM + 1] + 1.0)
        store(*unit, o)
        o_sum = o_sum + o
    bad = jnp.max(jnp.where(jnp.isfinite(o_sum), 0.0, 1.0))

    @pl.when(bad > 0.0)
    def _recompute_with_row_max():
        for unit in units:
            s = scores(*unit)
            m = jnp.maximum(jnp.max(s, axis=0, keepdims=True), 0.0)
            p = jnp.exp2(s - m)
            l = jnp.sum(p, axis=0, keepdims=True) + jnp.exp2(-m)
            ov = jnp.dot(values_t(*unit), p.astype(BF16), preferred_element_type=F32)
            store(*unit, ov[:SWA_HEAD_DIM] / l)


def _swa_attn(qs_t, ks, vs_t, bias_tab, sub_blocks, unit_heads, lookahead):
    B, W, S = qs_t.shape
    kvw = ks.shape[2]
    tq = sub_blocks * BLOCK
    return pl.pallas_call(
        functools.partial(_swa_kernel, sub_blocks=sub_blocks, unit_heads=unit_heads, lookahead=lookahead),
        out_shape=jax.ShapeDtypeStruct((B, W, S), BF16),
        grid=(B, S // tq),
        in_specs=[pl.BlockSpec((1, W, tq), lambda b, i: (b, 0, i)),
                  pl.BlockSpec((1, S, kvw), lambda b, i: (b, 0, 0)),
                  pl.BlockSpec((1, S // BLOCK, SWA_KV_HEADS, SWA_VT_ROWS, BLOCK), lambda b, i: (b, 0, 0, 0, 0)),
                  _resident(bias_tab.shape)],
        out_specs=pl.BlockSpec((1, W, tq), lambda b, i: (b, 0, i)),
        compiler_params=_params(("arbitrary", "arbitrary")),
        name="swa_attn",
    )(qs_t, ks, vs_t, bias_tab)


def _out_mlp_kernel(x_ref, oat_ref, obt_ref, mod_ref, nmix_ref, nmlp_ref, nfin_ref,
                    wg_ref, woa_ref, wob_ref, wout_ref, w1_ref, w2_ref, o_ref, *, tf, n_groups):
    tm, D = x_ref.shape[1:]
    groups = [slice(r, r + tm // n_groups) for r in range(0, tm, tm // n_groups)]

    def gated_merge(rows):
        y_a = lax.dot_general(oat_ref[0, 0, :, rows], woa_ref[...], TN_DIMS, preferred_element_type=F32)
        y_b = lax.dot_general(obt_ref[0, :, rows], wob_ref[...], TN_DIMS, preferred_element_type=F32)
        h = (_rms(x_ref[0, rows, :]) * nmix_ref[...] * (1.0 + _mod_row(mod_ref, MOD_SC1))
             + _mod_row(mod_ref, MOD_SH1)).astype(BF16)
        merged = None
        for j, y in enumerate((y_a, y_b)):
            g = jnp.dot(h, wg_ref[:, j * D:(j + 1) * D], preferred_element_type=F32)
            term = (0.5 * jnp.tanh(0.5 * g) + 0.5) * y
            merged = term if merged is None else merged + term
        return merged.astype(BF16)

    def residual(rows, merged):
        att = jnp.dot(merged, wout_ref[...], preferred_element_type=F32)
        return x_ref[0, rows, :] + _mod_row(mod_ref, MOD_G1) * att

    def mlp(rows, x1):
        h = (_rms(x1) * nmlp_ref[...] * (1.0 + _mod_row(mod_ref, MOD_SC2)) + _mod_row(mod_ref, MOD_SH2)).astype(BF16)
        ff = jnp.zeros_like(x1)
        for c in range(w1_ref.shape[1] // tf):
            a = jnp.dot(h, w1_ref[:, c * tf:(c + 1) * tf], preferred_element_type=F32)
            a = jnp.square(jnp.maximum(a, 0.0)).astype(BF16)
            ff = ff + jnp.dot(a, w2_ref[c * tf:(c + 1) * tf, :], preferred_element_type=F32)
        x2 = x1 + _mod_row(mod_ref, MOD_G2) * ff
        o_ref[0, rows, :] = _rms(x2) * nfin_ref[...]

    merged = [gated_merge(rows) for rows in groups]
    x1s = [residual(rows, m) for rows, m in zip(groups, merged)]
    for rows, x1 in zip(groups, x1s):
        mlp(rows, x1)


def _out_mlp(x, o_mla, o_swa, mod, norm_mix, norm_mlp, norm_final,
             w_g, w_o_mla, w_o_swa, w_out, w_ff1, w_ff2, tm, tf, n_groups):
    B, S, D = x.shape
    tok = lambda w: pl.BlockSpec((1, tm, w), lambda b, i: (b, i, 0))
    return pl.pallas_call(
        functools.partial(_out_mlp_kernel, tf=tf, n_groups=n_groups),
        out_shape=jax.ShapeDtypeStruct((B, S, D), F32),
        grid=(B, S // tm),
        in_specs=[tok(D), pl.BlockSpec((1, 1, D, tm), lambda b, i: (b, i, 0, 0)),
                  pl.BlockSpec((1, D, tm), lambda b, i: (b, 0, i)),
                  _resident(mod.shape), _resident((1, D)), _resident((1, D)), _resident((1, D)),
                  _resident(w_g.shape), _resident(w_o_mla.shape), _resident(w_o_swa.shape),
                  _resident(w_out.shape), _resident(w_ff1.shape), _resident(w_ff2.shape)],
        out_specs=tok(D),
        compiler_params=_params(("arbitrary", "arbitrary")),
        name="out_mlp",
    )(x, o_mla, o_swa, mod, norm_mix.reshape(1, D), norm_mlp.reshape(1, D), norm_final.reshape(1, D),
      w_g, w_o_mla, w_o_swa, w_out, w_ff1, w_ff2)


def _pack_w_in(w_in):
    kr1 = Q_LORA + KV_LORA + MLA_ROPE
    qs1 = kr1 + SWA_HEADS * SWA_HEAD_DIM
    ks1 = qs1 + SWA_KV_HEADS * SWA_HEAD_DIM
    vs1 = ks1 + SWA_KV_HEADS * SWA_HEAD_DIM
    pad = jnp.zeros((w_in.shape[0], LANES - MLA_ROPE), w_in.dtype)
    w_tok = jnp.concatenate([w_in[:, :kr1], pad, w_in[:, qs1:ks1]], axis=1)
    w_t = jnp.concatenate([w_in[:, kr1:qs1], w_in[:, ks1:vs1]], axis=1).T
    return w_tok.astype(BF16), w_t.astype(BF16), w_in[:, vs1:].astype(BF16)


def _split_heads(w, n_first):
    w3 = w.reshape(w.shape[0], MLA_HEADS, -1)
    first = w3[:, :, :n_first].reshape(w.shape[0], -1)
    second = w3[:, :, n_first:].reshape(w.shape[0], -1)
    return first.astype(BF16), second.astype(BF16)


def kernel(x, c, positions, w_ada, b_ada, norm_mix, w_in, q_norm, w_uq, kv_norm, w_ukv, rel_bias, sink,
           w_o_mla, w_o_swa, w_out, norm_mlp, w_ff1, w_ff2, norm_final):
    B, S, D = x.shape
    assert w_ada.shape[0] == 1, "single-layer block"
    assert D == D_MODEL and S % (4 * BLOCK) == 0 and S >= BLOCK + 2 * WINDOW

    mod = _ada_mod(c, w_ada[0], b_ada[0])
    bias_tab = _swa_bias(rel_bias, sink[0])

    tile = 512
    w_k, w_v = _split_heads(w_ukv[0], MLA_NOPE)
    w_in_tok, w_in_t, w_gate = _pack_w_in(w_in[0])
    qt_mla, k_mla, vt_mla, qs_t, ks, vs_t = _in_proj(
        x, norm_mix[0], mod, positions, w_in_tok, w_in_t, q_norm[0], w_uq[0].T.astype(BF16),
        kv_norm[0], w_k, w_v.T, tile=tile, tiles_per_step=2)

    ot_mla = _mla_attn(qt_mla, k_mla, vt_mla, heads_per_step=2, tk=1024, tr=256, lookahead=1)
    ot_swa = _swa_attn(qs_t, ks, vs_t, bias_tab, sub_blocks=8, unit_heads=2, lookahead=2)

    return _out_mlp(x, ot_mla, ot_swa, mod, norm_mix[0], norm_mlp[0], norm_final, w_gate,
                    w_o_mla[0].astype(BF16), w_o_swa[0].astype(BF16), w_out[0].astype(BF16),
                    w_ff1[0].astype(BF16), w_ff2[0].astype(BF16), tm=tile, tf=1024, n_groups=2)
```

```python
import functools
import math

import jax
import jax.numpy as jnp
from jax import lax
from jax.experimental import pallas as pl
from jax.experimental.pallas import tpu as pltpu

F32 = jnp.float32
BF16 = jnp.bfloat16

D_MODEL = 1024
MLA_HEADS = 8
MLA_NOPE = 128
MLA_ROPE = 64
MLA_V = 128
Q_LORA = 384
KV_LORA = 256
ROPE_THETA = 10000.0
SWA_HEADS = 8
SWA_KV_HEADS = 2
SWA_GROUP = SWA_HEADS // SWA_KV_HEADS
SWA_HEAD_DIM = 128
WINDOW = 128
BLOCK = 128
N_BUCKETS = 32
N_MOD = 6
EPS = 1e-6
NEG_INF = -1e30

ROPE_HALF = MLA_ROPE // 2
LANES = 128
BF16_ROWS = 16
MLA_QK_PAD = 256
MLA_SHIFT_COL = MLA_NOPE + MLA_ROPE
MLA_VT_ROWS = MLA_V + 16
SWA_VT_ROWS = SWA_HEAD_DIM + 16
VMEM_LIMIT = 56 * 1024 * 1024

LOG2_E = math.log2(math.e)
MLA_Q_SCALE = (MLA_NOPE + MLA_ROPE) ** -0.5 * LOG2_E
SWA_Q_SCALE = SWA_HEAD_DIM ** -0.5 * LOG2_E

C_CQ = 0
C_CKV = C_CQ + Q_LORA
C_KR = C_CKV + KV_LORA
C_KS = C_KR + LANES
C_END = C_KS + SWA_KV_HEADS * SWA_HEAD_DIM

T5_LARGE_THRESHOLDS = (12, 16, 23, 32, 46, 64, 91)

NT_DIMS = (((1,), (1,)), ((), ()))
TN_DIMS = (((0,), (0,)), ((), ()))


def _resident(shape):
    nd = len(shape)
    return pl.BlockSpec(shape, lambda *_: (0,) * nd, pipeline_mode=pl.Buffered(1))


def _params(semantics):
    return pltpu.CompilerParams(dimension_semantics=semantics, vmem_limit_bytes=VMEM_LIMIT)


def _rms(x):
    return x * lax.rsqrt(jnp.mean(x * x, axis=-1, keepdims=True) + EPS)


MOD_SH1, MOD_SC1, MOD_G1, MOD_SH2, MOD_SC2, MOD_G2 = range(N_MOD)


def _mod_row(mod_ref, chunk):
    return mod_ref[pl.ds(pl.program_id(0), 1), chunk * D_MODEL:(chunk + 1) * D_MODEL]


def _ada_kernel(c_ref, w_ref, b_ref, o_ref):
    c = c_ref[...]
    c_act = c * jax.nn.sigmoid(c)
    o_ref[...] = jnp.dot(c_act, w_ref[...], preferred_element_type=F32,
                         precision=lax.Precision.HIGHEST) + b_ref[...]


def _ada_mod(c, w_ada, b_ada):
    B, D = c.shape
    N = w_ada.shape[1]
    tn = 1024
    return pl.pallas_call(
        _ada_kernel,
        out_shape=jax.ShapeDtypeStruct((B, N), F32),
        grid=(N // tn,),
        in_specs=[pl.BlockSpec((B, D), lambda j: (0, 0)),
                  pl.BlockSpec((D, tn), lambda j: (0, j)),
                  pl.BlockSpec((1, tn), lambda j: (0, j))],
        out_specs=pl.BlockSpec((B, tn), lambda j: (0, j)),
        compiler_params=_params(("arbitrary",)),
        name="ada_mod",
    )(c, w_ada, b_ada.reshape(1, N))


def _swa_bias_kernel(tab_ref, sink_ref, o_ref):
    h = pl.program_id(0)
    half = N_BUCKETS // 2
    max_exact = half // 2
    kj = lax.broadcasted_iota(jnp.int32, (BLOCK, LANES), 0)
    qi = lax.broadcasted_iota(jnp.int32, (BLOCK, LANES), 1)
    for cb in range(5):
        rel = kj - qi + (cb * LANES - 2 * WINDOW)
        n = jnp.abs(rel)
        large = jnp.full_like(n, max_exact)
        for t in T5_LARGE_THRESHOLDS:
            large = large + jnp.where(n >= t, 1, 0)
        bucket = jnp.where(rel > 0, half, 0) + jnp.where(n < max_exact, n, large)
        bias = jnp.zeros((BLOCK, LANES), F32)
        for b in range(N_BUCKETS):
            bias = jnp.where(bucket == b, tab_ref[b, h], bias)
        o_ref[0, cb] = jnp.where(n <= WINDOW, (bias - sink_ref[0, h]) * LOG2_E, NEG_INF)


def _swa_bias(rel_bias, sink):
    return pl.pallas_call(
        _swa_bias_kernel,
        out_shape=jax.ShapeDtypeStruct((SWA_HEADS, 5, BLOCK, LANES), F32),
        grid=(SWA_HEADS,),
        in_specs=[pl.BlockSpec(memory_space=pltpu.SMEM), pl.BlockSpec(memory_space=pltpu.SMEM)],
        out_specs=pl.BlockSpec((1, 5, BLOCK, LANES), lambda h: (h, 0, 0, 0)),
        compiler_params=_params(("arbitrary",)),
        name="swa_bias",
    )(rel_bias, sink.reshape(1, SWA_HEADS))


def _rope_rows(x, tab):
    x1, x2 = x[:ROPE_HALF], x[ROPE_HALF:]
    cos, sin = tab[:ROPE_HALF], tab[ROPE_HALF:]
    return jnp.concatenate([x1 * cos - x2 * sin, x2 * cos + x1 * sin], axis=0)


def _in_proj_kernel(x_ref, nm_ref, mod_ref, pos_ref, inv_ref, win_ref, wqvt_ref, qn_ref, wuqt_ref, kvn_ref,
                    wk_ref, wvt_ref, qt_ref, km_ref, vt_ref, qst_ref, ks_ref, vst_ref, *, tile):
    n_q = SWA_HEADS * SWA_HEAD_DIM
    rope_pad = jnp.zeros((MLA_QK_PAD - MLA_NOPE - MLA_ROPE, tile), F32)
    one_row = jnp.where(lax.broadcasted_iota(jnp.int32, rope_pad.shape, 0) == 0, 1.0, 0.0)
    mla_ones_rows = jnp.where(lax.broadcasted_iota(jnp.int32, (MLA_VT_ROWS - MLA_V, tile), 0) == 0,
                              1.0, 0.0).astype(BF16)
    swa_ones_rows = jnp.where(lax.broadcasted_iota(jnp.int32, (SWA_VT_ROWS - SWA_HEAD_DIM, BLOCK), 0) == 0,
                              1.0, 0.0).astype(BF16)

    def token_tile(t):
        rows = slice(t * tile, (t + 1) * tile)
        h = (_rms(x_ref[0, rows, :]) * nm_ref[...] * (1.0 + _mod_row(mod_ref, MOD_SC1))
             + _mod_row(mod_ref, MOD_SH1)).astype(BF16)
        ang = inv_ref[...] * pos_ref[0, :, rows].astype(F32)
        tab = jnp.concatenate([jnp.cos(ang), jnp.sin(ang)], axis=0)

        lat = jnp.dot(h, win_ref[:, C_CQ:C_KS], preferred_element_type=F32)
        qv_t = lax.dot_general(wqvt_ref[...], h, NT_DIMS, preferred_element_type=F32)
        qst_ref[0, :, rows] = (qv_t[:n_q] * SWA_Q_SCALE).astype(BF16)
        for n in range(SWA_KV_HEADS):
            vs_t = qv_t[n_q + n * SWA_HEAD_DIM:n_q + (n + 1) * SWA_HEAD_DIM].astype(BF16)
            for j in range(tile // BLOCK):
                blk = t * (tile // BLOCK) + j
                vst_ref[0, blk, n, :SWA_HEAD_DIM, :] = vs_t[:, j * BLOCK:(j + 1) * BLOCK]
                vst_ref[0, blk, n, SWA_HEAD_DIM:, :] = swa_ones_rows
        ks_ref[0, rows, :] = jnp.dot(h, win_ref[:, C_KS:C_END], preferred_element_type=F32).astype(BF16)

        cq = lat[:, C_CQ:C_CKV]
        ckv = lat[:, C_CKV:C_KR]
        kr_t = lat[:, C_KR:C_KS].T
        k_rope = jnp.concatenate([_rope_rows(kr_t[:MLA_ROPE], tab), one_row], axis=0).T.astype(BF16)

        cqn = (_rms(cq) * qn_ref[...]).astype(BF16)
        q_t = lax.dot_general(wuqt_ref[...], cqn, NT_DIMS, preferred_element_type=F32)
        for hh in range(MLA_HEADS):
            q0 = hh * MLA_QK_PAD
            qh = q_t[hh * (MLA_NOPE + MLA_ROPE):(hh + 1) * (MLA_NOPE + MLA_ROPE)]
            qt_ref[0, t, q0:q0 + MLA_NOPE, :] = (qh[:MLA_NOPE] * MLA_Q_SCALE).astype(BF16)
            q_rope = jnp.concatenate([_rope_rows(qh[MLA_NOPE:], tab) * MLA_Q_SCALE, rope_pad], axis=0)
            qt_ref[0, t, q0 + MLA_NOPE:q0 + MLA_QK_PAD, :] = q_rope.astype(BF16)

        ckvn = (_rms(ckv) * kvn_ref[...]).astype(BF16)
        kn = jnp.dot(ckvn, wk_ref[...], preferred_element_type=F32)
        for hh in range(MLA_HEADS):
            km_ref[0, hh, rows, :MLA_NOPE] = kn[:, hh * MLA_NOPE:(hh + 1) * MLA_NOPE].astype(BF16)
            km_ref[0, hh, rows, MLA_NOPE:] = k_rope
        vt = lax.dot_general(wvt_ref[...], ckvn, NT_DIMS, preferred_element_type=F32)
        for hh in range(MLA_HEADS):
            v0 = hh * MLA_VT_ROWS
            vt_ref[0, t, v0:v0 + MLA_V, :] = vt[hh * MLA_V:(hh + 1) * MLA_V].astype(BF16)
            vt_ref[0, t, v0 + MLA_V:v0 + MLA_VT_ROWS, :] = mla_ones_rows

    for t in range(x_ref.shape[1] // tile):
        token_tile(t)


def _in_proj(x, norm_mix, mod, positions, w_in_p, w_qv_t, q_norm, w_uq_t, kv_norm, w_k, w_v_t, tile, tiles_per_step):
    B, S, D = x.shape
    tm = tile * tiles_per_step
    inv_freq = ROPE_THETA ** (-jnp.arange(0, MLA_ROPE, 2, dtype=F32) / MLA_ROPE)
    kvw = SWA_KV_HEADS * SWA_HEAD_DIM
    tok = lambda w: pl.BlockSpec((1, tm, w), lambda b, i: (b, i, 0))
    tok_t = lambda r: pl.BlockSpec((1, r, tm), lambda b, i: (b, 0, i))
    tile_t = lambda r: pl.BlockSpec((1, tiles_per_step, r, tile), lambda b, i: (b, i, 0, 0))
    sds = lambda *shape: jax.ShapeDtypeStruct(shape, BF16)
    return pl.pallas_call(
        functools.partial(_in_proj_kernel, tile=tile),
        out_shape=[sds(B, S // tile, MLA_HEADS * MLA_QK_PAD, tile), sds(B, MLA_HEADS, S, MLA_QK_PAD),
                   sds(B, S // tile, MLA_HEADS * MLA_VT_ROWS, tile), sds(B, SWA_HEADS * SWA_HEAD_DIM, S),
                   sds(B, S, kvw), sds(B, S // BLOCK, SWA_KV_HEADS, SWA_VT_ROWS, BLOCK)],
        grid=(B, S // tm),
        in_specs=[tok(D), _resident((1, D)), _resident(mod.shape), tok_t(1), _resident((ROPE_HALF, 1)),
                  _resident(w_in_p.shape), _resident(w_qv_t.shape), _resident((1, Q_LORA)),
                  _resident(w_uq_t.shape), _resident((1, KV_LORA)), _resident(w_k.shape),
                  _resident(w_v_t.shape)],
        out_specs=[tile_t(MLA_HEADS * MLA_QK_PAD),
                   pl.BlockSpec((1, MLA_HEADS, tm, MLA_QK_PAD), lambda b, i: (b, 0, i, 0)),
                   tile_t(MLA_HEADS * MLA_VT_ROWS),
                   tok_t(SWA_HEADS * SWA_HEAD_DIM), tok(kvw),
                   pl.BlockSpec((1, tm // BLOCK, SWA_KV_HEADS, SWA_VT_ROWS, BLOCK), lambda b, i: (b, i, 0, 0, 0))],
        compiler_params=_params(("arbitrary", "arbitrary")),
        name="in_proj",
    )(x, norm_mix.reshape(1, D), mod, positions.reshape(B, 1, S), inv_freq.reshape(ROPE_HALF, 1),
      w_in_p, w_qv_t, q_norm.reshape(1, Q_LORA), w_uq_t, kv_norm.reshape(1, KV_LORA), w_k, w_v_t)


def _mla_kernel(qt_ref, k_ref, vt_ref, o_ref, *, tk, tr, lookahead):
    n_tiles, _, tq = qt_ref.shape[1:]
    n_heads, S = k_ref.shape[1:3]
    n_chunks = S // tk
    tiles_per_chunk = tk // tq
    g0 = MLA_SHIFT_COL // BF16_ROWS * BF16_ROWS
    row = lax.broadcasted_iota(jnp.int32, (BF16_ROWS, tq), 0)
    qk = lambda hd: slice(hd * MLA_QK_PAD, (hd + 1) * MLA_QK_PAD)

    def shifted_q(hd, i):
        qt = qt_ref[0, i, qk(hd), :]
        s = jnp.dot(k_ref[0, hd, :tr, :], qt, preferred_element_type=F32)
        ref = jnp.max(s, axis=0, keepdims=True)
        grp = jnp.where(row == MLA_SHIFT_COL - g0, -ref, qt[g0:g0 + BF16_ROWS].astype(F32)).astype(BF16)
        return jnp.concatenate([qt[:g0], grp, qt[g0 + BF16_ROWS:]], axis=0)

    def probs(item, qts):
        hd, i, c = item
        s = jnp.dot(k_ref[0, hd, c * tk:(c + 1) * tk, :], qts[hd, i], preferred_element_type=F32)
        return jnp.exp2(s).astype(BF16)

    def values_t(hd, c):
        rows = slice(hd * MLA_VT_ROWS, (hd + 1) * MLA_VT_ROWS)
        return jnp.concatenate([vt_ref[0, c * tiles_per_chunk + j, rows, :] for j in range(tiles_per_chunk)], axis=1)

    tiles = [(hd, i) for hd in range(n_heads) for i in range(n_tiles)]
    items = [(hd, i, c) for hd, i in tiles for c in range(n_chunks)]
    qts = {tiles[0]: shifted_q(*tiles[0])}
    ahead = [probs(item, qts) for item in items[:lookahead]]
    o_sum = jnp.zeros((MLA_V, tq), F32)
    for idx, (hd, i, c) in enumerate(items):
        t = idx // n_chunks
        if c == 0 and t + 1 < len(tiles):
            qts[tiles[t + 1]] = shifted_q(*tiles[t + 1])
        p = ahead.pop(0)
        if idx + lookahead < len(items):
            ahead.append(probs(items[idx + lookahead], qts))
        pv = jnp.dot(values_t(hd, c), p, preferred_element_type=F32)
        acc = pv if c == 0 else acc + pv
        if c == n_chunks - 1:
            o = acc[:MLA_V] / acc[MLA_V:MLA_V + 1]
            o_ref[0, i, hd * MLA_V:(hd + 1) * MLA_V, :] = o.astype(BF16)
            o_sum = o_sum + o
            qts.pop((hd, i))
    bad = jnp.max(jnp.where(jnp.isfinite(o_sum), 0.0, 1.0))

    @pl.when(bad > 0.0)
    def _recompute_with_running_max():
        for hd in range(n_heads):
            def tile_body(i, carry, hd=hd):
                qt = qt_ref[0, i, qk(hd), :]

                def key_tile_body(j, state):
                    m, l, acc = state
                    k = k_ref[0, hd, pl.ds(pl.multiple_of(j * tq, tq), tq), :]
                    s = jnp.dot(k, qt, preferred_element_type=F32)
                    m_new = jnp.maximum(m, jnp.max(s, axis=0, keepdims=True))
                    p = jnp.exp2(s - m_new)
                    alpha = jnp.exp2(m - m_new)
                    l = alpha * l + jnp.sum(p, axis=0, keepdims=True)
                    vt = vt_ref[0, j, hd * MLA_VT_ROWS:hd * MLA_VT_ROWS + MLA_V, :]
                    acc = alpha * acc + jnp.dot(vt, p.astype(BF16), preferred_element_type=F32)
                    return m_new, l, acc

                init = (jnp.full((1, tq), -jnp.inf, F32), jnp.zeros((1, tq), F32), jnp.zeros((MLA_V, tq), F32))
                _, l, acc = lax.fori_loop(0, n_tiles, key_tile_body, init)
                o_ref[0, i, hd * MLA_V:(hd + 1) * MLA_V, :] = (acc / l).astype(BF16)
                return carry

            lax.fori_loop(0, n_tiles, tile_body, 0)


def _mla_attn(qt_mla, k_mla, vt_mla, heads_per_step, tk, tr, lookahead):
    B, n_tiles, _, tq = qt_mla.shape
    S = k_mla.shape[2]
    hps = heads_per_step
    return pl.pallas_call(
        functools.partial(_mla_kernel, tk=tk, tr=tr, lookahead=lookahead),
        out_shape=jax.ShapeDtypeStruct((B, n_tiles, MLA_HEADS * MLA_V, tq), BF16),
        grid=(B, MLA_HEADS // hps),
        in_specs=[pl.BlockSpec((1, n_tiles, hps * MLA_QK_PAD, tq), lambda b, h: (b, 0, h, 0)),
                  pl.BlockSpec((1, hps, S, MLA_QK_PAD), lambda b, h: (b, h, 0, 0)),
                  pl.BlockSpec((1, n_tiles, hps * MLA_VT_ROWS, tq), lambda b, h: (b, 0, h, 0))],
        out_specs=pl.BlockSpec((1, n_tiles, hps * MLA_V, tq), lambda b, h: (b, 0, h, 0)),
        compiler_params=_params(("arbitrary", "arbitrary")),
        name="mla_attn",
    )(qt_mla, k_mla, vt_mla)


def _swa_kernel(qt_ref, k_ref, vt_ref, t_ref, o_ref, *, sub_blocks, unit_heads, lookahead):
    S = k_ref.shape[1]
    span = BLOCK + 2 * WINDOW
    n_win = span // BLOCK
    step = pl.program_id(1)
    units = [(sb, hd0) for sb in range(sub_blocks) for hd0 in range(0, SWA_HEADS, unit_heads)]

    def window(sb):
        q0 = (step * sub_blocks + sb) * BLOCK
        start = pl.multiple_of(jnp.clip(q0 - WINDOW, 0, S - span), BLOCK)
        cb0 = jnp.where(q0 == 0, 2, jnp.where(q0 == S - BLOCK, 0, 1))
        return start, cb0

    def scores(sb, hd0):
        start, cb0 = window(sb)
        n = hd0 // SWA_GROUP
        heads = range(hd0, hd0 + unit_heads)
        qt = jnp.concatenate(
            [qt_ref[0, hd * SWA_HEAD_DIM:(hd + 1) * SWA_HEAD_DIM, sb * BLOCK:(sb + 1) * BLOCK] for hd in heads],
            axis=1)
        kw = k_ref[0, pl.ds(start, span), n * SWA_HEAD_DIM:(n + 1) * SWA_HEAD_DIM]
        bias = jnp.concatenate(
            [jnp.concatenate([t_ref[hd, cb0 + c] for c in range(n_win)], axis=0) for hd in heads], axis=1)
        return jnp.dot(kw, qt, preferred_element_type=F32) + bias

    def values_t(sb, hd0):
        start, _ = window(sb)
        blk0 = start // BLOCK
        return jnp.concatenate([vt_ref[0, blk0 + c, hd0 // SWA_GROUP] for c in range(n_win)], axis=1)

    def store(sb, hd0, o):
        for g in range(unit_heads):
            hd = hd0 + g
            o_ref[0, hd * SWA_HEAD_DIM:(hd + 1) * SWA_HEAD_DIM, sb * BLOCK:(sb + 1) * BLOCK] = (
                o[:, g * BLOCK:(g + 1) * BLOCK].astype(BF16))

    o_sum = jnp.zeros((SWA_HEAD_DIM, unit_heads * BLOCK), F32)
    probs = lambda unit: jnp.exp2(scores(*unit)).astype(BF16)
    ahead = [probs(unit) for unit in units[:lookahead]]
    for u, unit in enumerate(units):
        p = ahead.pop(0)
        if u + lookahead < len(units):
            ahead.append(probs(units[u + lookahead]))
        ov = jnp.dot(values_t(*unit), p, preferred_element_type=F32)
        o = ov[:SWA_HEAD_DIM] / (ov[SWA_HEAD_DIM:SWA_HEAD_DIM + 1] + 1.0)
        store(*unit, o)
        o_sum = o_sum + o
    bad = jnp.max(jnp.where(jnp.isfinite(o_sum), 0.0, 1.0))

    @pl.when(bad > 0.0)
    def _recompute_with_row_max():
        for unit in units:
            s = scores(*unit)
            m = jnp.maximum(jnp.max(s, axis=0, keepdims=True), 0.0)
            p = jnp.exp2(s - m)
            l = jnp.sum(p, axis=0, keepdims=True) + jnp.exp2(-m)
            ov = jnp.dot(values_t(*unit), p.astype(BF16), preferred_element_type=F32)
            store(*unit, ov[:SWA_HEAD_DIM] / l)


def _swa_attn(qs_t, ks, vs_t, bias_tab, sub_blocks, unit_heads, lookahead):
    B, W, S = qs_t.shape
    kvw = ks.shape[2]
    tq = sub_blocks * BLOCK
    return pl.pallas_call(
        functools.partial(_swa_kernel, sub_blocks=sub_blocks, unit_heads=unit_heads, lookahead=lookahead),
        out_shape=jax.ShapeDtypeStruct((B, W, S), BF16),
        grid=(B, S // tq),
        in_specs=[pl.BlockSpec((1, W, tq), lambda b, i: (b, 0, i)),
                  pl.BlockSpec((1, S, kvw), lambda b, i: (b, 0, 0)),
                  pl.BlockSpec((1, S // BLOCK, SWA_KV_HEADS, SWA_VT_ROWS, BLOCK), lambda b, i: (b, 0, 0, 0, 0)),
                  _resident(bias_tab.shape)],
        out_specs=pl.BlockSpec((1, W, tq), lambda b, i: (b, 0, i)),
        compiler_params=_params(("arbitrary", "arbitrary")),
        name="swa_attn",
    )(qs_t, ks, vs_t, bias_tab)


def _out_mlp_kernel(x_ref, oat_ref, obt_ref, mod_ref, nmix_ref, nmlp_ref, nfin_ref,
                    wg_ref, woa_ref, wob_ref, wout_ref, w1_ref, w2_ref, o_ref, *, tf, n_groups):
    tm, D = x_ref.shape[1:]
    groups = [slice(r, r + tm // n_groups) for r in range(0, tm, tm // n_groups)]

    def gated_merge(rows):
        y_a = lax.dot_general(oat_ref[0, 0, :, rows], woa_ref[...], TN_DIMS, preferred_element_type=F32)
        y_b = lax.dot_general(obt_ref[0, :, rows], wob_ref[...], TN_DIMS, preferred_element_type=F32)
        h = (_rms(x_ref[0, rows, :]) * nmix_ref[...] * (1.0 + _mod_row(mod_ref, MOD_SC1))
             + _mod_row(mod_ref, MOD_SH1)).astype(BF16)
        merged = None
        for j, y in enumerate((y_a, y_b)):
            g = jnp.dot(h, wg_ref[:, j * D:(j + 1) * D], preferred_element_type=F32)
            term = (0.5 * jnp.tanh(0.5 * g) + 0.5) * y
            merged = term if merged is None else merged + term
        return merged.astype(BF16)

    def residual(rows, merged):
        att = jnp.dot(merged, wout_ref[...], preferred_element_type=F32)
        return x_ref[0, rows, :] + _mod_row(mod_ref, MOD_G1) * att

    def mlp(rows, x1):
        h = (_rms(x1) * nmlp_ref[...] * (1.0 + _mod_row(mod_ref, MOD_SC2)) + _mod_row(mod_ref, MOD_SH2)).astype(BF16)
        ff = jnp.zeros_like(x1)
        for c in range(w1_ref.shape[1] // tf):
            a = jnp.dot(h, w1_ref[:, c * tf:(c + 1) * tf], preferred_element_type=F32)
            a = jnp.square(jnp.maximum(a, 0.0)).astype(BF16)
            ff = ff + jnp.dot(a, w2_ref[c * tf:(c + 1) * tf, :], preferred_element_type=F32)
        x2 = x1 + _mod_row(mod_ref, MOD_G2) * ff
        o_ref[0, rows, :] = _rms(x2) * nfin_ref[...]

    merged = [gated_merge(rows) for rows in groups]
    x1s = [residual(rows, m) for rows, m in zip(groups, merged)]
    for rows, x1 in zip(groups, x1s):
        mlp(rows, x1)


def _out_mlp(x, o_mla, o_swa, mod, norm_mix, norm_mlp, norm_final,
             w_g, w_o_mla, w_o_swa, w_out, w_ff1, w_ff2, tm, tf, n_groups):
    B, S, D = x.shape
    tok = lambda w: pl.BlockSpec((1, tm, w), lambda b, i: (b, i, 0))
    return pl.pallas_call(
        functools.partial(_out_mlp_kernel, tf=tf, n_groups=n_groups),
        out_shape=jax.ShapeDtypeStruct((B, S, D), F32),
        grid=(B, S // tm),
        in_specs=[tok(D), pl.BlockSpec((1, 1, D, tm), lambda b, i: (b, i, 0, 0)),
                  pl.BlockSpec((1, D, tm), lambda b, i: (b, 0, i)),
                  _resident(mod.shape), _resident((1, D)), _resident((1, D)), _resident((1, D)),
                  _resident(w_g.shape), _resident(w_o_mla.shape), _resident(w_o_swa.shape),
                  _resident(w_out.shape), _resident(w_ff1.shape), _resident(w_ff2.shape)],
        out_specs=tok(D),
        compiler_params=_params(("arbitrary", "arbitrary")),
        name="out_mlp",
    )(x, o_mla, o_swa, mod, norm_mix.reshape(1, D), norm_mlp.reshape(1, D), norm_final.reshape(1, D),
      w_g, w_o_mla, w_o_swa, w_out, w_ff1, w_ff2)


def _pack_w_in(w_in):
    kr1 = Q_LORA + KV_LORA + MLA_ROPE
    qs1 = kr1 + SWA_HEADS * SWA_HEAD_DIM
    ks1 = qs1 + SWA_KV_HEADS * SWA_HEAD_DIM
    vs1 = ks1 + SWA_KV_HEADS * SWA_HEAD_DIM
    pad = jnp.zeros((w_in.shape[0], LANES - MLA_ROPE), w_in.dtype)
    w_tok = jnp.concatenate([w_in[:, :kr1], pad, w_in[:, qs1:ks1]], axis=1)
    w_t = jnp.concatenate([w_in[:, kr1:qs1], w_in[:, ks1:vs1]], axis=1).T
    return w_tok.astype(BF16), w_t.astype(BF16), w_in[:, vs1:].astype(BF16)


def _split_heads(w, n_first):
    w3 = w.reshape(w.shape[0], MLA_HEADS, -1)
    first = w3[:, :, :n_first].reshape(w.shape[0], -1)
    second = w3[:, :, n_first:].reshape(w.shape[0], -1)
    return first.astype(BF16), second.astype(BF16)


def kernel(x, c, positions, w_ada, b_ada, norm_mix, w_in, q_norm, w_uq, kv_norm, w_ukv, rel_bias, sink,
           w_o_mla, w_o_swa, w_out, norm_mlp, w_ff1, w_ff2, norm_final):
    B, S, D = x.shape
    assert w_ada.shape[0] == 1, "single-layer block"
    assert D == D_MODEL and S % (4 * BLOCK) == 0 and S >= BLOCK + 2 * WINDOW

    mod = _ada_mod(c, w_ada[0], b_ada[0])
    bias_tab = _swa_bias(rel_bias, sink[0])

    tile = 512
    w_k, w_v = _split_heads(w_ukv[0], MLA_NOPE)
    w_in_tok, w_in_t, w_gate = _pack_w_in(w_in[0])
    qt_mla, k_mla, vt_mla, qs_t, ks, vs_t = _in_proj(
        x, norm_mix[0], mod, positions, w_in_tok, w_in_t, q_norm[0], w_uq[0].T.astype(BF16),
        kv_norm[0], w_k, w_v.T, tile=tile, tiles_per_step=2)

    ot_mla = _mla_attn(qt_mla, k_mla, vt_mla, heads_per_step=2, tk=1024, tr=256, lookahead=1)
    ot_swa = _swa_attn(qs_t, ks, vs_t, bias_tab, sub_blocks=8, unit_heads=2, lookahead=2)

    return _out_mlp(x, ot_mla, ot_swa, mod, norm_mix[0], norm_mlp[0], norm_final, w_gate,
                    w_o_mla[0].astype(BF16), w_o_swa[0].astype(BF16), w_out[0].astype(BF16),
                    w_ff1[0].astype(BF16), w_ff2[0].astype(BF16), tm=tile, tf=1024, n_groups=2)
```

```python
import functools
import math

import jax
import jax.numpy as jnp
from jax import lax
from jax.experimental import pallas as pl
from jax.experimental.pallas import tpu as pltpu

F32 = jnp.float32
BF16 = jnp.bfloat16

D_MODEL = 1024
MLA_HEADS = 8
MLA_NOPE = 128
MLA_ROPE = 64
MLA_V = 128
Q_LORA = 384
KV_LORA = 256
ROPE_THETA = 10000.0
SWA_HEADS = 8
SWA_KV_HEADS = 2
SWA_GROUP = SWA_HEADS // SWA_KV_HEADS
SWA_HEAD_DIM = 128
WINDOW = 128
BLOCK = 128
N_BUCKETS = 32
N_MOD = 6
EPS = 1e-6
NEG_INF = -1e30

ROPE_HALF = MLA_ROPE // 2
LANES = 128
BF16_ROWS = 16
MLA_QK_PAD = 256
MLA_SHIFT_COL = MLA_NOPE + MLA_ROPE
MLA_VT_ROWS = MLA_V + 16
SWA_VT_ROWS = SWA_HEAD_DIM + 16
VMEM_LIMIT = 56 * 1024 * 1024

LOG2_E = math.log2(math.e)
MLA_Q_SCALE = (MLA_NOPE + MLA_ROPE) ** -0.5 * LOG2_E
SWA_Q_SCALE = SWA_HEAD_DIM ** -0.5 * LOG2_E

C_CQ = 0
C_CKV = C_CQ + Q_LORA
C_KR = C_CKV + KV_LORA
C_KS = C_KR + LANES
C_END = C_KS + SWA_KV_HEADS * SWA_HEAD_DIM

T5_LARGE_THRESHOLDS = (12, 16, 23, 32, 46, 64, 91)

NT_DIMS = (((1,), (1,)), ((), ()))
TN_DIMS = (((0,), (0,)), ((), ()))


def _resident(shape):
    nd = len(shape)
    return pl.BlockSpec(shape, lambda *_: (0,) * nd, pipeline_mode=pl.Buffered(1))


def _params(semantics):
    return pltpu.CompilerParams(dimension_semantics=semantics, vmem_limit_bytes=VMEM_LIMIT)


def _rms(x):
    return x * lax.rsqrt(jnp.mean(x * x, axis=-1, keepdims=True) + EPS)


MOD_SH1, MOD_SC1, MOD_G1, MOD_SH2, MOD_SC2, MOD_G2 = range(N_MOD)


def _mod_row(mod_ref, chunk):
    return mod_ref[pl.ds(pl.program_id(0), 1), chunk * D_MODEL:(chunk + 1) * D_MODEL]


def _ada_kernel(c_ref, w_ref, b_ref, o_ref):
    c = c_ref[...]
    c_act = c * jax.nn.sigmoid(c)
    o_ref[...] = jnp.dot(c_act, w_ref[...], preferred_element_type=F32,
                         precision=lax.Precision.HIGHEST) + b_ref[...]


def _ada_mod(c, w_ada, b_ada):
    B, D = c.shape
    N = w_ada.shape[1]
    tn = 1024
    return pl.pallas_call(
        _ada_kernel,
        out_shape=jax.ShapeDtypeStruct((B, N), F32),
        grid=(N // tn,),
        in_specs=[pl.BlockSpec((B, D), lambda j: (0, 0)),
                  pl.BlockSpec((D, tn), lambda j: (0, j)),
                  pl.BlockSpec((1, tn), lambda j: (0, j))],
        out_specs=pl.BlockSpec((B, tn), lambda j: (0, j)),
        compiler_params=_params(("arbitrary",)),
        name="ada_mod",
    )(c, w_ada, b_ada.reshape(1, N))


def _swa_bias_kernel(tab_ref, sink_ref, o_ref):
    h = pl.program_id(0)
    half = N_BUCKETS // 2
    max_exact = half // 2
    kj = lax.broadcasted_iota(jnp.int32, (BLOCK, LANES), 0)
    qi = lax.broadcasted_iota(jnp.int32, (BLOCK, LANES), 1)
    for cb in range(5):
        rel = kj - qi + (cb * LANES - 2 * WINDOW)
        n = jnp.abs(rel)
        large = jnp.full_like(n, max_exact)
        for t in T5_LARGE_THRESHOLDS:
            large = large + jnp.where(n >= t, 1, 0)
        bucket = jnp.where(rel > 0, half, 0) + jnp.where(n < max_exact, n, large)
        bias = jnp.zeros((BLOCK, LANES), F32)
        for b in range(N_BUCKETS):
            bias = jnp.where(bucket == b, tab_ref[b, h], bias)
        o_ref[0, cb] = jnp.where(n <= WINDOW, (bias - sink_ref[0, h]) * LOG2_E, NEG_INF)


def _swa_bias(rel_bias, sink):
    return pl.pallas_call(
        _swa_bias_kernel,
        out_shape=jax.ShapeDtypeStruct((SWA_HEADS, 5, BLOCK, LANES), F32),
        grid=(SWA_HEADS,),
        in_specs=[pl.BlockSpec(memory_space=pltpu.SMEM), pl.BlockSpec(memory_space=pltpu.SMEM)],
        out_specs=pl.BlockSpec((1, 5, BLOCK, LANES), lambda h: (h, 0, 0, 0)),
        compiler_params=_params(("arbitrary",)),
        name="swa_bias",
    )(rel_bias, sink.reshape(1, SWA_HEADS))


def _rope_rows(x, tab):
    x1, x2 = x[:ROPE_HALF], x[ROPE_HALF:]
    cos, sin = tab[:ROPE_HALF], tab[ROPE_HALF:]
    return jnp.concatenate([x1 * cos - x2 * sin, x2 * cos + x1 * sin], axis=0)


def _in_proj_kernel(x_ref, nm_ref, mod_ref, pos_ref, inv_ref, win_ref, wqvt_ref, qn_ref, wuqt_ref, kvn_ref,
                    wk_ref, wvt_ref, qt_ref, km_ref, vt_ref, qst_ref, ks_ref, vst_ref, *, tile):
    n_q = SWA_HEADS * SWA_HEAD_DIM
    rope_pad = jnp.zeros((MLA_QK_PAD - MLA_NOPE - MLA_ROPE, tile), F32)
    one_row = jnp.where(lax.broadcasted_iota(jnp.int32, rope_pad.shape, 0) == 0, 1.0, 0.0)
    mla_ones_rows = jnp.where(lax.broadcasted_iota(jnp.int32, (MLA_VT_ROWS - MLA_V, tile), 0) == 0,
                              1.0, 0.0).astype(BF16)
    swa_ones_rows = jnp.where(lax.broadcasted_iota(jnp.int32, (SWA_VT_ROWS - SWA_HEAD_DIM, BLOCK), 0) == 0,
                              1.0, 0.0).astype(BF16)

    def from_x(t):
        rows = slice(t * tile, (t + 1) * tile)
        h = (_rms(x_ref[0, rows, :]) * nm_ref[...] * (1.0 + _mod_row(mod_ref, MOD_SC1))
             + _mod_row(mod_ref, MOD_SH1)).astype(BF16)
        ang = inv_ref[...] * pos_ref[0, :, rows].astype(F32)
        tab = jnp.concatenate([jnp.cos(ang), jnp.sin(ang)], axis=0)

        lat = jnp.dot(h, win_ref[:, C_CQ:C_KS], preferred_element_type=F32)
        qv_t = lax.dot_general(wqvt_ref[...], h, NT_DIMS, preferred_element_type=F32)
        qst_ref[0, :, rows] = (qv_t[:n_q] * SWA_Q_SCALE).astype(BF16)
        for n in range(SWA_KV_HEADS):
            vs_t = qv_t[n_q + n * SWA_HEAD_DIM:n_q + (n + 1) * SWA_HEAD_DIM].astype(BF16)
            for j in range(tile // BLOCK):
                blk = t * (tile // BLOCK) + j
                vst_ref[0, blk, n, :SWA_HEAD_DIM, :] = vs_t[:, j * BLOCK:(j + 1) * BLOCK]
                vst_ref[0, blk, n, SWA_HEAD_DIM:, :] = swa_ones_rows
        ks_ref[0, rows, :] = jnp.dot(h, win_ref[:, C_KS:C_END], preferred_element_type=F32).astype(BF16)
        return lat, tab

    def from_latents(t, lat, tab):
        rows = slice(t * tile, (t + 1) * tile)
        cq = lat[:, C_CQ:C_CKV]
        ckv = lat[:, C_CKV:C_KR]
        kr_t = lat[:, C_KR:C_KS].T
        k_rope = jnp.concatenate([_rope_rows(kr_t[:MLA_ROPE], tab), one_row], axis=0).T.astype(BF16)

        cqn = (_rms(cq) * qn_ref[...]).astype(BF16)
        q_t = lax.dot_general(wuqt_ref[...], cqn, NT_DIMS, preferred_element_type=F32)
        for hh in range(MLA_HEADS):
            q0 = hh * MLA_QK_PAD
            qh = q_t[hh * (MLA_NOPE + MLA_ROPE):(hh + 1) * (MLA_NOPE + MLA_ROPE)]
            qt_ref[0, t, q0:q0 + MLA_NOPE, :] = (qh[:MLA_NOPE] * MLA_Q_SCALE).astype(BF16)
            q_rope = jnp.concatenate([_rope_rows(qh[MLA_NOPE:], tab) * MLA_Q_SCALE, rope_pad], axis=0)
            qt_ref[0, t, q0 + MLA_NOPE:q0 + MLA_QK_PAD, :] = q_rope.astype(BF16)

        ckvn = (_rms(ckv) * kvn_ref[...]).astype(BF16)
        kn = jnp.dot(ckvn, wk_ref[...], preferred_element_type=F32)
        for hh in range(MLA_HEADS):
            c0 = hh * MLA_QK_PAD
            km_ref[0, rows, c0:c0 + MLA_NOPE] = kn[:, hh * MLA_NOPE:(hh + 1) * MLA_NOPE].astype(BF16)
            km_ref[0, rows, c0 + MLA_NOPE:c0 + MLA_QK_PAD] = k_rope
        vt = lax.dot_general(wvt_ref[...], ckvn, NT_DIMS, preferred_element_type=F32)
        for hh in range(MLA_HEADS):
            v0 = hh * MLA_VT_ROWS
            vt_ref[0, t, v0:v0 + MLA_V, :] = vt[hh * MLA_V:(hh + 1) * MLA_V].astype(BF16)
            vt_ref[0, t, v0 + MLA_V:v0 + MLA_VT_ROWS, :] = mla_ones_rows

    tiles = range(x_ref.shape[1] // tile)
    staged = [from_x(t) for t in tiles]
    for t in tiles:
        from_latents(t, *staged[t])


def _in_proj(x, norm_mix, mod, positions, w_in_p, w_qv_t, q_norm, w_uq_t, kv_norm, w_k, w_v_t, tile, tiles_per_step):
    B, S, D = x.shape
    tm = tile * tiles_per_step
    inv_freq = ROPE_THETA ** (-jnp.arange(0, MLA_ROPE, 2, dtype=F32) / MLA_ROPE)
    kvw = SWA_KV_HEADS * SWA_HEAD_DIM
    tok = lambda w: pl.BlockSpec((1, tm, w), lambda b, i: (b, i, 0))
    tok_t = lambda r: pl.BlockSpec((1, r, tm), lambda b, i: (b, 0, i))
    tile_t = lambda r: pl.BlockSpec((1, tiles_per_step, r, tile), lambda b, i: (b, i, 0, 0))
    sds = lambda *shape: jax.ShapeDtypeStruct(shape, BF16)
    return pl.pallas_call(
        functools.partial(_in_proj_kernel, tile=tile),
        out_shape=[sds(B, S // tile, MLA_HEADS * MLA_QK_PAD, tile), sds(B, S, MLA_HEADS * MLA_QK_PAD),
                   sds(B, S // tile, MLA_HEADS * MLA_VT_ROWS, tile), sds(B, SWA_HEADS * SWA_HEAD_DIM, S),
                   sds(B, S, kvw), sds(B, S // BLOCK, SWA_KV_HEADS, SWA_VT_ROWS, BLOCK)],
        grid=(B, S // tm),
        in_specs=[tok(D), _resident((1, D)), _resident(mod.shape), tok_t(1), _resident((ROPE_HALF, 1)),
                  _resident(w_in_p.shape), _resident(w_qv_t.shape), _resident((1, Q_LORA)),
                  _resident(w_uq_t.shape), _resident((1, KV_LORA)), _resident(w_k.shape),
                  _resident(w_v_t.shape)],
        out_specs=[tile_t(MLA_HEADS * MLA_QK_PAD), tok(MLA_HEADS * MLA_QK_PAD), tile_t(MLA_HEADS * MLA_VT_ROWS),
                   tok_t(SWA_HEADS * SWA_HEAD_DIM), tok(kvw),
                   pl.BlockSpec((1, tm // BLOCK, SWA_KV_HEADS, SWA_VT_ROWS, BLOCK), lambda b, i: (b, i, 0, 0, 0))],
        compiler_params=_params(("arbitrary", "arbitrary")),
        name="in_proj",
    )(x, norm_mix.reshape(1, D), mod, positions.reshape(B, 1, S), inv_freq.reshape(ROPE_HALF, 1),
      w_in_p, w_qv_t, q_norm.reshape(1, Q_LORA), w_uq_t, kv_norm.reshape(1, KV_LORA), w_k, w_v_t)


def _mla_kernel(qt_ref, k_ref, vt_ref, o_ref, *, tk, tr, lookahead):
    n_tiles, _, tq = qt_ref.shape[1:]
    S = k_ref.shape[1]
    n_heads = k_ref.shape[2] // MLA_QK_PAD
    n_chunks = S // tk
    tiles_per_chunk = tk // tq
    g0 = MLA_SHIFT_COL // BF16_ROWS * BF16_ROWS
    row = lax.broadcasted_iota(jnp.int32, (BF16_ROWS, tq), 0)
    qk = lambda hd: slice(hd * MLA_QK_PAD, (hd + 1) * MLA_QK_PAD)

    def shifted_q(hd, i):
        qt = qt_ref[0, i, qk(hd), :]
        s = jnp.dot(k_ref[0, :tr, qk(hd)], qt, preferred_element_type=F32)
        ref = jnp.max(s, axis=0, keepdims=True)
        grp = jnp.where(row == MLA_SHIFT_COL - g0, -ref, qt[g0:g0 + BF16_ROWS].astype(F32)).astype(BF16)
        return jnp.concatenate([qt[:g0], grp, qt[g0 + BF16_ROWS:]], axis=0)

    def probs(item, qts):
        hd, i, c = item
        s = jnp.dot(k_ref[0, c * tk:(c + 1) * tk, qk(hd)], qts[hd, i], preferred_element_type=F32)
        return jnp.exp2(s).astype(BF16)

    def values_t(hd, c):
        rows = slice(hd * MLA_VT_ROWS, (hd + 1) * MLA_VT_ROWS)
        return jnp.concatenate([vt_ref[0, c * tiles_per_chunk + j, rows, :] for j in range(tiles_per_chunk)], axis=1)

    tiles = [(hd, i) for hd in range(n_heads) for i in range(n_tiles)]
    items = [(hd, i, c) for hd, i in tiles for c in range(n_chunks)]
    qts = {tiles[0]: shifted_q(*tiles[0])}
    ahead = [probs(item, qts) for item in items[:lookahead]]
    o_sum = jnp.zeros((MLA_V, tq), F32)
    for idx, (hd, i, c) in enumerate(items):
        t = idx // n_chunks
        if c == 0 and t + 1 < len(tiles):
            qts[tiles[t + 1]] = shifted_q(*tiles[t + 1])
        p = ahead.pop(0)
        if idx + lookahead < len(items):
            ahead.append(probs(items[idx + lookahead], qts))
        pv = jnp.dot(values_t(hd, c), p, preferred_element_type=F32)
        acc = pv if c == 0 else acc + pv
        if c == n_chunks - 1:
            o = acc[:MLA_V] / acc[MLA_V:MLA_V + 1]
            o_ref[0, i, hd * MLA_V:(hd + 1) * MLA_V, :] = o.astype(BF16)
            o_sum = o_sum + o
            qts.pop((hd, i))
    bad = jnp.max(jnp.where(jnp.isfinite(o_sum), 0.0, 1.0))

    @pl.when(bad > 0.0)
    def _recompute_with_running_max():
        for hd in range(n_heads):
            def tile_body(i, carry, hd=hd):
                qt = qt_ref[0, i, qk(hd), :]

                def key_tile_body(j, state):
                    m, l, acc = state
                    k = k_ref[0, pl.ds(pl.multiple_of(j * tq, tq), tq), qk(hd)]
                    s = jnp.dot(k, qt, preferred_element_type=F32)
                    m_new = jnp.maximum(m, jnp.max(s, axis=0, keepdims=True))
                    p = jnp.exp2(s - m_new)
                    alpha = jnp.exp2(m - m_new)
                    l = alpha * l + jnp.sum(p, axis=0, keepdims=True)
                    vt = vt_ref[0, j, hd * MLA_VT_ROWS:hd * MLA_VT_ROWS + MLA_V, :]
                    acc = alpha * acc + jnp.dot(vt, p.astype(BF16), preferred_element_type=F32)
                    return m_new, l, acc

                init = (jnp.full((1, tq), -jnp.inf, F32), jnp.zeros((1, tq), F32), jnp.zeros((MLA_V, tq), F32))
                _, l, acc = lax.fori_loop(0, n_tiles, key_tile_body, init)
                o_ref[0, i, hd * MLA_V:(hd + 1) * MLA_V, :] = (acc / l).astype(BF16)
                return carry

            lax.fori_loop(0, n_tiles, tile_body, 0)


def _mla_attn(qt_mla, k_mla, vt_mla, heads_per_step, tk, tr, lookahead):
    B, n_tiles, _, tq = qt_mla.shape
    S = k_mla.shape[1]
    hps = heads_per_step
    return pl.pallas_call(
        functools.partial(_mla_kernel, tk=tk, tr=tr, lookahead=lookahead),
        out_shape=jax.ShapeDtypeStruct((B, n_tiles, MLA_HEADS * MLA_V, tq), BF16),
        grid=(B, MLA_HEADS // hps),
        in_specs=[pl.BlockSpec((1, n_tiles, hps * MLA_QK_PAD, tq), lambda b, h: (b, 0, h, 0)),
                  pl.BlockSpec((1, S, hps * MLA_QK_PAD), lambda b, h: (b, 0, h)),
                  pl.BlockSpec((1, n_tiles, hps * MLA_VT_ROWS, tq), lambda b, h: (b, 0, h, 0))],
        out_specs=pl.BlockSpec((1, n_tiles, hps * MLA_V, tq), lambda b, h: (b, 0, h, 0)),
        compiler_params=_params(("arbitrary", "arbitrary")),
        name="mla_attn",
    )(qt_mla, k_mla, vt_mla)


def _swa_kernel(qt_ref, k_ref, vt_ref, t_ref, o_ref, *, sub_blocks, unit_heads, lookahead):
    S = k_ref.shape[1]
    span = BLOCK + 2 * WINDOW
    n_win = span // BLOCK
    step = pl.program_id(1)
    units = [(sb, hd0) for sb in range(sub_blocks) for hd0 in range(0, SWA_HEADS, unit_heads)]

    def window(sb):
        q0 = (step * sub_blocks + sb) * BLOCK
        start = pl.multiple_of(jnp.clip(q0 - WINDOW, 0, S - span), BLOCK)
        cb0 = jnp.where(q0 == 0, 2, jnp.where(q0 == S - BLOCK, 0, 1))
        return start, cb0

    def scores(sb, hd0):
        start, cb0 = window(sb)
        n = hd0 // SWA_GROUP
        heads = range(hd0, hd0 + unit_heads)
        qt = jnp.concatenate(
            [qt_ref[0, hd * SWA_HEAD_DIM:(hd + 1) * SWA_HEAD_DIM, sb * BLOCK:(sb + 1) * BLOCK] for hd in heads],
            axis=1)
        kw = k_ref[0, pl.ds(start, span), n * SWA_HEAD_DIM:(n + 1) * SWA_HEAD_DIM]
        bias = jnp.concatenate(
            [jnp.concatenate([t_ref[hd, cb0 + c] for c in range(n_win)], axis=0) for hd in heads], axis=1)
        return jnp.dot(kw, qt, preferred_element_type=F32) + bias

    def values_t(sb, hd0):
        start, _ = window(sb)
        blk0 = start // BLOCK
        return jnp.concatenate([vt_ref[0, blk0 + c, hd0 // SWA_GROUP] for c in range(n_win)], axis=1)

    def store(sb, hd0, o):
        for g in range(unit_heads):
            hd = hd0 + g
            o_ref[0, hd * SWA_HEAD_DIM:(hd + 1) * SWA_HEAD_DIM, sb * BLOCK:(sb + 1) * BLOCK] = (
                o[:, g * BLOCK:(g + 1) * BLOCK].astype(BF16))

    o_sum = jnp.zeros((SWA_HEAD_DIM, unit_heads * BLOCK), F32)
    probs = lambda unit: jnp.exp2(scores(*unit)).astype(BF16)
    ahead = [probs(unit) for unit in units[:lookahead]]
    for u, unit in enumerate(units):
        p = ahead.pop(0)
        if u + lookahead < len(units):
            ahead.append(probs(units[u + lookahead]))
        ov = jnp.dot(values_t(*unit), p, preferred_element_type=F32)
        o = ov[:SWA_HEAD_DIM] / (ov[SWA_HEAD_DIM:SWA_HEAD_DIM + 1] + 1.0)
        store(*unit, o)
        o_sum = o_sum + o
    bad = jnp.max(jnp.where(jnp.isfinite(o_sum), 0.0, 1.0))

    @pl.when(bad > 0.0)
    def _recompute_with_row_max():
        for unit in units:
            s = scores(*unit)
            m = jnp.maximum(jnp.max(s, axis=0, keepdims=True), 0.0)
            p = jnp.exp2(s - m)
            l = jnp.sum(p, axis=0, keepdims=True) + jnp.exp2(-m)
            ov = jnp.dot(values_t(*unit), p.astype(BF16), preferred_element_type=F32)
            store(*unit, ov[:SWA_HEAD_DIM] / l)


def _swa_attn(qs_t, ks, vs_t, bias_tab, sub_blocks, unit_heads, lookahead):
    B, W, S = qs_t.shape
    kvw = ks.shape[2]
    tq = sub_blocks * BLOCK
    return pl.pallas_call(
        functools.partial(_swa_kernel, sub_blocks=sub_blocks, unit_heads=unit_heads, lookahead=lookahead),
        out_shape=jax.ShapeDtypeStruct((B, W, S), BF16),
        grid=(B, S // tq),
        in_specs=[pl.BlockSpec((1, W, tq), lambda b, i: (b, 0, i)),
                  pl.BlockSpec((1, S, kvw), lambda b, i: (b, 0, 0)),
                  pl.BlockSpec((1, S // BLOCK, SWA_KV_HEADS, SWA_VT_ROWS, BLOCK), lambda b, i: (b, 0, 0, 0, 0)),
                  _resident(bias_tab.shape)],
        out_specs=pl.BlockSpec((1, W, tq), lambda b, i: (b, 0, i)),
        compiler_params=_params(("arbitrary", "arbitrary")),
        name="swa_attn",
    )(qs_t, ks, vs_t, bias_tab)


def _out_mlp_kernel(x_ref, oat_ref, obt_ref, mod_ref, nmix_ref, nmlp_ref, nfin_ref,
                    wg_ref, woa_ref, wob_ref, wout_ref, w1_ref, w2_ref, o_ref, *, tf, n_groups):
    tm, D = x_ref.shape[1:]
    groups = [slice(r, r + tm // n_groups) for r in range(0, tm, tm // n_groups)]

    def gated_merge(rows):
        y_a = lax.dot_general(oat_ref[0, 0, :, rows], woa_ref[...], TN_DIMS, preferred_element_type=F32)
        y_b = lax.dot_general(obt_ref[0, :, rows], wob_ref[...], TN_DIMS, preferred_element_type=F32)
        h = (_rms(x_ref[0, rows, :]) * nmix_ref[...] * (1.0 + _mod_row(mod_ref, MOD_SC1))
             + _mod_row(mod_ref, MOD_SH1)).astype(BF16)
        merged = None
        for j, y in enumerate((y_a, y_b)):
            g = jnp.dot(h, wg_ref[:, j * D:(j + 1) * D], preferred_element_type=F32)
            term = (0.5 * jnp.tanh(0.5 * g) + 0.5) * y
            merged = term if merged is None else merged + term
        return merged.astype(BF16)

    def residual(rows, merged):
        att = jnp.dot(merged, wout_ref[...], preferred_element_type=F32)
        return x_ref[0, rows, :] + _mod_row(mod_ref, MOD_G1) * att

    def mlp(rows, x1):
        h = (_rms(x1) * nmlp_ref[...] * (1.0 + _mod_row(mod_ref, MOD_SC2)) + _mod_row(mod_ref, MOD_SH2)).astype(BF16)
        ff = jnp.zeros_like(x1)
        for c in range(w1_ref.shape[1] // tf):
            a = jnp.dot(h, w1_ref[:, c * tf:(c + 1) * tf], preferred_element_type=F32)
            a = jnp.square(jnp.maximum(a, 0.0)).astype(BF16)
            ff = ff + jnp.dot(a, w2_ref[c * tf:(c + 1) * tf, :], preferred_element_type=F32)
        x2 = x1 + _mod_row(mod_ref, MOD_G2) * ff
        o_ref[0, rows, :] = _rms(x2) * nfin_ref[...]

    merged = [gated_merge(rows) for rows in groups]
    x1s = [residual(rows, m) for rows, m in zip(groups, merged)]
    for rows, x1 in zip(groups, x1s):
        mlp(rows, x1)


def _out_mlp(x, o_mla, o_swa, mod, norm_mix, norm_mlp, norm_final,
             w_g, w_o_mla, w_o_swa, w_out, w_ff1, w_ff2, tm, tf, n_groups):
    B, S, D = x.shape
    tok = lambda w: pl.BlockSpec((1, tm, w), lambda b, i: (b, i, 0))
    return pl.pallas_call(
        functools.partial(_out_mlp_kernel, tf=tf, n_groups=n_groups),
        out_shape=jax.ShapeDtypeStruct((B, S, D), F32),
        grid=(B, S // tm),
        in_specs=[tok(D), pl.BlockSpec((1, 1, D, tm), lambda b, i: (b, i, 0, 0)),
                  pl.BlockSpec((1, D, tm), lambda b, i: (b, 0, i)),
                  _resident(mod.shape), _resident((1, D)), _resident((1, D)), _resident((1, D)),
                  _resident(w_g.shape), _resident(w_o_mla.shape), _resident(w_o_swa.shape),
                  _resident(w_out.shape), _resident(w_ff1.shape), _resident(w_ff2.shape)],
        out_specs=tok(D),
        compiler_params=_params(("arbitrary", "arbitrary")),
        name="out_mlp",
    )(x, o_mla, o_swa, mod, norm_mix.reshape(1, D), norm_mlp.reshape(1, D), norm_final.reshape(1, D),
      w_g, w_o_mla, w_o_swa, w_out, w_ff1, w_ff2)


def _pack_w_in(w_in):
    kr1 = Q_LORA + KV_LORA + MLA_ROPE
    qs1 = kr1 + SWA_HEADS * SWA_HEAD_DIM
    ks1 = qs1 + SWA_KV_HEADS * SWA_HEAD_DIM
    vs1 = ks1 + SWA_KV_HEADS * SWA_HEAD_DIM
    pad = jnp.zeros((w_in.shape[0], LANES - MLA_ROPE), w_in.dtype)
    w_tok = jnp.concatenate([w_in[:, :kr1], pad, w_in[:, qs1:ks1]], axis=1)
    w_t = jnp.concatenate([w_in[:, kr1:qs1], w_in[:, ks1:vs1]], axis=1).T
    return w_tok.astype(BF16), w_t.astype(BF16), w_in[:, vs1:].astype(BF16)


def _split_heads(w, n_first):
    w3 = w.reshape(w.shape[0], MLA_HEADS, -1)
    first = w3[:, :, :n_first].reshape(w.shape[0], -1)
    second = w3[:, :, n_first:].reshape(w.shape[0], -1)
    return first.astype(BF16), second.astype(BF16)


def kernel(x, c, positions, w_ada, b_ada, norm_mix, w_in, q_norm, w_uq, kv_norm, w_ukv, rel_bias, sink,
           w_o_mla, w_o_swa, w_out, norm_mlp, w_ff1, w_ff2, norm_final):
    B, S, D = x.shape
    assert w_ada.shape[0] == 1, "single-layer block"
    assert D == D_MODEL and S % (4 * BLOCK) == 0 and S >= BLOCK + 2 * WINDOW

    mod = _ada_mod(c, w_ada[0], b_ada[0])
    bias_tab = _swa_bias(rel_bias, sink[0])

    tile = 512
    w_k, w_v = _split_heads(w_ukv[0], MLA_NOPE)
    w_in_tok, w_in_t, w_gate = _pack_w_in(w_in[0])
    qt_mla, k_mla, vt_mla, qs_t, ks, vs_t = _in_proj(
        x, norm_mix[0], mod, positions, w_in_tok, w_in_t, q_norm[0], w_uq[0].T.astype(BF16),
        kv_norm[0], w_k, w_v.T, tile=tile, tiles_per_step=2)

    ot_mla = _mla_attn(qt_mla, k_mla, vt_mla, heads_per_step=1, tk=1024, tr=256, lookahead=1)
    ot_swa = _swa_attn(qs_t, ks, vs_t, bias_tab, sub_blocks=8, unit_heads=2, lookahead=2)

    return _out_mlp(x, ot_mla, ot_swa, mod, norm_mix[0], norm_mlp[0], norm_final, w_gate,
                    w_o_mla[0].astype(BF16), w_o_swa[0].astype(BF16), w_out[0].astype(BF16),
                    w_ff1[0].astype(BF16), w_ff2[0].astype(BF16), tm=tile, tf=1024, n_groups=2)
```

```python
import functools
import math

import jax
import jax.numpy as jnp
from jax import lax
from jax.experimental import pallas as pl
from jax.experimental.pallas import tpu as pltpu

F32 = jnp.float32
BF16 = jnp.bfloat16

D_MODEL = 1024
MLA_HEADS = 8
MLA_NOPE = 128
MLA_ROPE = 64
MLA_V = 128
Q_LORA = 384
KV_LORA = 256
ROPE_THETA = 10000.0
SWA_HEADS = 8
SWA_KV_HEADS = 2
SWA_GROUP = SWA_HEADS // SWA_KV_HEADS
SWA_HEAD_DIM = 128
WINDOW = 128
BLOCK = 128
N_BUCKETS = 32
N_MOD = 6
EPS = 1e-6
NEG_INF = -1e30

ROPE_HALF = MLA_ROPE // 2
LANES = 128
BF16_ROWS = 16
MLA_QK_PAD = 256
MLA_SHIFT_COL = MLA_NOPE + MLA_ROPE
MLA_VT_ROWS = MLA_V + 16
SWA_VT_ROWS = SWA_HEAD_DIM + 16
VMEM_LIMIT = 56 * 1024 * 1024

LOG2_E = math.log2(math.e)
MLA_Q_SCALE = (MLA_NOPE + MLA_ROPE) ** -0.5 * LOG2_E
SWA_Q_SCALE = SWA_HEAD_DIM ** -0.5 * LOG2_E

C_CQ = 0
C_CKV = C_CQ + Q_LORA
C_KR = C_CKV + KV_LORA
C_KS = C_KR + LANES
C_END = C_KS + SWA_KV_HEADS * SWA_HEAD_DIM

T5_LARGE_THRESHOLDS = (12, 16, 23, 32, 46, 64, 91)

NT_DIMS = (((1,), (1,)), ((), ()))
TN_DIMS = (((0,), (0,)), ((), ()))


def _resident(shape):
    nd = len(shape)
    return pl.BlockSpec(shape, lambda *_: (0,) * nd, pipeline_mode=pl.Buffered(1))


def _params(semantics):
    return pltpu.CompilerParams(dimension_semantics=semantics, vmem_limit_bytes=VMEM_LIMIT)


def _rms(x):
    return x * lax.rsqrt(jnp.mean(x * x, axis=-1, keepdims=True) + EPS)


MOD_SH1, MOD_SC1, MOD_G1, MOD_SH2, MOD_SC2, MOD_G2 = range(N_MOD)


def _mod_row(mod_ref, chunk):
    return mod_ref[pl.ds(pl.program_id(0), 1), chunk * D_MODEL:(chunk + 1) * D_MODEL]


def _ada_kernel(c_ref, w_ref, b_ref, o_ref):
    c = c_ref[...]
    c_act = c * jax.nn.sigmoid(c)
    o_ref[...] = jnp.dot(c_act, w_ref[...], preferred_element_type=F32,
                         precision=lax.Precision.HIGHEST) + b_ref[...]


def _ada_mod(c, w_ada, b_ada):
    B, D = c.shape
    N = w_ada.shape[1]
    tn = 1024
    return pl.pallas_call(
        _ada_kernel,
        out_shape=jax.ShapeDtypeStruct((B, N), F32),
        grid=(N // tn,),
        in_specs=[pl.BlockSpec((B, D), lambda j: (0, 0)),
                  pl.BlockSpec((D, tn), lambda j: (0, j)),
                  pl.BlockSpec((1, tn), lambda j: (0, j))],
        out_specs=pl.BlockSpec((B, tn), lambda j: (0, j)),
        compiler_params=_params(("arbitrary",)),
        name="ada_mod",
    )(c, w_ada, b_ada.reshape(1, N))


def _swa_bias_kernel(tab_ref, sink_ref, o_ref):
    h = pl.program_id(0)
    half = N_BUCKETS // 2
    max_exact = half // 2
    kj = lax.broadcasted_iota(jnp.int32, (BLOCK, LANES), 0)
    qi = lax.broadcasted_iota(jnp.int32, (BLOCK, LANES), 1)
    for cb in range(5):
        rel = kj - qi + (cb * LANES - 2 * WINDOW)
        n = jnp.abs(rel)
        large = jnp.full_like(n, max_exact)
        for t in T5_LARGE_THRESHOLDS:
            large = large + jnp.where(n >= t, 1, 0)
        bucket = jnp.where(rel > 0, half, 0) + jnp.where(n < max_exact, n, large)
        bias = jnp.zeros((BLOCK, LANES), F32)
        for b in range(N_BUCKETS):
            bias = jnp.where(bucket == b, tab_ref[b, h], bias)
        o_ref[0, cb] = jnp.where(n <= WINDOW, (bias - sink_ref[0, h]) * LOG2_E, NEG_INF)


def _swa_bias(rel_bias, sink):
    return pl.pallas_call(
        _swa_bias_kernel,
        out_shape=jax.ShapeDtypeStruct((SWA_HEADS, 5, BLOCK, LANES), F32),
        grid=(SWA_HEADS,),
        in_specs=[pl.BlockSpec(memory_space=pltpu.SMEM), pl.BlockSpec(memory_space=pltpu.SMEM)],
        out_specs=pl.BlockSpec((1, 5, BLOCK, LANES), lambda h: (h, 0, 0, 0)),
        compiler_params=_params(("arbitrary",)),
        name="swa_bias",
    )(rel_bias, sink.reshape(1, SWA_HEADS))


def _rope_rows(x, tab):
    x1, x2 = x[:ROPE_HALF], x[ROPE_HALF:]
    cos, sin = tab[:ROPE_HALF], tab[ROPE_HALF:]
    return jnp.concatenate([x1 * cos - x2 * sin, x2 * cos + x1 * sin], axis=0)


def _in_proj_kernel(x_ref, nm_ref, mod_ref, pos_ref, inv_ref, win_ref, wqvt_ref, qn_ref, wuqt_ref, kvn_ref,
                    wk_ref, wvt_ref, qt_ref, km_ref, vt_ref, qst_ref, ks_ref, vst_ref, *, tile):
    n_q = SWA_HEADS * SWA_HEAD_DIM
    rope_pad = jnp.zeros((MLA_QK_PAD - MLA_NOPE - MLA_ROPE, tile), F32)
    one_row = jnp.where(lax.broadcasted_iota(jnp.int32, rope_pad.shape, 0) == 0, 1.0, 0.0)
    mla_ones_rows = jnp.where(lax.broadcasted_iota(jnp.int32, (MLA_VT_ROWS - MLA_V, tile), 0) == 0,
                              1.0, 0.0).astype(BF16)
    swa_ones_rows = jnp.where(lax.broadcasted_iota(jnp.int32, (SWA_VT_ROWS - SWA_HEAD_DIM, BLOCK), 0) == 0,
                              1.0, 0.0).astype(BF16)

    def from_x(t):
        rows = slice(t * tile, (t + 1) * tile)
        h = (_rms(x_ref[0, rows, :]) * nm_ref[...] * (1.0 + _mod_row(mod_ref, MOD_SC1))
             + _mod_row(mod_ref, MOD_SH1)).astype(BF16)
        ang = inv_ref[...] * pos_ref[0, :, rows].astype(F32)
        tab = jnp.concatenate([jnp.cos(ang), jnp.sin(ang)], axis=0)

        lat = jnp.dot(h, win_ref[:, C_CQ:C_KS], preferred_element_type=F32)
        qv_t = lax.dot_general(wqvt_ref[...], h, NT_DIMS, preferred_element_type=F32)
        qst_ref[0, :, rows] = (qv_t[:n_q] * SWA_Q_SCALE).astype(BF16)
        for n in range(SWA_KV_HEADS):
            vs_t = qv_t[n_q + n * SWA_HEAD_DIM:n_q + (n + 1) * SWA_HEAD_DIM].astype(BF16)
            for j in range(tile // BLOCK):
                blk = t * (tile // BLOCK) + j
                vst_ref[0, blk, n, :SWA_HEAD_DIM, :] = vs_t[:, j * BLOCK:(j + 1) * BLOCK]
                vst_ref[0, blk, n, SWA_HEAD_DIM:, :] = swa_ones_rows
        ks_ref[0, rows, :] = jnp.dot(h, win_ref[:, C_KS:C_END], preferred_element_type=F32).astype(BF16)
        return lat, tab

    def from_latents(t, lat, tab):
        rows = slice(t * tile, (t + 1) * tile)
        cq = lat[:, C_CQ:C_CKV]
        ckv = lat[:, C_CKV:C_KR]
        kr_t = lat[:, C_KR:C_KS].T
        k_rope = jnp.concatenate([_rope_rows(kr_t[:MLA_ROPE], tab), one_row], axis=0).T.astype(BF16)

        cqn = (_rms(cq) * qn_ref[...]).astype(BF16)
        q_t = lax.dot_general(wuqt_ref[...], cqn, NT_DIMS, preferred_element_type=F32)
        for hh in range(MLA_HEADS):
            q0 = hh * MLA_QK_PAD
            qh = q_t[hh * (MLA_NOPE + MLA_ROPE):(hh + 1) * (MLA_NOPE + MLA_ROPE)]
            qt_ref[0, t, q0:q0 + MLA_NOPE, :] = (qh[:MLA_NOPE] * MLA_Q_SCALE).astype(BF16)
            q_rope = jnp.concatenate([_rope_rows(qh[MLA_NOPE:], tab) * MLA_Q_SCALE, rope_pad], axis=0)
            qt_ref[0, t, q0 + MLA_NOPE:q0 + MLA_QK_PAD, :] = q_rope.astype(BF16)

        ckvn = (_rms(ckv) * kvn_ref[...]).astype(BF16)
        kn = jnp.dot(ckvn, wk_ref[...], preferred_element_type=F32)
        for hh in range(MLA_HEADS):
            c0 = hh * MLA_QK_PAD
            km_ref[0, rows, c0:c0 + MLA_NOPE] = kn[:, hh * MLA_NOPE:(hh + 1) * MLA_NOPE].astype(BF16)
            km_ref[0, rows, c0 + MLA_NOPE:c0 + MLA_QK_PAD] = k_rope
        vt = lax.dot_general(wvt_ref[...], ckvn, NT_DIMS, preferred_element_type=F32)
        for hh in range(MLA_HEADS):
            v0 = hh * MLA_VT_ROWS
            vt_ref[0, t, v0:v0 + MLA_V, :] = vt[hh * MLA_V:(hh + 1) * MLA_V].astype(BF16)
            vt_ref[0, t, v0 + MLA_V:v0 + MLA_VT_ROWS, :] = mla_ones_rows

    tiles = range(x_ref.shape[1] // tile)
    staged = [from_x(t) for t in tiles]
    for t in tiles:
        from_latents(t, *staged[t])


def _in_proj(x, norm_mix, mod, positions, w_in_p, w_qv_t, q_norm, w_uq_t, kv_norm, w_k, w_v_t, tile, tiles_per_step):
    B, S, D = x.shape
    tm = tile * tiles_per_step
    inv_freq = ROPE_THETA ** (-jnp.arange(0, MLA_ROPE, 2, dtype=F32) / MLA_ROPE)
    kvw = SWA_KV_HEADS * SWA_HEAD_DIM
    tok = lambda w: pl.BlockSpec((1, tm, w), lambda b, i: (b, i, 0))
    tok_t = lambda r: pl.BlockSpec((1, r, tm), lambda b, i: (b, 0, i))
    tile_t = lambda r: pl.BlockSpec((1, tiles_per_step, r, tile), lambda b, i: (b, i, 0, 0))
    sds = lambda *shape: jax.ShapeDtypeStruct(shape, BF16)
    return pl.pallas_call(
        functools.partial(_in_proj_kernel, tile=tile),
        out_shape=[sds(B, S // tile, MLA_HEADS * MLA_QK_PAD, tile), sds(B, S, MLA_HEADS * MLA_QK_PAD),
                   sds(B, S // tile, MLA_HEADS * MLA_VT_ROWS, tile), sds(B, SWA_HEADS * SWA_HEAD_DIM, S),
                   sds(B, S, kvw), sds(B, S // BLOCK, SWA_KV_HEADS, SWA_VT_ROWS, BLOCK)],
        grid=(B, S // tm),
        in_specs=[tok(D), _resident((1, D)), _resident(mod.shape), tok_t(1), _resident((ROPE_HALF, 1)),
                  _resident(w_in_p.shape), _resident(w_qv_t.shape), _resident((1, Q_LORA)),
                  _resident(w_uq_t.shape), _resident((1, KV_LORA)), _resident(w_k.shape),
                  _resident(w_v_t.shape)],
        out_specs=[tile_t(MLA_HEADS * MLA_QK_PAD), tok(MLA_HEADS * MLA_QK_PAD), tile_t(MLA_HEADS * MLA_VT_ROWS),
                   tok_t(SWA_HEADS * SWA_HEAD_DIM), tok(kvw),
                   pl.BlockSpec((1, tm // BLOCK, SWA_KV_HEADS, SWA_VT_ROWS, BLOCK), lambda b, i: (b, i, 0, 0, 0))],
        compiler_params=_params(("arbitrary", "arbitrary")),
        name="in_proj",
    )(x, norm_mix.reshape(1, D), mod, positions.reshape(B, 1, S), inv_freq.reshape(ROPE_HALF, 1),
      w_in_p, w_qv_t, q_norm.reshape(1, Q_LORA), w_uq_t, kv_norm.reshape(1, KV_LORA), w_k, w_v_t)


def _mla_kernel(qt_ref, k_ref, vt_ref, qt_next_ref, k_next_ref, o_ref, q0_scr, *, tk, tr, lookahead):
    n_tiles, _, tq = qt_ref.shape[1:]
    S = k_ref.shape[1]
    n_heads = k_ref.shape[2] // MLA_QK_PAD
    n_chunks = S // tk
    tiles_per_chunk = tk // tq
    g0 = MLA_SHIFT_COL // BF16_ROWS * BF16_ROWS
    row = lax.broadcasted_iota(jnp.int32, (BF16_ROWS, tq), 0)
    qk = lambda hd: slice(hd * MLA_QK_PAD, (hd + 1) * MLA_QK_PAD)

    def shift(qt, k_head):
        s = jnp.dot(k_head, qt, preferred_element_type=F32)
        ref = jnp.max(s, axis=0, keepdims=True)
        grp = jnp.where(row == MLA_SHIFT_COL - g0, -ref, qt[g0:g0 + BF16_ROWS].astype(F32)).astype(BF16)
        return jnp.concatenate([qt[:g0], grp, qt[g0 + BF16_ROWS:]], axis=0)

    def shifted_q(hd, i):
        return shift(qt_ref[0, i, qk(hd), :], k_ref[0, :tr, qk(hd)])

    @pl.when((pl.program_id(0) == 0) & (pl.program_id(1) == 0))
    def _first_step():
        q0_scr[...] = shifted_q(0, 0)

    def probs(item, qts):
        hd, i, c = item
        s = jnp.dot(k_ref[0, c * tk:(c + 1) * tk, qk(hd)], qts[hd, i], preferred_element_type=F32)
        return jnp.exp2(s).astype(BF16)

    def values_t(hd, c):
        rows = slice(hd * MLA_VT_ROWS, (hd + 1) * MLA_VT_ROWS)
        return jnp.concatenate([vt_ref[0, c * tiles_per_chunk + j, rows, :] for j in range(tiles_per_chunk)], axis=1)

    tiles = [(hd, i) for hd in range(n_heads) for i in range(n_tiles)]
    items = [(hd, i, c) for hd, i in tiles for c in range(n_chunks)]
    qts = {tiles[0]: q0_scr[...]}
    ahead = [probs(item, qts) for item in items[:lookahead]]
    o_sum = jnp.zeros((MLA_V, tq), F32)
    for idx, (hd, i, c) in enumerate(items):
        t = idx // n_chunks
        if c == 0 and t + 1 < len(tiles):
            qts[tiles[t + 1]] = shifted_q(*tiles[t + 1])
        if c == 0 and t + 1 == len(tiles):
            q0_scr[...] = shift(qt_next_ref[0, 0], k_next_ref[0])
        p = ahead.pop(0)
        if idx + lookahead < len(items):
            ahead.append(probs(items[idx + lookahead], qts))
        pv = jnp.dot(values_t(hd, c), p, preferred_element_type=F32)
        acc = pv if c == 0 else acc + pv
        if c == n_chunks - 1:
            o = acc[:MLA_V] / acc[MLA_V:MLA_V + 1]
            o_ref[0, i, hd * MLA_V:(hd + 1) * MLA_V, :] = o.astype(BF16)
            o_sum = o_sum + o
            qts.pop((hd, i))
    bad = jnp.max(jnp.where(jnp.isfinite(o_sum), 0.0, 1.0))

    @pl.when(bad > 0.0)
    def _recompute_with_running_max():
        for hd in range(n_heads):
            def tile_body(i, carry, hd=hd):
                qt = qt_ref[0, i, qk(hd), :]

                def key_tile_body(j, state):
                    m, l, acc = state
                    k = k_ref[0, pl.ds(pl.multiple_of(j * tq, tq), tq), qk(hd)]
                    s = jnp.dot(k, qt, preferred_element_type=F32)
                    m_new = jnp.maximum(m, jnp.max(s, axis=0, keepdims=True))
                    p = jnp.exp2(s - m_new)
                    alpha = jnp.exp2(m - m_new)
                    l = alpha * l + jnp.sum(p, axis=0, keepdims=True)
                    vt = vt_ref[0, j, hd * MLA_VT_ROWS:hd * MLA_VT_ROWS + MLA_V, :]
                    acc = alpha * acc + jnp.dot(vt, p.astype(BF16), preferred_element_type=F32)
                    return m_new, l, acc

                init = (jnp.full((1, tq), -jnp.inf, F32), jnp.zeros((1, tq), F32), jnp.zeros((MLA_V, tq), F32))
                _, l, acc = lax.fori_loop(0, n_tiles, key_tile_body, init)
                o_ref[0, i, hd * MLA_V:(hd + 1) * MLA_V, :] = (acc / l).astype(BF16)
                return carry

            lax.fori_loop(0, n_tiles, tile_body, 0)


def _mla_attn(qt_mla, k_mla, vt_mla, heads_per_step, tk, tr, lookahead):
    B, n_tiles, _, tq = qt_mla.shape
    S = k_mla.shape[1]
    hps = heads_per_step
    assert hps == 1, "the carried first tile assumes one head per grid step"

    def next_step(b, h):
        nxt = jnp.minimum(b * MLA_HEADS + h + 1, B * MLA_HEADS - 1)
        return nxt // MLA_HEADS, nxt % MLA_HEADS

    def next_q_tile(b, h):
        b2, h2 = next_step(b, h)
        return b2, 0, h2, 0

    def next_k_head(b, h):
        b2, h2 = next_step(b, h)
        return b2, 0, h2

    return pl.pallas_call(
        functools.partial(_mla_kernel, tk=tk, tr=tr, lookahead=lookahead),
        out_shape=jax.ShapeDtypeStruct((B, n_tiles, MLA_HEADS * MLA_V, tq), BF16),
        grid=(B, MLA_HEADS // hps),
        in_specs=[pl.BlockSpec((1, n_tiles, hps * MLA_QK_PAD, tq), lambda b, h: (b, 0, h, 0)),
                  pl.BlockSpec((1, S, hps * MLA_QK_PAD), lambda b, h: (b, 0, h)),
                  pl.BlockSpec((1, n_tiles, hps * MLA_VT_ROWS, tq), lambda b, h: (b, 0, h, 0)),
                  pl.BlockSpec((1, 1, MLA_QK_PAD, tq), next_q_tile),
                  pl.BlockSpec((1, tr, MLA_QK_PAD), next_k_head)],
        out_specs=pl.BlockSpec((1, n_tiles, hps * MLA_V, tq), lambda b, h: (b, 0, h, 0)),
        scratch_shapes=[pltpu.VMEM((MLA_QK_PAD, tq), BF16)],
        compiler_params=_params(("arbitrary", "arbitrary")),
        name="mla_attn",
    )(qt_mla, k_mla, vt_mla, qt_mla, k_mla)


def _swa_kernel(qt_ref, k_ref, vt_ref, t_ref, o_ref, *, sub_blocks, unit_heads, lookahead):
    S = k_ref.shape[1]
    span = BLOCK + 2 * WINDOW
    n_win = span // BLOCK
    step = pl.program_id(1)
    units = [(sb, hd0) for sb in range(sub_blocks) for hd0 in range(0, SWA_HEADS, unit_heads)]

    def window(sb):
        q0 = (step * sub_blocks + sb) * BLOCK
        start = pl.multiple_of(jnp.clip(q0 - WINDOW, 0, S - span), BLOCK)
        cb0 = jnp.where(q0 == 0, 2, jnp.where(q0 == S - BLOCK, 0, 1))
        return start, cb0

    def scores(sb, hd0):
        start, cb0 = window(sb)
        n = hd0 // SWA_GROUP
        heads = range(hd0, hd0 + unit_heads)
        qt = jnp.concatenate(
            [qt_ref[0, hd * SWA_HEAD_DIM:(hd + 1) * SWA_HEAD_DIM, sb * BLOCK:(sb + 1) * BLOCK] for hd in heads],
            axis=1)
        kw = k_ref[0, pl.ds(start, span), n * SWA_HEAD_DIM:(n + 1) * SWA_HEAD_DIM]
        bias = jnp.concatenate(
            [jnp.concatenate([t_ref[hd, cb0 + c] for c in range(n_win)], axis=0) for hd in heads], axis=1)
        return jnp.dot(kw, qt, preferred_element_type=F32) + bias

    def values_t(sb, hd0):
        start, _ = window(sb)
        blk0 = start // BLOCK
        return jnp.concatenate([vt_ref[0, blk0 + c, hd0 // SWA_GROUP] for c in range(n_win)], axis=1)

    def store(sb, hd0, o):
        for g in range(unit_heads):
            hd = hd0 + g
            o_ref[0, hd * SWA_HEAD_DIM:(hd + 1) * SWA_HEAD_DIM, sb * BLOCK:(sb + 1) * BLOCK] = (
                o[:, g * BLOCK:(g + 1) * BLOCK].astype(BF16))

    o_sum = jnp.zeros((SWA_HEAD_DIM, unit_heads * BLOCK), F32)
    probs = lambda unit: jnp.exp2(scores(*unit)).astype(BF16)
    ahead = [probs(unit) for unit in units[:lookahead]]
    for u, unit in enumerate(units):
        p = ahead.pop(0)
        if u + lookahead < len(units):
            ahead.append(probs(units[u + lookahead]))
        ov = jnp.dot(values_t(*unit), p, preferred_element_type=F32)
        o = ov[:SWA_HEAD_DIM] / (ov[SWA_HEAD_DIM:SWA_HEAD_DIM + 1] + 1.0)
        store(*unit, o)
        o_sum = o_sum + o
    bad = jnp.max(jnp.where(jnp.isfinite(o_sum), 0.0, 1.0))

    @pl.when(bad > 0.0)
    def _recompute_with_row_max():
        for unit in units:
            s = scores(*unit)
            m = jnp.maximum(jnp.max(s, axis=0, keepdims=True), 0.0)
            p = jnp.exp2(s - m)
            l = jnp.sum(p, axis=0, keepdims=True) + jnp.exp2(-m)
            ov = jnp.dot(values_t(*unit), p.astype(BF16), preferred_element_type=F32)
            store(*unit, ov[:SWA_HEAD_DIM] / l)


def _swa_attn(qs_t, ks, vs_t, bias_tab, sub_blocks, unit_heads, lookahead):
    B, W, S = qs_t.shape
    kvw = ks.shape[2]
    tq = sub_blocks * BLOCK
    return pl.pallas_call(
        functools.partial(_swa_kernel, sub_blocks=sub_blocks, unit_heads=unit_heads, lookahead=lookahead),
        out_shape=jax.ShapeDtypeStruct((B, W, S), BF16),
        grid=(B, S // tq),
        in_specs=[pl.BlockSpec((1, W, tq), lambda b, i: (b, 0, i)),
                  pl.BlockSpec((1, S, kvw), lambda b, i: (b, 0, 0)),
                  pl.BlockSpec((1, S // BLOCK, SWA_KV_HEADS, SWA_VT_ROWS, BLOCK), lambda b, i: (b, 0, 0, 0, 0)),
                  _resident(bias_tab.shape)],
        out_specs=pl.BlockSpec((1, W, tq), lambda b, i: (b, 0, i)),
        compiler_params=_params(("arbitrary", "arbitrary")),
        name="swa_attn",
    )(qs_t, ks, vs_t, bias_tab)


def _out_mlp_kernel(x_ref, oat_ref, obt_ref, mod_ref, nmix_ref, nmlp_ref, nfin_ref,
                    wg_ref, woa_ref, wob_ref, wout_ref, w1_ref, w2_ref, o_ref, *, tf, n_groups):
    tm, D = x_ref.shape[1:]
    groups = [slice(r, r + tm // n_groups) for r in range(0, tm, tm // n_groups)]

    def gated_merge(rows):
        y_a = lax.dot_general(oat_ref[0, 0, :, rows], woa_ref[...], TN_DIMS, preferred_element_type=F32)
        y_b = lax.dot_general(obt_ref[0, :, rows], wob_ref[...], TN_DIMS, preferred_element_type=F32)
        h = (_rms(x_ref[0, rows, :]) * nmix_ref[...] * (1.0 + _mod_row(mod_ref, MOD_SC1))
             + _mod_row(mod_ref, MOD_SH1)).astype(BF16)
        merged = None
        for j, y in enumerate((y_a, y_b)):
            g = jnp.dot(h, wg_ref[:, j * D:(j + 1) * D], preferred_element_type=F32)
            term = (0.5 * jnp.tanh(0.5 * g) + 0.5) * y
            merged = term if merged is None else merged + term
        return merged.astype(BF16)

    def residual(rows, merged):
        att = jnp.dot(merged, wout_ref[...], preferred_element_type=F32)
        return x_ref[0, rows, :] + _mod_row(mod_ref, MOD_G1) * att

    def mlp(rows, x1):
        h = (_rms(x1) * nmlp_ref[...] * (1.0 + _mod_row(mod_ref, MOD_SC2)) + _mod_row(mod_ref, MOD_SH2)).astype(BF16)
        ff = jnp.zeros_like(x1)
        for c in range(w1_ref.shape[1] // tf):
            a = jnp.dot(h, w1_ref[:, c * tf:(c + 1) * tf], preferred_element_type=F32)
            a = jnp.square(jnp.maximum(a, 0.0)).astype(BF16)
            ff = ff + jnp.dot(a, w2_ref[c * tf:(c + 1) * tf, :], preferred_element_type=F32)
        x2 = x1 + _mod_row(mod_ref, MOD_G2) * ff
        o_ref[0, rows, :] = _rms(x2) * nfin_ref[...]

    merged = [gated_merge(rows) for rows in groups]
    x1s = [residual(rows, m) for rows, m in zip(groups, merged)]
    for rows, x1 in zip(groups, x1s):
        mlp(rows, x1)


def _out_mlp(x, o_mla, o_swa, mod, norm_mix, norm_mlp, norm_final,
             w_g, w_o_mla, w_o_swa, w_out, w_ff1, w_ff2, tm, tf, n_groups):
    B, S, D = x.shape
    tok = lambda w: pl.BlockSpec((1, tm, w), lambda b, i: (b, i, 0))
    return pl.pallas_call(
        functools.partial(_out_mlp_kernel, tf=tf, n_groups=n_groups),
        out_shape=jax.ShapeDtypeStruct((B, S, D), F32),
        grid=(B, S // tm),
        in_specs=[tok(D), pl.BlockSpec((1, 1, D, tm), lambda b, i: (b, i, 0, 0)),
                  pl.BlockSpec((1, D, tm), lambda b, i: (b, 0, i)),
                  _resident(mod.shape), _resident((1, D)), _resident((1, D)), _resident((1, D)),
                  _resident(w_g.shape), _resident(w_o_mla.shape), _resident(w_o_swa.shape),
                  _resident(w_out.shape), _resident(w_ff1.shape), _resident(w_ff2.shape)],
        out_specs=tok(D),
        compiler_params=_params(("arbitrary", "arbitrary")),
        name="out_mlp",
    )(x, o_mla, o_swa, mod, norm_mix.reshape(1, D), norm_mlp.reshape(1, D), norm_final.reshape(1, D),
      w_g, w_o_mla, w_o_swa, w_out, w_ff1, w_ff2)


def _pack_w_in(w_in):
    kr1 = Q_LORA + KV_LORA + MLA_ROPE
    qs1 = kr1 + SWA_HEADS * SWA_HEAD_DIM
    ks1 = qs1 + SWA_KV_HEADS * SWA_HEAD_DIM
    vs1 = ks1 + SWA_KV_HEADS * SWA_HEAD_DIM
    pad = jnp.zeros((w_in.shape[0], LANES - MLA_ROPE), w_in.dtype)
    w_tok = jnp.concatenate([w_in[:, :kr1], pad, w_in[:, qs1:ks1]], axis=1)
    w_t = jnp.concatenate([w_in[:, kr1:qs1], w_in[:, ks1:vs1]], axis=1).T
    return w_tok.astype(BF16), w_t.astype(BF16), w_in[:, vs1:].astype(BF16)


def _split_heads(w, n_first):
    w3 = w.reshape(w.shape[0], MLA_HEADS, -1)
    first = w3[:, :, :n_first].reshape(w.shape[0], -1)
    second = w3[:, :, n_first:].reshape(w.shape[0], -1)
    return first.astype(BF16), second.astype(BF16)


def kernel(x, c, positions, w_ada, b_ada, norm_mix, w_in, q_norm, w_uq, kv_norm, w_ukv, rel_bias, sink,
           w_o_mla, w_o_swa, w_out, norm_mlp, w_ff1, w_ff2, norm_final):
    B, S, D = x.shape
    assert w_ada.shape[0] == 1, "single-layer block"
    assert D == D_MODEL and S % (4 * BLOCK) == 0 and S >= BLOCK + 2 * WINDOW

    mod = _ada_mod(c, w_ada[0], b_ada[0])
    bias_tab = _swa_bias(rel_bias, sink[0])

    tile = 512
    w_k, w_v = _split_heads(w_ukv[0], MLA_NOPE)
    w_in_tok, w_in_t, w_gate = _pack_w_in(w_in[0])
    qt_mla, k_mla, vt_mla, qs_t, ks, vs_t = _in_proj(
        x, norm_mix[0], mod, positions, w_in_tok, w_in_t, q_norm[0], w_uq[0].T.astype(BF16),
        kv_norm[0], w_k, w_v.T, tile=tile, tiles_per_step=2)

    ot_mla = _mla_attn(qt_mla, k_mla, vt_mla, heads_per_step=1, tk=1024, tr=256, lookahead=1)
    ot_swa = _swa_attn(qs_t, ks, vs_t, bias_tab, sub_blocks=8, unit_heads=2, lookahead=2)

    return _out_mlp(x, ot_mla, ot_swa, mod, norm_mix[0], norm_mlp[0], norm_final, w_gate,
                    w_o_mla[0].astype(BF16), w_o_swa[0].astype(BF16), w_out[0].astype(BF16),
                    w_ff1[0].astype(BF16), w_ff2[0].astype(BF16), tm=tile, tf=1024, n_groups=2)
```

```python
import functools
import math

import jax
import jax.numpy as jnp
from jax import lax
from jax.experimental import pallas as pl
from jax.experimental.pallas import tpu as pltpu

F32 = jnp.float32
BF16 = jnp.bfloat16

D_MODEL = 1024
MLA_HEADS = 8
MLA_NOPE = 128
MLA_ROPE = 64
MLA_V = 128
Q_LORA = 384
KV_LORA = 256
ROPE_THETA = 10000.0
SWA_HEADS = 8
SWA_KV_HEADS = 2
SWA_GROUP = SWA_HEADS // SWA_KV_HEADS
SWA_HEAD_DIM = 128
WINDOW = 128
BLOCK = 128
N_BUCKETS = 32
N_MOD = 6
EPS = 1e-6
NEG_INF = -1e30

ROPE_HALF = MLA_ROPE // 2
LANES = 128
BF16_ROWS = 16
MLA_QK_PAD = 256
MLA_SHIFT_COL = MLA_NOPE + MLA_ROPE
MLA_VT_ROWS = MLA_V + 16
SWA_VT_ROWS = SWA_HEAD_DIM + 16
VMEM_LIMIT = 56 * 1024 * 1024

LOG2_E = math.log2(math.e)
MLA_Q_SCALE = (MLA_NOPE + MLA_ROPE) ** -0.5 * LOG2_E
SWA_Q_SCALE = SWA_HEAD_DIM ** -0.5 * LOG2_E

C_CQ = 0
C_CKV = C_CQ + Q_LORA
C_KR = C_CKV + KV_LORA
C_KS = C_KR + LANES
C_END = C_KS + SWA_KV_HEADS * SWA_HEAD_DIM

T5_LARGE_THRESHOLDS = (12, 16, 23, 32, 46, 64, 91)

NT_DIMS = (((1,), (1,)), ((), ()))
TN_DIMS = (((0,), (0,)), ((), ()))


def _resident(shape):
    nd = len(shape)
    return pl.BlockSpec(shape, lambda *_: (0,) * nd, pipeline_mode=pl.Buffered(1))


def _params(semantics):
    return pltpu.CompilerParams(dimension_semantics=semantics, vmem_limit_bytes=VMEM_LIMIT)


def _rms(x):
    return x * lax.rsqrt(jnp.mean(x * x, axis=-1, keepdims=True) + EPS)


MOD_SH1, MOD_SC1, MOD_G1, MOD_SH2, MOD_SC2, MOD_G2 = range(N_MOD)


def _mod_row(mod_ref, chunk):
    return mod_ref[pl.ds(pl.program_id(0), 1), chunk * D_MODEL:(chunk + 1) * D_MODEL]


def _ada_kernel(c_ref, w_ref, b_ref, o_ref):
    c = c_ref[...]
    c_act = c * jax.nn.sigmoid(c)
    o_ref[...] = jnp.dot(c_act, w_ref[...], preferred_element_type=F32,
                         precision=lax.Precision.HIGHEST) + b_ref[...]


def _ada_mod(c, w_ada, b_ada):
    B, D = c.shape
    N = w_ada.shape[1]
    tn = 1024
    return pl.pallas_call(
        _ada_kernel,
        out_shape=jax.ShapeDtypeStruct((B, N), F32),
        grid=(N // tn,),
        in_specs=[pl.BlockSpec((B, D), lambda j: (0, 0)),
                  pl.BlockSpec((D, tn), lambda j: (0, j)),
                  pl.BlockSpec((1, tn), lambda j: (0, j))],
        out_specs=pl.BlockSpec((B, tn), lambda j: (0, j)),
        compiler_params=_params(("arbitrary",)),
        name="ada_mod",
    )(c, w_ada, b_ada.reshape(1, N))


def _swa_bias_kernel(tab_ref, sink_ref, o_ref):
    h = pl.program_id(0)
    half = N_BUCKETS // 2
    max_exact = half // 2
    kj = lax.broadcasted_iota(jnp.int32, (BLOCK, LANES), 0)
    qi = lax.broadcasted_iota(jnp.int32, (BLOCK, LANES), 1)
    for cb in range(5):
        rel = kj - qi + (cb * LANES - 2 * WINDOW)
        n = jnp.abs(rel)
        large = jnp.full_like(n, max_exact)
        for t in T5_LARGE_THRESHOLDS:
            large = large + jnp.where(n >= t, 1, 0)
        bucket = jnp.where(rel > 0, half, 0) + jnp.where(n < max_exact, n, large)
        bias = jnp.zeros((BLOCK, LANES), F32)
        for b in range(N_BUCKETS):
            bias = jnp.where(bucket == b, tab_ref[b, h], bias)
        o_ref[0, cb] = jnp.where(n <= WINDOW, (bias - sink_ref[0, h]) * LOG2_E, NEG_INF)


def _swa_bias(rel_bias, sink):
    return pl.pallas_call(
        _swa_bias_kernel,
        out_shape=jax.ShapeDtypeStruct((SWA_HEADS, 5, BLOCK, LANES), F32),
        grid=(SWA_HEADS,),
        in_specs=[pl.BlockSpec(memory_space=pltpu.SMEM), pl.BlockSpec(memory_space=pltpu.SMEM)],
        out_specs=pl.BlockSpec((1, 5, BLOCK, LANES), lambda h: (h, 0, 0, 0)),
        compiler_params=_params(("arbitrary",)),
        name="swa_bias",
    )(rel_bias, sink.reshape(1, SWA_HEADS))


def _rope_rows(x, tab):
    x1, x2 = x[:ROPE_HALF], x[ROPE_HALF:]
    cos, sin = tab[:ROPE_HALF], tab[ROPE_HALF:]
    return jnp.concatenate([x1 * cos - x2 * sin, x2 * cos + x1 * sin], axis=0)


def _in_proj_kernel(x_ref, nm_ref, mod_ref, pos_ref, inv_ref, win_ref, wqvt_ref, qn_ref, wuqt_ref, kvn_ref,
                    wk_ref, wvt_ref, qt_ref, km_ref, vt_ref, qst_ref, ks_ref, vst_ref, *, tile):
    n_q = SWA_HEADS * SWA_HEAD_DIM
    rope_pad = jnp.zeros((MLA_QK_PAD - MLA_NOPE - MLA_ROPE, tile), F32)
    one_row = jnp.where(lax.broadcasted_iota(jnp.int32, rope_pad.shape, 0) == 0, 1.0, 0.0)
    mla_ones_rows = jnp.where(lax.broadcasted_iota(jnp.int32, (MLA_VT_ROWS - MLA_V, tile), 0) == 0,
                              1.0, 0.0).astype(BF16)
    swa_ones_rows = jnp.where(lax.broadcasted_iota(jnp.int32, (SWA_VT_ROWS - SWA_HEAD_DIM, BLOCK), 0) == 0,
                              1.0, 0.0).astype(BF16)

    def from_x(t):
        rows = slice(t * tile, (t + 1) * tile)
        h = (_rms(x_ref[0, rows, :]) * nm_ref[...] * (1.0 + _mod_row(mod_ref, MOD_SC1))
             + _mod_row(mod_ref, MOD_SH1)).astype(BF16)
        ang = inv_ref[...] * pos_ref[0, :, rows].astype(F32)
        tab = jnp.concatenate([jnp.cos(ang), jnp.sin(ang)], axis=0)

        lat = jnp.dot(h, win_ref[:, C_CQ:C_KS], preferred_element_type=F32)
        qv_t = lax.dot_general(wqvt_ref[...], h, NT_DIMS, preferred_element_type=F32)
        qst_ref[0, :, rows] = (qv_t[:n_q] * SWA_Q_SCALE).astype(BF16)
        for n in range(SWA_KV_HEADS):
            vs_t = qv_t[n_q + n * SWA_HEAD_DIM:n_q + (n + 1) * SWA_HEAD_DIM].astype(BF16)
            for j in range(tile // BLOCK):
                blk = t * (tile // BLOCK) + j
                vst_ref[0, blk, n, :SWA_HEAD_DIM, :] = vs_t[:, j * BLOCK:(j + 1) * BLOCK]
                vst_ref[0, blk, n, SWA_HEAD_DIM:, :] = swa_ones_rows
        ks_ref[0, rows, :] = jnp.dot(h, win_ref[:, C_KS:C_END], preferred_element_type=F32).astype(BF16)
        return lat, tab

    def from_latents(t, lat, tab):
        rows = slice(t * tile, (t + 1) * tile)
        cq = lat[:, C_CQ:C_CKV]
        ckv = lat[:, C_CKV:C_KR]
        kr_t = lat[:, C_KR:C_KS].T
        k_rope = jnp.concatenate([_rope_rows(kr_t[:MLA_ROPE], tab), one_row], axis=0).T.astype(BF16)

        cqn = (_rms(cq) * qn_ref[...]).astype(BF16)
        q_t = lax.dot_general(wuqt_ref[...], cqn, NT_DIMS, preferred_element_type=F32)
        for hh in range(MLA_HEADS):
            q0 = hh * MLA_QK_PAD
            qh = q_t[hh * (MLA_NOPE + MLA_ROPE):(hh + 1) * (MLA_NOPE + MLA_ROPE)]
            qt_ref[0, t, q0:q0 + MLA_NOPE, :] = (qh[:MLA_NOPE] * MLA_Q_SCALE).astype(BF16)
            q_rope = jnp.concatenate([_rope_rows(qh[MLA_NOPE:], tab) * MLA_Q_SCALE, rope_pad], axis=0)
            qt_ref[0, t, q0 + MLA_NOPE:q0 + MLA_QK_PAD, :] = q_rope.astype(BF16)

        ckvn = (_rms(ckv) * kvn_ref[...]).astype(BF16)
        kn = jnp.dot(ckvn, wk_ref[...], preferred_element_type=F32)
        for hh in range(MLA_HEADS):
            c0 = hh * MLA_QK_PAD
            km_ref[0, rows, c0:c0 + MLA_NOPE] = kn[:, hh * MLA_NOPE:(hh + 1) * MLA_NOPE].astype(BF16)
            km_ref[0, rows, c0 + MLA_NOPE:c0 + MLA_QK_PAD] = k_rope
        vt = lax.dot_general(wvt_ref[...], ckvn, NT_DIMS, preferred_element_type=F32)
        for hh in range(MLA_HEADS):
            v0 = hh * MLA_VT_ROWS
            vt_ref[0, t, v0:v0 + MLA_V, :] = vt[hh * MLA_V:(hh + 1) * MLA_V].astype(BF16)
            vt_ref[0, t, v0 + MLA_V:v0 + MLA_VT_ROWS, :] = mla_ones_rows

    tiles = range(x_ref.shape[1] // tile)
    staged = [from_x(t) for t in tiles]
    for t in tiles:
        from_latents(t, *staged[t])


def _in_proj(x, norm_mix, mod, positions, w_in_p, w_qv_t, q_norm, w_uq_t, kv_norm, w_k, w_v_t, tile, tiles_per_step):
    B, S, D = x.shape
    tm = tile * tiles_per_step
    inv_freq = ROPE_THETA ** (-jnp.arange(0, MLA_ROPE, 2, dtype=F32) / MLA_ROPE)
    kvw = SWA_KV_HEADS * SWA_HEAD_DIM
    tok = lambda w: pl.BlockSpec((1, tm, w), lambda b, i: (b, i, 0))
    tok_t = lambda r: pl.BlockSpec((1, r, tm), lambda b, i: (b, 0, i))
    tile_t = lambda r: pl.BlockSpec((1, tiles_per_step, r, tile), lambda b, i: (b, i, 0, 0))
    sds = lambda *shape: jax.ShapeDtypeStruct(shape, BF16)
    return pl.pallas_call(
        functools.partial(_in_proj_kernel, tile=tile),
        out_shape=[sds(B, S // tile, MLA_HEADS * MLA_QK_PAD, tile), sds(B, S, MLA_HEADS * MLA_QK_PAD),
                   sds(B, S // tile, MLA_HEADS * MLA_VT_ROWS, tile), sds(B, SWA_HEADS * SWA_HEAD_DIM, S),
                   sds(B, S, kvw), sds(B, S // BLOCK, SWA_KV_HEADS, SWA_VT_ROWS, BLOCK)],
        grid=(B, S // tm),
        in_specs=[tok(D), _resident((1, D)), _resident(mod.shape), tok_t(1), _resident((ROPE_HALF, 1)),
                  _resident(w_in_p.shape), _resident(w_qv_t.shape), _resident((1, Q_LORA)),
                  _resident(w_uq_t.shape), _resident((1, KV_LORA)), _resident(w_k.shape),
                  _resident(w_v_t.shape)],
        out_specs=[tile_t(MLA_HEADS * MLA_QK_PAD), tok(MLA_HEADS * MLA_QK_PAD), tile_t(MLA_HEADS * MLA_VT_ROWS),
                   tok_t(SWA_HEADS * SWA_HEAD_DIM), tok(kvw),
                   pl.BlockSpec((1, tm // BLOCK, SWA_KV_HEADS, SWA_VT_ROWS, BLOCK), lambda b, i: (b, i, 0, 0, 0))],
        compiler_params=_params(("arbitrary", "arbitrary")),
        name="in_proj",
    )(x, norm_mix.reshape(1, D), mod, positions.reshape(B, 1, S), inv_freq.reshape(ROPE_HALF, 1),
      w_in_p, w_qv_t, q_norm.reshape(1, Q_LORA), w_uq_t, kv_norm.reshape(1, KV_LORA), w_k, w_v_t)


def _mla_kernel(qt_ref, k_ref, vt_ref, o_ref, *, tk, tr, lookahead):
    n_tiles, _, tq = qt_ref.shape[1:]
    S = k_ref.shape[1]
    n_heads = k_ref.shape[2] // MLA_QK_PAD
    n_chunks = S // tk
    tiles_per_chunk = tk // tq
    g0 = MLA_SHIFT_COL // BF16_ROWS * BF16_ROWS
    row = lax.broadcasted_iota(jnp.int32, (BF16_ROWS, tq), 0)
    qk = lambda hd: slice(hd * MLA_QK_PAD, (hd + 1) * MLA_QK_PAD)

    def shifted_q(hd, i):
        qt = qt_ref[0, i, qk(hd), :]
        s = jnp.dot(k_ref[0, :tr, qk(hd)], qt, preferred_element_type=F32)
        ref = jnp.max(s, axis=0, keepdims=True)
        grp = jnp.where(row == MLA_SHIFT_COL - g0, -ref, qt[g0:g0 + BF16_ROWS].astype(F32)).astype(BF16)
        return jnp.concatenate([qt[:g0], grp, qt[g0 + BF16_ROWS:]], axis=0)

    def probs(item, qts):
        hd, i, c = item
        s = jnp.dot(k_ref[0, c * tk:(c + 1) * tk, qk(hd)], qts[hd, i], preferred_element_type=F32)
        return jnp.exp2(s).astype(BF16)

    def values_t(hd, c):
        rows = slice(hd * MLA_VT_ROWS, (hd + 1) * MLA_VT_ROWS)
        return jnp.concatenate([vt_ref[0, c * tiles_per_chunk + j, rows, :] for j in range(tiles_per_chunk)], axis=1)

    tiles = [(hd, i) for hd in range(n_heads) for i in range(n_tiles)]
    items = [(hd, i, c) for hd, i in tiles for c in range(n_chunks)]
    qts = {tiles[0]: shifted_q(*tiles[0])}
    ahead = [probs(item, qts) for item in items[:lookahead]]
    o_sum = jnp.zeros((MLA_V, tq), F32)
    for idx, (hd, i, c) in enumerate(items):
        t = idx // n_chunks
        if c == 0 and t + 1 < len(tiles):
            qts[tiles[t + 1]] = shifted_q(*tiles[t + 1])
        p = ahead.pop(0)
        if idx + lookahead < len(items):
            ahead.append(probs(items[idx + lookahead], qts))
        pv = jnp.dot(values_t(hd, c), p, preferred_element_type=F32)
        acc = pv if c == 0 else acc + pv
        if c == n_chunks - 1:
            o = acc[:MLA_V] / acc[MLA_V:MLA_V + 1]
            o_ref[0, i, hd * MLA_V:(hd + 1) * MLA_V, :] = o.astype(BF16)
            o_sum = o_sum + o
            qts.pop((hd, i))
    bad = jnp.max(jnp.where(jnp.isfinite(o_sum), 0.0, 1.0))

    @pl.when(bad > 0.0)
    def _recompute_with_running_max():
        for hd in range(n_heads):
            def tile_body(i, carry, hd=hd):
                qt = qt_ref[0, i, qk(hd), :]

                def key_tile_body(j, state):
                    m, l, acc = state
                    k = k_ref[0, pl.ds(pl.multiple_of(j * tq, tq), tq), qk(hd)]
                    s = jnp.dot(k, qt, preferred_element_type=F32)
                    m_new = jnp.maximum(m, jnp.max(s, axis=0, keepdims=True))
                    p = jnp.exp2(s - m_new)
                    alpha = jnp.exp2(m - m_new)
                    l = alpha * l + jnp.sum(p, axis=0, keepdims=True)
                    vt = vt_ref[0, j, hd * MLA_VT_ROWS:hd * MLA_VT_ROWS + MLA_V, :]
                    acc = alpha * acc + jnp.dot(vt, p.astype(BF16), preferred_element_type=F32)
                    return m_new, l, acc

                init = (jnp.full((1, tq), -jnp.inf, F32), jnp.zeros((1, tq), F32), jnp.zeros((MLA_V, tq), F32))
                _, l, acc = lax.fori_loop(0, n_tiles, key_tile_body, init)
                o_ref[0, i, hd * MLA_V:(hd + 1) * MLA_V, :] = (acc / l).astype(BF16)
                return carry

            lax.fori_loop(0, n_tiles, tile_body, 0)


def _mla_attn(qt_mla, k_mla, vt_mla, heads_per_step, tk, tr, lookahead):
    B, n_tiles, _, tq = qt_mla.shape
    S = k_mla.shape[1]
    hps = heads_per_step
    return pl.pallas_call(
        functools.partial(_mla_kernel, tk=tk, tr=tr, lookahead=lookahead),
        out_shape=jax.ShapeDtypeStruct((B, n_tiles, MLA_HEADS * MLA_V, tq), BF16),
        grid=(B, MLA_HEADS // hps),
        in_specs=[pl.BlockSpec((1, n_tiles, hps * MLA_QK_PAD, tq), lambda b, h: (b, 0, h, 0)),
                  pl.BlockSpec((1, S, hps * MLA_QK_PAD), lambda b, h: (b, 0, h)),
                  pl.BlockSpec((1, n_tiles, hps * MLA_VT_ROWS, tq), lambda b, h: (b, 0, h, 0))],
        out_specs=pl.BlockSpec((1, n_tiles, hps * MLA_V, tq), lambda b, h: (b, 0, h, 0)),
        compiler_params=_params(("arbitrary", "arbitrary")),
        name="mla_attn",
    )(qt_mla, k_mla, vt_mla)


def _swa_kernel(qt_ref, k_ref, vt_ref, t_ref, o_ref, *, sub_blocks, unit_heads, lookahead):
    S = k_ref.shape[1]
    span = BLOCK + 2 * WINDOW
    n_win = span // BLOCK
    step = pl.program_id(1)
    units = [(sb, hd0) for sb in range(sub_blocks) for hd0 in range(0, SWA_HEADS, unit_heads)]

    def window(sb):
        q0 = (step * sub_blocks + sb) * BLOCK
        start = pl.multiple_of(jnp.clip(q0 - WINDOW, 0, S - span), BLOCK)
        cb0 = jnp.where(q0 == 0, 2, jnp.where(q0 == S - BLOCK, 0, 1))
        return start, cb0

    def scores(sb, hd0):
        start, cb0 = window(sb)
        n = hd0 // SWA_GROUP
        heads = range(hd0, hd0 + unit_heads)
        qt = jnp.concatenate(
            [qt_ref[0, hd * SWA_HEAD_DIM:(hd + 1) * SWA_HEAD_DIM, sb * BLOCK:(sb + 1) * BLOCK] for hd in heads],
            axis=1)
        kw = k_ref[0, pl.ds(start, span), n * SWA_HEAD_DIM:(n + 1) * SWA_HEAD_DIM]
        bias = jnp.concatenate(
            [jnp.concatenate([t_ref[hd, cb0 + c] for c in range(n_win)], axis=0) for hd in heads], axis=1)
        return jnp.dot(kw, qt, preferred_element_type=F32) + bias

    def values_t(sb, hd0):
        start, _ = window(sb)
        blk0 = start // BLOCK
        return jnp.concatenate([vt_ref[0, blk0 + c, hd0 // SWA_GROUP] for c in range(n_win)], axis=1)

    def store(sb, hd0, o):
        for g in range(unit_heads):
            hd = hd0 + g
            o_ref[0, hd * SWA_HEAD_DIM:(hd + 1) * SWA_HEAD_DIM, sb * BLOCK:(sb + 1) * BLOCK] = (
                o[:, g * BLOCK:(g + 1) * BLOCK].astype(BF16))

    o_sum = jnp.zeros((SWA_HEAD_DIM, unit_heads * BLOCK), F32)
    probs = lambda unit: jnp.exp2(scores(*unit)).astype(BF16)
    ahead = [probs(unit) for unit in units[:lookahead]]
    for u, unit in enumerate(units):
        p = ahead.pop(0)
        if u + lookahead < len(units):
            ahead.append(probs(units[u + lookahead]))
        ov = jnp.dot(values_t(*unit), p, preferred_element_type=F32)
        o = ov[:SWA_HEAD_DIM] / (ov[SWA_HEAD_DIM:SWA_HEAD_DIM + 1] + 1.0)
        store(*unit, o)
        o_sum = o_sum + o
    bad = jnp.max(jnp.where(jnp.isfinite(o_sum), 0.0, 1.0))

    @pl.when(bad > 0.0)
    def _recompute_with_row_max():
        for unit in units:
            s = scores(*unit)
            m = jnp.maximum(jnp.max(s, axis=0, keepdims=True), 0.0)
            p = jnp.exp2(s - m)
            l = jnp.sum(p, axis=0, keepdims=True) + jnp.exp2(-m)
            ov = jnp.dot(values_t(*unit), p.astype(BF16), preferred_element_type=F32)
            store(*unit, ov[:SWA_HEAD_DIM] / l)


def _swa_attn(qs_t, ks, vs_t, bias_tab, sub_blocks, unit_heads, lookahead):
    B, W, S = qs_t.shape
    kvw = ks.shape[2]
    tq = sub_blocks * BLOCK
    return pl.pallas_call(
        functools.partial(_swa_kernel, sub_blocks=sub_blocks, unit_heads=unit_heads, lookahead=lookahead),
        out_shape=jax.ShapeDtypeStruct((B, W, S), BF16),
        grid=(B, S // tq),
        in_specs=[pl.BlockSpec((1, W, tq), lambda b, i: (b, 0, i)),
                  pl.BlockSpec((1, S, kvw), lambda b, i: (b, 0, 0)),
                  pl.BlockSpec((1, S // BLOCK, SWA_KV_HEADS, SWA_VT_ROWS, BLOCK), lambda b, i: (b, 0, 0, 0, 0)),
                  _resident(bias_tab.shape)],
        out_specs=pl.BlockSpec((1, W, tq), lambda b, i: (b, 0, i)),
        compiler_params=_params(("arbitrary", "arbitrary")),
        name="swa_attn",
    )(qs_t, ks, vs_t, bias_tab)


def _out_mlp_kernel(x_ref, oat_ref, obt_ref, mod_ref, nmix_ref, nmlp_ref, nfin_ref,
                    wg_ref, woa_ref, wob_ref, wout_ref, w1_ref, w2_ref, o_ref, *, tf, n_groups):
    tm, D = x_ref.shape[1:]
    groups = [slice(r, r + tm // n_groups) for r in range(0, tm, tm // n_groups)]

    def gated_merge(rows):
        y_a = lax.dot_general(oat_ref[0, 0, :, rows], woa_ref[...], TN_DIMS, preferred_element_type=F32)
        y_b = lax.dot_general(obt_ref[0, :, rows], wob_ref[...], TN_DIMS, preferred_element_type=F32)
        h = (_rms(x_ref[0, rows, :]) * nmix_ref[...] * (1.0 + _mod_row(mod_ref, MOD_SC1))
             + _mod_row(mod_ref, MOD_SH1)).astype(BF16)
        merged = None
        for j, y in enumerate((y_a, y_b)):
            g = jnp.dot(h, wg_ref[:, j * D:(j + 1) * D], preferred_element_type=F32)
            term = (0.5 * jnp.tanh(0.5 * g) + 0.5) * y
            merged = term if merged is None else merged + term
        return merged.astype(BF16)

    def residual(rows, merged):
        att = jnp.dot(merged, wout_ref[...], preferred_element_type=F32)
        return x_ref[0, rows, :] + _mod_row(mod_ref, MOD_G1) * att

    def mlp(rows, x1):
        h = (_rms(x1) * nmlp_ref[...] * (1.0 + _mod_row(mod_ref, MOD_SC2)) + _mod_row(mod_ref, MOD_SH2)).astype(BF16)
        ff = jnp.zeros_like(x1)
        for c in range(w1_ref.shape[1] // tf):
            a = jnp.dot(h, w1_ref[:, c * tf:(c + 1) * tf], preferred_element_type=F32)
            a = jnp.square(jnp.maximum(a, 0.0)).astype(BF16)
            ff = ff + jnp.dot(a, w2_ref[c * tf:(c + 1) * tf, :], preferred_element_type=F32)
        x2 = x1 + _mod_row(mod_ref, MOD_G2) * ff
        o_ref[0, rows, :] = _rms(x2) * nfin_ref[...]

    merged = [gated_merge(rows) for rows in groups]
    x1s = [residual(rows, m) for rows, m in zip(groups, merged)]
    for rows, x1 in zip(groups, x1s):
        mlp(rows, x1)


def _out_mlp(x, o_mla, o_swa, mod, norm_mix, norm_mlp, norm_final,
             w_g, w_o_mla, w_o_swa, w_out, w_ff1, w_ff2, tm, tf, n_groups):
    B, S, D = x.shape
    tok = lambda w: pl.BlockSpec((1, tm, w), lambda b, i: (b, i, 0))
    return pl.pallas_call(
        functools.partial(_out_mlp_kernel, tf=tf, n_groups=n_groups),
        out_shape=jax.ShapeDtypeStruct((B, S, D), F32),
        grid=(B, S // tm),
        in_specs=[tok(D), pl.BlockSpec((1, 1, D, tm), lambda b, i: (b, i, 0, 0)),
                  pl.BlockSpec((1, D, tm), lambda b, i: (b, 0, i)),
                  _resident(mod.shape), _resident((1, D)), _resident((1, D)), _resident((1, D)),
                  _resident(w_g.shape), _resident(w_o_mla.shape), _resident(w_o_swa.shape),
                  _resident(w_out.shape), _resident(w_ff1.shape), _resident(w_ff2.shape)],
        out_specs=tok(D),
        compiler_params=_params(("arbitrary", "arbitrary")),
        name="out_mlp",
    )(x, o_mla, o_swa, mod, norm_mix.reshape(1, D), norm_mlp.reshape(1, D), norm_final.reshape(1, D),
      w_g, w_o_mla, w_o_swa, w_out, w_ff1, w_ff2)


def _pack_w_in_kernel(w_ref, tok_ref, t_ref, gate_ref):
    kr0 = Q_LORA + KV_LORA
    kr1 = kr0 + MLA_ROPE
    n_q = SWA_HEADS * SWA_HEAD_DIM
    n_kv = SWA_KV_HEADS * SWA_HEAD_DIM
    qs1 = kr1 + n_q
    ks1 = qs1 + n_kv
    vs1 = ks1 + n_kv
    tok_ref[:, :kr0] = w_ref[:, :kr0].astype(BF16)
    lane = lax.broadcasted_iota(jnp.int32, (w_ref.shape[0], LANES), 1)
    tok_ref[:, kr0:kr0 + LANES] = jnp.where(lane < MLA_ROPE, w_ref[:, kr0:kr0 + LANES], 0.0).astype(BF16)
    a1 = -(-vs1 // LANES) * LANES
    wt = w_ref[:, kr0:a1].T
    t_ref[:n_q, :] = wt[kr1 - kr0:qs1 - kr0].astype(BF16)
    t_ref[n_q:, :] = wt[ks1 - kr0:vs1 - kr0].astype(BF16)
    tok_ref[:, kr0 + LANES:] = wt[qs1 - kr0:ks1 - kr0].T.astype(BF16)
    gate_ref[...] = w_ref[:, vs1:].astype(BF16)


def _pack_w_in(w_in, rows=256):
    D, n_in = w_in.shape
    n_t = (SWA_HEADS + SWA_KV_HEADS) * SWA_HEAD_DIM
    n_gate = n_in - (Q_LORA + KV_LORA + MLA_ROPE + n_t + SWA_KV_HEADS * SWA_HEAD_DIM)
    return pl.pallas_call(
        _pack_w_in_kernel,
        out_shape=(jax.ShapeDtypeStruct((D, C_END), BF16), jax.ShapeDtypeStruct((n_t, D), BF16),
                   jax.ShapeDtypeStruct((D, n_gate), BF16)),
        grid=(D // rows,),
        in_specs=[pl.BlockSpec((rows, n_in), lambda i: (i, 0))],
        out_specs=(pl.BlockSpec((rows, C_END), lambda i: (i, 0)), pl.BlockSpec((n_t, rows), lambda i: (0, i)),
                   pl.BlockSpec((rows, n_gate), lambda i: (i, 0))),
        compiler_params=_params(("arbitrary",)),
        name="pack_w_in",
    )(w_in)


def _split_heads(w, n_first):
    w3 = w.reshape(w.shape[0], MLA_HEADS, -1)
    first = w3[:, :, :n_first].reshape(w.shape[0], -1)
    second = w3[:, :, n_first:].reshape(w.shape[0], -1)
    return first.astype(BF16), second.astype(BF16)


def kernel(x, c, positions, w_ada, b_ada, norm_mix, w_in, q_norm, w_uq, kv_norm, w_ukv, rel_bias, sink,
           w_o_mla, w_o_swa, w_out, norm_mlp, w_ff1, w_ff2, norm_final):
    B, S, D = x.shape
    assert w_ada.shape[0] == 1, "single-layer block"
    assert D == D_MODEL and S % (4 * BLOCK) == 0 and S >= BLOCK + 2 * WINDOW

    mod = _ada_mod(c, w_ada[0], b_ada[0])
    bias_tab = _swa_bias(rel_bias, sink[0])

    tile = 512
    w_k, w_v = _split_heads(w_ukv[0], MLA_NOPE)
    w_in_tok, w_in_t, w_gate = _pack_w_in(w_in[0])
    qt_mla, k_mla, vt_mla, qs_t, ks, vs_t = _in_proj(
        x, norm_mix[0], mod, positions, w_in_tok, w_in_t, q_norm[0], w_uq[0].T.astype(BF16),
        kv_norm[0], w_k, w_v.T, tile=tile, tiles_per_step=2)

    ot_mla = _mla_attn(qt_mla, k_mla, vt_mla, heads_per_step=1, tk=1024, tr=256, lookahead=1)
    ot_swa = _swa_attn(qs_t, ks, vs_t, bias_tab, sub_blocks=8, unit_heads=2, lookahead=2)

    return _out_mlp(x, ot_mla, ot_swa, mod, norm_mix[0], norm_mlp[0], norm_final, w_gate,
                    w_o_mla[0].astype(BF16), w_o_swa[0].astype(BF16), w_out[0].astype(BF16),
                    w_ff1[0].astype(BF16), w_ff2[0].astype(BF16), tm=tile, tf=1024, n_groups=2)
```

```python
import functools
import math

import jax
import jax.numpy as jnp
from jax import lax
from jax.experimental import pallas as pl
from jax.experimental.pallas import tpu as pltpu

F32 = jnp.float32
BF16 = jnp.bfloat16

D_MODEL = 1024
MLA_HEADS = 8
MLA_NOPE = 128
MLA_ROPE = 64
MLA_V = 128
Q_LORA = 384
KV_LORA = 256
ROPE_THETA = 10000.0
SWA_HEADS = 8
SWA_KV_HEADS = 2
SWA_GROUP = SWA_HEADS // SWA_KV_HEADS
SWA_HEAD_DIM = 128
WINDOW = 128
BLOCK = 128
N_BUCKETS = 32
N_MOD = 6
EPS = 1e-6
NEG_INF = -1e30

ROPE_HALF = MLA_ROPE // 2
LANES = 128
BF16_ROWS = 16
MLA_QK_PAD = 256
MLA_SHIFT_COL = MLA_NOPE + MLA_ROPE
MLA_VT_ROWS = MLA_V + 16
SWA_VT_ROWS = SWA_HEAD_DIM + 16
VMEM_LIMIT = 56 * 1024 * 1024

LOG2_E = math.log2(math.e)
MLA_Q_SCALE = (MLA_NOPE + MLA_ROPE) ** -0.5 * LOG2_E
SWA_Q_SCALE = SWA_HEAD_DIM ** -0.5 * LOG2_E

C_CQ = 0
C_CKV = C_CQ + Q_LORA
C_KR = C_CKV + KV_LORA
C_KS = C_KR + LANES
C_END = C_KS + SWA_KV_HEADS * SWA_HEAD_DIM

T5_LARGE_THRESHOLDS = (12, 16, 23, 32, 46, 64, 91)

NT_DIMS = (((1,), (1,)), ((), ()))
TN_DIMS = (((0,), (0,)), ((), ()))


def _resident(shape):
    nd = len(shape)
    return pl.BlockSpec(shape, lambda *_: (0,) * nd, pipeline_mode=pl.Buffered(1))


def _params(semantics):
    return pltpu.CompilerParams(dimension_semantics=semantics, vmem_limit_bytes=VMEM_LIMIT)


def _rms(x):
    return x * lax.rsqrt(jnp.mean(x * x, axis=-1, keepdims=True) + EPS)


MOD_SH1, MOD_SC1, MOD_G1, MOD_SH2, MOD_SC2, MOD_G2 = range(N_MOD)


def _mod_row(mod_ref, chunk):
    return mod_ref[pl.ds(pl.program_id(0), 1), chunk * D_MODEL:(chunk + 1) * D_MODEL]


def _ada_kernel(c_ref, w_ref, b_ref, o_ref):
    c = c_ref[...]
    c_act = c * jax.nn.sigmoid(c)
    o_ref[...] = jnp.dot(c_act, w_ref[...], preferred_element_type=F32,
                         precision=lax.Precision.HIGHEST) + b_ref[...]


def _ada_mod(c, w_ada, b_ada):
    B, D = c.shape
    N = w_ada.shape[1]
    tn = 1024
    return pl.pallas_call(
        _ada_kernel,
        out_shape=jax.ShapeDtypeStruct((B, N), F32),
        grid=(N // tn,),
        in_specs=[pl.BlockSpec((B, D), lambda j: (0, 0)),
                  pl.BlockSpec((D, tn), lambda j: (0, j)),
                  pl.BlockSpec((1, tn), lambda j: (0, j))],
        out_specs=pl.BlockSpec((B, tn), lambda j: (0, j)),
        compiler_params=_params(("arbitrary",)),
        name="ada_mod",
    )(c, w_ada, b_ada.reshape(1, N))


def _swa_bias_kernel(tab_ref, sink_ref, o_ref):
    h = pl.program_id(0)
    half = N_BUCKETS // 2
    max_exact = half // 2
    kj = lax.broadcasted_iota(jnp.int32, (BLOCK, LANES), 0)
    qi = lax.broadcasted_iota(jnp.int32, (BLOCK, LANES), 1)
    for cb in range(5):
        rel = kj - qi + (cb * LANES - 2 * WINDOW)
        n = jnp.abs(rel)
        large = jnp.full_like(n, max_exact)
        for t in T5_LARGE_THRESHOLDS:
            large = large + jnp.where(n >= t, 1, 0)
        bucket = jnp.where(rel > 0, half, 0) + jnp.where(n < max_exact, n, large)
        bias = jnp.zeros((BLOCK, LANES), F32)
        for b in range(N_BUCKETS):
            bias = jnp.where(bucket == b, tab_ref[b, h], bias)
        o_ref[0, cb] = jnp.where(n <= WINDOW, (bias - sink_ref[0, h]) * LOG2_E, NEG_INF)


def _swa_bias(rel_bias, sink):
    return pl.pallas_call(
        _swa_bias_kernel,
        out_shape=jax.ShapeDtypeStruct((SWA_HEADS, 5, BLOCK, LANES), F32),
        grid=(SWA_HEADS,),
        in_specs=[pl.BlockSpec(memory_space=pltpu.SMEM), pl.BlockSpec(memory_space=pltpu.SMEM)],
        out_specs=pl.BlockSpec((1, 5, BLOCK, LANES), lambda h: (h, 0, 0, 0)),
        compiler_params=_params(("arbitrary",)),
        name="swa_bias",
    )(rel_bias, sink.reshape(1, SWA_HEADS))


def _rope_rows(x, tab):
    x1, x2 = x[:ROPE_HALF], x[ROPE_HALF:]
    cos, sin = tab[:ROPE_HALF], tab[ROPE_HALF:]
    return jnp.concatenate([x1 * cos - x2 * sin, x2 * cos + x1 * sin], axis=0)


def _in_proj_kernel(x_ref, nm_ref, mod_ref, pos_ref, inv_ref, win_ref, wqvt_ref, qn_ref, wuqt_ref, kvn_ref,
                    wk_ref, wvt_ref, qt_ref, km_ref, vt_ref, qst_ref, ks_ref, vst_ref, *, tile):
    n_q = SWA_HEADS * SWA_HEAD_DIM
    rope_pad = jnp.zeros((MLA_QK_PAD - MLA_NOPE - MLA_ROPE, tile), F32)
    one_row = jnp.where(lax.broadcasted_iota(jnp.int32, rope_pad.shape, 0) == 0, 1.0, 0.0)
    mla_ones_rows = jnp.where(lax.broadcasted_iota(jnp.int32, (MLA_VT_ROWS - MLA_V, tile), 0) == 0,
                              1.0, 0.0).astype(BF16)
    swa_ones_rows = jnp.where(lax.broadcasted_iota(jnp.int32, (SWA_VT_ROWS - SWA_HEAD_DIM, BLOCK), 0) == 0,
                              1.0, 0.0).astype(BF16)

    def from_x(t):
        rows = slice(t * tile, (t + 1) * tile)
        h = (_rms(x_ref[0, rows, :]) * nm_ref[...] * (1.0 + _mod_row(mod_ref, MOD_SC1))
             + _mod_row(mod_ref, MOD_SH1)).astype(BF16)
        ang = inv_ref[...] * pos_ref[0, :, rows].astype(F32)
        tab = jnp.concatenate([jnp.cos(ang), jnp.sin(ang)], axis=0)

        lat = jnp.dot(h, win_ref[:, C_CQ:C_KS], preferred_element_type=F32)
        qv_t = lax.dot_general(wqvt_ref[...], h, NT_DIMS, preferred_element_type=F32)
        qst_ref[0, :, rows] = (qv_t[:n_q] * SWA_Q_SCALE).astype(BF16)
        for n in range(SWA_KV_HEADS):
            vs_t = qv_t[n_q + n * SWA_HEAD_DIM:n_q + (n + 1) * SWA_HEAD_DIM].astype(BF16)
            for j in range(tile // BLOCK):
                blk = t * (tile // BLOCK) + j
                vst_ref[0, blk, n, :SWA_HEAD_DIM, :] = vs_t[:, j * BLOCK:(j + 1) * BLOCK]
                vst_ref[0, blk, n, SWA_HEAD_DIM:, :] = swa_ones_rows
        ks_ref[0, rows, :] = jnp.dot(h, win_ref[:, C_KS:C_END], preferred_element_type=F32).astype(BF16)
        return lat, tab

    def from_latents(t, lat, tab):
        rows = slice(t * tile, (t + 1) * tile)
        cq = lat[:, C_CQ:C_CKV]
        ckv = lat[:, C_CKV:C_KR]
        kr_t = lat[:, C_KR:C_KS].T
        k_rope = jnp.concatenate([_rope_rows(kr_t[:MLA_ROPE], tab), one_row], axis=0).T.astype(BF16)

        cqn = (_rms(cq) * qn_ref[...]).astype(BF16)
        q_t = lax.dot_general(wuqt_ref[...], cqn, NT_DIMS, preferred_element_type=F32)
        for hh in range(MLA_HEADS):
            q0 = hh * MLA_QK_PAD
            qh = q_t[hh * (MLA_NOPE + MLA_ROPE):(hh + 1) * (MLA_NOPE + MLA_ROPE)]
            qt_ref[0, t, q0:q0 + MLA_NOPE, :] = (qh[:MLA_NOPE] * MLA_Q_SCALE).astype(BF16)
            q_rope = jnp.concatenate([_rope_rows(qh[MLA_NOPE:], tab) * MLA_Q_SCALE, rope_pad], axis=0)
            qt_ref[0, t, q0 + MLA_NOPE:q0 + MLA_QK_PAD, :] = q_rope.astype(BF16)

        ckvn = (_rms(ckv) * kvn_ref[...]).astype(BF16)
        kn = jnp.dot(ckvn, wk_ref[...], preferred_element_type=F32)
        for hh in range(MLA_HEADS):
            c0 = hh * MLA_QK_PAD
            km_ref[0, rows, c0:c0 + MLA_NOPE] = kn[:, hh * MLA_NOPE:(hh + 1) * MLA_NOPE].astype(BF16)
            km_ref[0, rows, c0 + MLA_NOPE:c0 + MLA_QK_PAD] = k_rope
        vt = lax.dot_general(wvt_ref[...], ckvn, NT_DIMS, preferred_element_type=F32)
        for hh in range(MLA_HEADS):
            v0 = hh * MLA_VT_ROWS
            vt_ref[0, t, v0:v0 + MLA_V, :] = vt[hh * MLA_V:(hh + 1) * MLA_V].astype(BF16)
            vt_ref[0, t, v0 + MLA_V:v0 + MLA_VT_ROWS, :] = mla_ones_rows

    tiles = range(x_ref.shape[1] // tile)
    staged = [from_x(t) for t in tiles]
    for t in tiles:
        from_latents(t, *staged[t])


def _in_proj(x, norm_mix, mod, positions, w_in_p, w_qv_t, q_norm, w_uq_t, kv_norm, w_k, w_v_t, tile, tiles_per_step):
    B, S, D = x.shape
    tm = tile * tiles_per_step
    inv_freq = ROPE_THETA ** (-jnp.arange(0, MLA_ROPE, 2, dtype=F32) / MLA_ROPE)
    kvw = SWA_KV_HEADS * SWA_HEAD_DIM
    tok = lambda w: pl.BlockSpec((1, tm, w), lambda b, i: (b, i, 0))
    tok_t = lambda r: pl.BlockSpec((1, r, tm), lambda b, i: (b, 0, i))
    tile_t = lambda r: pl.BlockSpec((1, tiles_per_step, r, tile), lambda b, i: (b, i, 0, 0))
    sds = lambda *shape: jax.ShapeDtypeStruct(shape, BF16)
    return pl.pallas_call(
        functools.partial(_in_proj_kernel, tile=tile),
        out_shape=[sds(B, S // tile, MLA_HEADS * MLA_QK_PAD, tile), sds(B, S, MLA_HEADS * MLA_QK_PAD),
                   sds(B, S // tile, MLA_HEADS * MLA_VT_ROWS, tile), sds(B, SWA_HEADS * SWA_HEAD_DIM, S),
                   sds(B, S, kvw), sds(B, S // BLOCK, SWA_KV_HEADS, SWA_VT_ROWS, BLOCK)],
        grid=(B, S // tm),
        in_specs=[tok(D), _resident((1, D)), _resident(mod.shape), tok_t(1), _resident((ROPE_HALF, 1)),
                  _resident(w_in_p.shape), _resident(w_qv_t.shape), _resident((1, Q_LORA)),
                  _resident(w_uq_t.shape), _resident((1, KV_LORA)), _resident(w_k.shape),
                  _resident(w_v_t.shape)],
        out_specs=[tile_t(MLA_HEADS * MLA_QK_PAD), tok(MLA_HEADS * MLA_QK_PAD), tile_t(MLA_HEADS * MLA_VT_ROWS),
                   tok_t(SWA_HEADS * SWA_HEAD_DIM), tok(kvw),
                   pl.BlockSpec((1, tm // BLOCK, SWA_KV_HEADS, SWA_VT_ROWS, BLOCK), lambda b, i: (b, i, 0, 0, 0))],
        compiler_params=_params(("arbitrary", "arbitrary")),
        name="in_proj",
    )(x, norm_mix.reshape(1, D), mod, positions.reshape(B, 1, S), inv_freq.reshape(ROPE_HALF, 1),
      w_in_p, w_qv_t, q_norm.reshape(1, Q_LORA), w_uq_t, kv_norm.reshape(1, KV_LORA), w_k, w_v_t)


def _mla_kernel(qt_ref, k_ref, vt_ref, o_ref, *, tk, tr, lookahead):
    n_tiles, _, tq = qt_ref.shape[1:]
    S = k_ref.shape[1]
    n_heads = k_ref.shape[2] // MLA_QK_PAD
    n_chunks = S // tk
    tiles_per_chunk = tk // tq
    g0 = MLA_SHIFT_COL // BF16_ROWS * BF16_ROWS
    row = lax.broadcasted_iota(jnp.int32, (BF16_ROWS, tq), 0)
    qk = lambda hd: slice(hd * MLA_QK_PAD, (hd + 1) * MLA_QK_PAD)

    def shifted_q(hd, i):
        qt = qt_ref[0, i, qk(hd), :]
        s = jnp.dot(k_ref[0, :tr, qk(hd)], qt, preferred_element_type=F32)
        ref = jnp.max(s, axis=0, keepdims=True)
        grp = jnp.where(row == MLA_SHIFT_COL - g0, -ref, qt[g0:g0 + BF16_ROWS].astype(F32)).astype(BF16)
        return jnp.concatenate([qt[:g0], grp, qt[g0 + BF16_ROWS:]], axis=0)

    def probs(item, qts):
        hd, i, c = item
        s = jnp.dot(k_ref[0, c * tk:(c + 1) * tk, qk(hd)], qts[hd, i], preferred_element_type=F32)
        return jnp.exp2(s).astype(BF16)

    def values_t(hd, c):
        rows = slice(hd * MLA_VT_ROWS, (hd + 1) * MLA_VT_ROWS)
        return jnp.concatenate([vt_ref[0, c * tiles_per_chunk + j, rows, :] for j in range(tiles_per_chunk)], axis=1)

    tiles = [(hd, i) for hd in range(n_heads) for i in range(n_tiles)]
    items = [(hd, i, c) for hd, i in tiles for c in range(n_chunks)]
    qts = {tiles[0]: shifted_q(*tiles[0])}
    ahead = [probs(item, qts) for item in items[:lookahead]]
    o_sum = jnp.zeros((MLA_V, tq), F32)
    for idx, (hd, i, c) in enumerate(items):
        t = idx // n_chunks
        if c == 0 and t + 1 < len(tiles):
            qts[tiles[t + 1]] = shifted_q(*tiles[t + 1])
        p = ahead.pop(0)
        if idx + lookahead < len(items):
            ahead.append(probs(items[idx + lookahead], qts))
        pv = jnp.dot(values_t(hd, c), p, preferred_element_type=F32)
        acc = pv if c == 0 else acc + pv
        if c == n_chunks - 1:
            o = acc[:MLA_V] / acc[MLA_V:MLA_V + 1]
            o_ref[0, i, hd * MLA_V:(hd + 1) * MLA_V, :] = o.astype(BF16)
            o_sum = o_sum + o
            qts.pop((hd, i))
    bad = jnp.max(jnp.where(jnp.isfinite(o_sum), 0.0, 1.0))

    @pl.when(bad > 0.0)
    def _recompute_with_running_max():
        for hd in range(n_heads):
            def tile_body(i, carry, hd=hd):
                qt = qt_ref[0, i, qk(hd), :]

                def key_tile_body(j, state):
                    m, l, acc = state
                    k = k_ref[0, pl.ds(pl.multiple_of(j * tq, tq), tq), qk(hd)]
                    s = jnp.dot(k, qt, preferred_element_type=F32)
                    m_new = jnp.maximum(m, jnp.max(s, axis=0, keepdims=True))
                    p = jnp.exp2(s - m_new)
                    alpha = jnp.exp2(m - m_new)
                    l = alpha * l + jnp.sum(p, axis=0, keepdims=True)
                    vt = vt_ref[0, j, hd * MLA_VT_ROWS:hd * MLA_VT_ROWS + MLA_V, :]
                    acc = alpha * acc + jnp.dot(vt, p.astype(BF16), preferred_element_type=F32)
                    return m_new, l, acc

                init = (jnp.full((1, tq), -jnp.inf, F32), jnp.zeros((1, tq), F32), jnp.zeros((MLA_V, tq), F32))
                _, l, acc = lax.fori_loop(0, n_tiles, key_tile_body, init)
                o_ref[0, i, hd * MLA_V:(hd + 1) * MLA_V, :] = (acc / l).astype(BF16)
                return carry

            lax.fori_loop(0, n_tiles, tile_body, 0)


def _mla_attn(qt_mla, k_mla, vt_mla, heads_per_step, tk, tr, lookahead):
    B, n_tiles, _, tq = qt_mla.shape
    S = k_mla.shape[1]
    hps = heads_per_step
    return pl.pallas_call(
        functools.partial(_mla_kernel, tk=tk, tr=tr, lookahead=lookahead),
        out_shape=jax.ShapeDtypeStruct((B, n_tiles, MLA_HEADS * MLA_V, tq), BF16),
        grid=(B, MLA_HEADS // hps),
        in_specs=[pl.BlockSpec((1, n_tiles, hps * MLA_QK_PAD, tq), lambda b, h: (b, 0, h, 0)),
                  pl.BlockSpec((1, S, hps * MLA_QK_PAD), lambda b, h: (b, 0, h)),
                  pl.BlockSpec((1, n_tiles, hps * MLA_VT_ROWS, tq), lambda b, h: (b, 0, h, 0))],
        out_specs=pl.BlockSpec((1, n_tiles, hps * MLA_V, tq), lambda b, h: (b, 0, h, 0)),
        compiler_params=_params(("arbitrary", "arbitrary")),
        name="mla_attn",
    )(qt_mla, k_mla, vt_mla)


def _swa_kernel(qt_ref, k_ref, vt_ref, t_ref, o_ref, *, sub_blocks, unit_heads, lookahead):
    S = k_ref.shape[1]
    span = BLOCK + 2 * WINDOW
    n_win = span // BLOCK
    step = pl.program_id(1)
    units = [(sb, hd0) for sb in range(sub_blocks) for hd0 in range(0, SWA_HEADS, unit_heads)]

    def window(sb):
        q0 = (step * sub_blocks + sb) * BLOCK
        start = pl.multiple_of(jnp.clip(q0 - WINDOW, 0, S - span), BLOCK)
        cb0 = jnp.where(q0 == 0, 2, jnp.where(q0 == S - BLOCK, 0, 1))
        return start, cb0

    def scores(sb, hd0):
        start, cb0 = window(sb)
        n = hd0 // SWA_GROUP
        heads = range(hd0, hd0 + unit_heads)
        qt = jnp.concatenate(
            [qt_ref[0, hd * SWA_HEAD_DIM:(hd + 1) * SWA_HEAD_DIM, sb * BLOCK:(sb + 1) * BLOCK] for hd in heads],
            axis=1)
        kw = k_ref[0, pl.ds(start, span), n * SWA_HEAD_DIM:(n + 1) * SWA_HEAD_DIM]
        bias = jnp.concatenate(
            [jnp.concatenate([t_ref[hd, cb0 + c] for c in range(n_win)], axis=0) for hd in heads], axis=1)
        return jnp.dot(kw, qt, preferred_element_type=F32) + bias

    def values_t(sb, hd0):
        start, _ = window(sb)
        blk0 = start // BLOCK
        return jnp.concatenate([vt_ref[0, blk0 + c, hd0 // SWA_GROUP] for c in range(n_win)], axis=1)

    def store(sb, hd0, o):
        for g in range(unit_heads):
            hd = hd0 + g
            o_ref[0, hd * SWA_HEAD_DIM:(hd + 1) * SWA_HEAD_DIM, sb * BLOCK:(sb + 1) * BLOCK] = (
                o[:, g * BLOCK:(g + 1) * BLOCK].astype(BF16))

    o_sum = jnp.zeros((SWA_HEAD_DIM, unit_heads * BLOCK), F32)
    probs = lambda unit: jnp.exp2(scores(*unit)).astype(BF16)
    ahead = [probs(unit) for unit in units[:lookahead]]
    for u, unit in enumerate(units):
        p = ahead.pop(0)
        if u + lookahead < len(units):
            ahead.append(probs(units[u + lookahead]))
        ov = jnp.dot(values_t(*unit), p, preferred_element_type=F32)
        o = ov[:SWA_HEAD_DIM] / (ov[SWA_HEAD_DIM:SWA_HEAD_DIM + 1] + 1.0)
        store(*unit, o)
        o_sum = o_sum + o
    bad = jnp.max(jnp.where(jnp.isfinite(o_sum), 0.0, 1.0))

    @pl.when(bad > 0.0)
    def _recompute_with_row_max():
        for unit in units:
            s = scores(*unit)
            m = jnp.maximum(jnp.max(s, axis=0, keepdims=True), 0.0)
            p = jnp.exp2(s - m)
            l = jnp.sum(p, axis=0, keepdims=True) + jnp.exp2(-m)
            ov = jnp.dot(values_t(*unit), p.astype(BF16), preferred_element_type=F32)
            store(*unit, ov[:SWA_HEAD_DIM] / l)


def _swa_attn(qs_t, ks, vs_t, bias_tab, sub_blocks, unit_heads, lookahead):
    B, W, S = qs_t.shape
    kvw = ks.shape[2]
    tq = sub_blocks * BLOCK
    return pl.pallas_call(
        functools.partial(_swa_kernel, sub_blocks=sub_blocks, unit_heads=unit_heads, lookahead=lookahead),
        out_shape=jax.ShapeDtypeStruct((B, W, S), BF16),
        grid=(B, S // tq),
        in_specs=[pl.BlockSpec((1, W, tq), lambda b, i: (b, 0, i)),
                  pl.BlockSpec((1, S, kvw), lambda b, i: (b, 0, 0)),
                  pl.BlockSpec((1, S // BLOCK, SWA_KV_HEADS, SWA_VT_ROWS, BLOCK), lambda b, i: (b, 0, 0, 0, 0)),
                  _resident(bias_tab.shape)],
        out_specs=pl.BlockSpec((1, W, tq), lambda b, i: (b, 0, i)),
        compiler_params=_params(("arbitrary", "arbitrary")),
        name="swa_attn",
    )(qs_t, ks, vs_t, bias_tab)


def _out_mlp_kernel(x_ref, oat_ref, obt_ref, mod_ref, nmix_ref, nmlp_ref, nfin_ref,
                    wg_ref, woa_ref, wob_ref, wout_ref, w1_ref, w2_ref, o_ref, *, tf, n_groups):
    tm, D = x_ref.shape[1:]
    groups = [slice(r, r + tm // n_groups) for r in range(0, tm, tm // n_groups)]

    def gated_merge(rows):
        y_a = lax.dot_general(oat_ref[0, 0, :, rows], woa_ref[...], TN_DIMS, preferred_element_type=F32)
        y_b = lax.dot_general(obt_ref[0, :, rows], wob_ref[...], TN_DIMS, preferred_element_type=F32)
        h = (_rms(x_ref[0, rows, :]) * nmix_ref[...] * (1.0 + _mod_row(mod_ref, MOD_SC1))
             + _mod_row(mod_ref, MOD_SH1)).astype(BF16)
        merged = None
        for j, y in enumerate((y_a, y_b)):
            g = jnp.dot(h, wg_ref[:, j * D:(j + 1) * D], preferred_element_type=F32)
            term = (0.5 * jnp.tanh(0.5 * g) + 0.5) * y
            merged = term if merged is None else merged + term
        return merged.astype(BF16)

    def residual(rows, merged):
        att = jnp.dot(merged, wout_ref[...], preferred_element_type=F32)
        return x_ref[0, rows, :] + _mod_row(mod_ref, MOD_G1) * att

    def mlp(rows, x1):
        h = (_rms(x1) * nmlp_ref[...] * (1.0 + _mod_row(mod_ref, MOD_SC2)) + _mod_row(mod_ref, MOD_SH2)).astype(BF16)
        ff = jnp.zeros_like(x1)
        for c in range(w1_ref.shape[1] // tf):
            a = jnp.dot(h, w1_ref[:, c * tf:(c + 1) * tf], preferred_element_type=F32)
            a = jnp.square(jnp.maximum(a, 0.0)).astype(BF16)
            ff = ff + jnp.dot(a, w2_ref[c * tf:(c + 1) * tf, :], preferred_element_type=F32)
        x2 = x1 + _mod_row(mod_ref, MOD_G2) * ff
        o_ref[0, rows, :] = _rms(x2) * nfin_ref[...]

    merged = [gated_merge(rows) for rows in groups]
    x1s = [residual(rows, m) for rows, m in zip(groups, merged)]
    for rows, x1 in zip(groups, x1s):
        mlp(rows, x1)


def _out_mlp(x, o_mla, o_swa, mod, norm_mix, norm_mlp, norm_final,
             w_g, w_o_mla, w_o_swa, w_out, w_ff1, w_ff2, tm, tf, n_groups):
    B, S, D = x.shape
    tok = lambda w: pl.BlockSpec((1, tm, w), lambda b, i: (b, i, 0))
    return pl.pallas_call(
        functools.partial(_out_mlp_kernel, tf=tf, n_groups=n_groups),
        out_shape=jax.ShapeDtypeStruct((B, S, D), F32),
        grid=(B, S // tm),
        in_specs=[tok(D), pl.BlockSpec((1, 1, D, tm), lambda b, i: (b, i, 0, 0)),
                  pl.BlockSpec((1, D, tm), lambda b, i: (b, 0, i)),
                  _resident(mod.shape), _resident((1, D)), _resident((1, D)), _resident((1, D)),
                  _resident(w_g.shape), _resident(w_o_mla.shape), _resident(w_o_swa.shape),
                  _resident(w_out.shape), _resident(w_ff1.shape), _resident(w_ff2.shape)],
        out_specs=tok(D),
        compiler_params=_params(("arbitrary", "arbitrary")),
        name="out_mlp",
    )(x, o_mla, o_swa, mod, norm_mix.reshape(1, D), norm_mlp.reshape(1, D), norm_final.reshape(1, D),
      w_g, w_o_mla, w_o_swa, w_out, w_ff1, w_ff2)


def _pack_w_in_kernel(wt_ref, tok_ref, t_ref, gate_ref):
    kr0 = Q_LORA + KV_LORA
    kr1 = kr0 + MLA_ROPE
    n_q = SWA_HEADS * SWA_HEAD_DIM
    n_kv = SWA_KV_HEADS * SWA_HEAD_DIM
    qs1 = kr1 + n_q
    ks1 = qs1 + n_kv
    vs1 = ks1 + n_kv
    t_ref[:n_q, :] = wt_ref[kr1:qs1, :].astype(BF16)
    t_ref[n_q:, :] = wt_ref[ks1:vs1, :].astype(BF16)
    tok_ref[:, :kr0] = wt_ref[:kr0, :].T.astype(BF16)
    kr = wt_ref[kr0:kr0 + LANES, :].T
    lane = lax.broadcasted_iota(jnp.int32, kr.shape, 1)
    tok_ref[:, kr0:kr0 + LANES] = jnp.where(lane < MLA_ROPE, kr, 0.0).astype(BF16)
    tok_ref[:, kr0 + LANES:] = wt_ref[qs1:ks1, :].T.astype(BF16)
    gate_ref[...] = wt_ref[vs1:, :].T.astype(BF16)


def _pack_w_in(w_in_t, cols=256):
    n_in, D = w_in_t.shape
    n_t = (SWA_HEADS + SWA_KV_HEADS) * SWA_HEAD_DIM
    n_gate = n_in - (Q_LORA + KV_LORA + MLA_ROPE + n_t + SWA_KV_HEADS * SWA_HEAD_DIM)
    return pl.pallas_call(
        _pack_w_in_kernel,
        out_shape=(jax.ShapeDtypeStruct((D, C_END), BF16), jax.ShapeDtypeStruct((n_t, D), BF16),
                   jax.ShapeDtypeStruct((D, n_gate), BF16)),
        grid=(D // cols,),
        in_specs=[pl.BlockSpec((n_in, cols), lambda i: (0, i))],
        out_specs=(pl.BlockSpec((cols, C_END), lambda i: (i, 0)), pl.BlockSpec((n_t, cols), lambda i: (0, i)),
                   pl.BlockSpec((cols, n_gate), lambda i: (i, 0))),
        compiler_params=_params(("arbitrary",)),
        name="pack_w_in",
    )(w_in_t)


def _split_heads(w, n_first):
    w3 = w.reshape(w.shape[0], MLA_HEADS, -1)
    first = w3[:, :, :n_first].reshape(w.shape[0], -1)
    second = w3[:, :, n_first:].reshape(w.shape[0], -1)
    return first.astype(BF16), second.astype(BF16)


def kernel(x, c, positions, w_ada, b_ada, norm_mix, w_in, q_norm, w_uq, kv_norm, w_ukv, rel_bias, sink,
           w_o_mla, w_o_swa, w_out, norm_mlp, w_ff1, w_ff2, norm_final):
    B, S, D = x.shape
    assert w_ada.shape[0] == 1, "single-layer block"
    assert D == D_MODEL and S % (4 * BLOCK) == 0 and S >= BLOCK + 2 * WINDOW

    mod = _ada_mod(c, w_ada[0], b_ada[0])
    bias_tab = _swa_bias(rel_bias, sink[0])

    tile = 512
    w_k, w_v = _split_heads(w_ukv[0], MLA_NOPE)
    w_in_tok, w_in_t, w_gate = _pack_w_in(w_in[0].T)
    qt_mla, k_mla, vt_mla, qs_t, ks, vs_t = _in_proj(
        x, norm_mix[0], mod, positions, w_in_tok, w_in_t, q_norm[0], w_uq[0].T.astype(BF16),
        kv_norm[0], w_k, w_v.T, tile=tile, tiles_per_step=2)

    ot_mla = _mla_attn(qt_mla, k_mla, vt_mla, heads_per_step=1, tk=1024, tr=256, lookahead=1)
    ot_swa = _swa_attn(qs_t, ks, vs_t, bias_tab, sub_blocks=8, unit_heads=2, lookahead=2)

    return _out_mlp(x, ot_mla, ot_swa, mod, norm_mix[0], norm_mlp[0], norm_final, w_gate,
                    w_o_mla[0].astype(BF16), w_o_swa[0].astype(BF16), w_out[0].astype(BF16),
                    w_ff1[0].astype(BF16), w_ff2[0].astype(BF16), tm=tile, tf=1024, n_groups=2)
```

```python
import functools
import math

import jax
import jax.numpy as jnp
from jax import lax
from jax.experimental import pallas as pl
from jax.experimental.pallas import tpu as pltpu

F32 = jnp.float32
BF16 = jnp.bfloat16

D_MODEL = 1024
MLA_HEADS = 8
MLA_NOPE = 128
MLA_ROPE = 64
MLA_V = 128
Q_LORA = 384
KV_LORA = 256
ROPE_THETA = 10000.0
SWA_HEADS = 8
SWA_KV_HEADS = 2
SWA_GROUP = SWA_HEADS // SWA_KV_HEADS
SWA_HEAD_DIM = 128
WINDOW = 128
BLOCK = 128
N_BUCKETS = 32
N_MOD = 6
EPS = 1e-6
NEG_INF = -1e30

ROPE_HALF = MLA_ROPE // 2
LANES = 128
BF16_ROWS = 16
MLA_QK_PAD = 256
MLA_SHIFT_COL = MLA_NOPE + MLA_ROPE
MLA_VT_ROWS = MLA_V + 16
SWA_VT_ROWS = SWA_HEAD_DIM + 16
VMEM_LIMIT = 56 * 1024 * 1024

LOG2_E = math.log2(math.e)
MLA_Q_SCALE = (MLA_NOPE + MLA_ROPE) ** -0.5 * LOG2_E
SWA_Q_SCALE = SWA_HEAD_DIM ** -0.5 * LOG2_E

C_CQ = 0
C_CKV = C_CQ + Q_LORA
C_KR = C_CKV + KV_LORA
C_KS = C_KR + LANES
C_END = C_KS + SWA_KV_HEADS * SWA_HEAD_DIM

T5_LARGE_THRESHOLDS = (12, 16, 23, 32, 46, 64, 91)

NT_DIMS = (((1,), (1,)), ((), ()))
TN_DIMS = (((0,), (0,)), ((), ()))


def _resident(shape):
    nd = len(shape)
    return pl.BlockSpec(shape, lambda *_: (0,) * nd, pipeline_mode=pl.Buffered(1))


def _params(semantics):
    return pltpu.CompilerParams(dimension_semantics=semantics, vmem_limit_bytes=VMEM_LIMIT)


def _rms(x):
    return x * lax.rsqrt(jnp.mean(x * x, axis=-1, keepdims=True) + EPS)


MOD_SH1, MOD_SC1, MOD_G1, MOD_SH2, MOD_SC2, MOD_G2 = range(N_MOD)


def _mod_row(mod_ref, chunk):
    return mod_ref[pl.ds(pl.program_id(0), 1), chunk * D_MODEL:(chunk + 1) * D_MODEL]


def _ada_kernel(c_ref, w_ref, b_ref, o_ref):
    c = c_ref[...]
    c_act = c * jax.nn.sigmoid(c)
    n = c.shape[0]
    w = w_ref[...]
    c_hi = c_act.astype(BF16).astype(F32)
    lhs = jnp.concatenate([c_hi, c_act - c_hi], axis=0).astype(BF16)
    w_hi = w.astype(BF16)
    w_lo = (w - w_hi.astype(F32)).astype(BF16)
    d_hi = jnp.dot(lhs, w_hi, preferred_element_type=F32)
    d_lo = jnp.dot(lhs, w_lo, preferred_element_type=F32)
    o_ref[...] = (d_lo[n:] + d_lo[:n] + d_hi[n:]) + d_hi[:n] + b_ref[...]


def _ada_mod(c, w_ada, b_ada):
    B, D = c.shape
    N = w_ada.shape[1]
    tn = 1024
    return pl.pallas_call(
        _ada_kernel,
        out_shape=jax.ShapeDtypeStruct((B, N), F32),
        grid=(N // tn,),
        in_specs=[pl.BlockSpec((B, D), lambda j: (0, 0)),
                  pl.BlockSpec((D, tn), lambda j: (0, j)),
                  pl.BlockSpec((1, tn), lambda j: (0, j))],
        out_specs=pl.BlockSpec((B, tn), lambda j: (0, j)),
        compiler_params=_params(("arbitrary",)),
        name="ada_mod",
    )(c, w_ada, b_ada.reshape(1, N))


def _swa_bias_kernel(tab_ref, sink_ref, o_ref):
    h = pl.program_id(0)
    half = N_BUCKETS // 2
    max_exact = half // 2
    kj = lax.broadcasted_iota(jnp.int32, (BLOCK, LANES), 0)
    qi = lax.broadcasted_iota(jnp.int32, (BLOCK, LANES), 1)
    for cb in range(5):
        rel = kj - qi + (cb * LANES - 2 * WINDOW)
        n = jnp.abs(rel)
        large = jnp.full_like(n, max_exact)
        for t in T5_LARGE_THRESHOLDS:
            large = large + jnp.where(n >= t, 1, 0)
        bucket = jnp.where(rel > 0, half, 0) + jnp.where(n < max_exact, n, large)
        bias = jnp.zeros((BLOCK, LANES), F32)
        for b in range(N_BUCKETS):
            bias = jnp.where(bucket == b, tab_ref[b, h], bias)
        o_ref[0, cb] = jnp.where(n <= WINDOW, (bias - sink_ref[0, h]) * LOG2_E, NEG_INF)


def _swa_bias(rel_bias, sink):
    return pl.pallas_call(
        _swa_bias_kernel,
        out_shape=jax.ShapeDtypeStruct((SWA_HEADS, 5, BLOCK, LANES), F32),
        grid=(SWA_HEADS,),
        in_specs=[pl.BlockSpec(memory_space=pltpu.SMEM), pl.BlockSpec(memory_space=pltpu.SMEM)],
        out_specs=pl.BlockSpec((1, 5, BLOCK, LANES), lambda h: (h, 0, 0, 0)),
        compiler_params=_params(("arbitrary",)),
        name="swa_bias",
    )(rel_bias, sink.reshape(1, SWA_HEADS))


def _rope_rows(x, tab):
    x1, x2 = x[:ROPE_HALF], x[ROPE_HALF:]
    cos, sin = tab[:ROPE_HALF], tab[ROPE_HALF:]
    return jnp.concatenate([x1 * cos - x2 * sin, x2 * cos + x1 * sin], axis=0)


def _in_proj_kernel(x_ref, nm_ref, mod_ref, pos_ref, inv_ref, win_ref, wqvt_ref, qn_ref, wuqt_ref, kvn_ref,
                    wk_ref, wvt_ref, qt_ref, km_ref, vt_ref, qst_ref, ks_ref, vst_ref, *, tile):
    n_q = SWA_HEADS * SWA_HEAD_DIM
    rope_pad = jnp.zeros((MLA_QK_PAD - MLA_NOPE - MLA_ROPE, tile), F32)
    one_row = jnp.where(lax.broadcasted_iota(jnp.int32, rope_pad.shape, 0) == 0, 1.0, 0.0)
    mla_ones_rows = jnp.where(lax.broadcasted_iota(jnp.int32, (MLA_VT_ROWS - MLA_V, tile), 0) == 0,
                              1.0, 0.0).astype(BF16)
    swa_ones_rows = jnp.where(lax.broadcasted_iota(jnp.int32, (SWA_VT_ROWS - SWA_HEAD_DIM, BLOCK), 0) == 0,
                              1.0, 0.0).astype(BF16)

    def from_x(t):
        rows = slice(t * tile, (t + 1) * tile)
        h = (_rms(x_ref[0, rows, :]) * nm_ref[...] * (1.0 + _mod_row(mod_ref, MOD_SC1))
             + _mod_row(mod_ref, MOD_SH1)).astype(BF16)
        ang = inv_ref[...] * pos_ref[0, :, rows].astype(F32)
        tab = jnp.concatenate([jnp.cos(ang), jnp.sin(ang)], axis=0)

        lat = jnp.dot(h, win_ref[:, C_CQ:C_KS], preferred_element_type=F32)
        qv_t = lax.dot_general(wqvt_ref[...], h, NT_DIMS, preferred_element_type=F32)
        qst_ref[0, :, rows] = (qv_t[:n_q] * SWA_Q_SCALE).astype(BF16)
        for n in range(SWA_KV_HEADS):
            vs_t = qv_t[n_q + n * SWA_HEAD_DIM:n_q + (n + 1) * SWA_HEAD_DIM].astype(BF16)
            for j in range(tile // BLOCK):
                blk = t * (tile // BLOCK) + j
                vst_ref[0, blk, n, :SWA_HEAD_DIM, :] = vs_t[:, j * BLOCK:(j + 1) * BLOCK]
                vst_ref[0, blk, n, SWA_HEAD_DIM:, :] = swa_ones_rows
        ks_ref[0, rows, :] = jnp.dot(h, win_ref[:, C_KS:C_END], preferred_element_type=F32).astype(BF16)
        return lat, tab

    def from_latents(t, lat, tab):
        rows = slice(t * tile, (t + 1) * tile)
        cq = lat[:, C_CQ:C_CKV]
        ckv = lat[:, C_CKV:C_KR]
        kr_t = lat[:, C_KR:C_KS].T
        k_rope = jnp.concatenate([_rope_rows(kr_t[:MLA_ROPE], tab), one_row], axis=0).T.astype(BF16)

        cqn = (_rms(cq) * qn_ref[...]).astype(BF16)
        q_t = lax.dot_general(wuqt_ref[...], cqn, NT_DIMS, preferred_element_type=F32)
        for hh in range(MLA_HEADS):
            q0 = hh * MLA_QK_PAD
            qh = q_t[hh * (MLA_NOPE + MLA_ROPE):(hh + 1) * (MLA_NOPE + MLA_ROPE)]
            qt_ref[0, t, q0:q0 + MLA_NOPE, :] = (qh[:MLA_NOPE] * MLA_Q_SCALE).astype(BF16)
            q_rope = jnp.concatenate([_rope_rows(qh[MLA_NOPE:], tab) * MLA_Q_SCALE, rope_pad], axis=0)
            qt_ref[0, t, q0 + MLA_NOPE:q0 + MLA_QK_PAD, :] = q_rope.astype(BF16)

        ckvn = (_rms(ckv) * kvn_ref[...]).astype(BF16)
        kn = jnp.dot(ckvn, wk_ref[...], preferred_element_type=F32)
        for hh in range(MLA_HEADS):
            c0 = hh * MLA_QK_PAD
            km_ref[0, rows, c0:c0 + MLA_NOPE] = kn[:, hh * MLA_NOPE:(hh + 1) * MLA_NOPE].astype(BF16)
            km_ref[0, rows, c0 + MLA_NOPE:c0 + MLA_QK_PAD] = k_rope
        vt = lax.dot_general(wvt_ref[...], ckvn, NT_DIMS, preferred_element_type=F32)
        for hh in range(MLA_HEADS):
            v0 = hh * MLA_VT_ROWS
            vt_ref[0, t, v0:v0 + MLA_V, :] = vt[hh * MLA_V:(hh + 1) * MLA_V].astype(BF16)
            vt_ref[0, t, v0 + MLA_V:v0 + MLA_VT_ROWS, :] = mla_ones_rows

    tiles = range(x_ref.shape[1] // tile)
    staged = [from_x(t) for t in tiles]
    for t in tiles:
        from_latents(t, *staged[t])


def _in_proj(x, norm_mix, mod, positions, w_in_p, w_qv_t, q_norm, w_uq_t, kv_norm, w_k, w_v_t, tile, tiles_per_step):
    B, S, D = x.shape
    tm = tile * tiles_per_step
    inv_freq = ROPE_THETA ** (-jnp.arange(0, MLA_ROPE, 2, dtype=F32) / MLA_ROPE)
    kvw = SWA_KV_HEADS * SWA_HEAD_DIM
    tok = lambda w: pl.BlockSpec((1, tm, w), lambda b, i: (b, i, 0))
    tok_t = lambda r: pl.BlockSpec((1, r, tm), lambda b, i: (b, 0, i))
    tile_t = lambda r: pl.BlockSpec((1, tiles_per_step, r, tile), lambda b, i: (b, i, 0, 0))
    sds = lambda *shape: jax.ShapeDtypeStruct(shape, BF16)
    return pl.pallas_call(
        functools.partial(_in_proj_kernel, tile=tile),
        out_shape=[sds(B, S // tile, MLA_HEADS * MLA_QK_PAD, tile), sds(B, S, MLA_HEADS * MLA_QK_PAD),
                   sds(B, S // tile, MLA_HEADS * MLA_VT_ROWS, tile), sds(B, SWA_HEADS * SWA_HEAD_DIM, S),
                   sds(B, S, kvw), sds(B, S // BLOCK, SWA_KV_HEADS, SWA_VT_ROWS, BLOCK)],
        grid=(B, S // tm),
        in_specs=[tok(D), _resident((1, D)), _resident(mod.shape), tok_t(1), _resident((ROPE_HALF, 1)),
                  _resident(w_in_p.shape), _resident(w_qv_t.shape), _resident((1, Q_LORA)),
                  _resident(w_uq_t.shape), _resident((1, KV_LORA)), _resident(w_k.shape),
                  _resident(w_v_t.shape)],
        out_specs=[tile_t(MLA_HEADS * MLA_QK_PAD), tok(MLA_HEADS * MLA_QK_PAD), tile_t(MLA_HEADS * MLA_VT_ROWS),
                   tok_t(SWA_HEADS * SWA_HEAD_DIM), tok(kvw),
                   pl.BlockSpec((1, tm // BLOCK, SWA_KV_HEADS, SWA_VT_ROWS, BLOCK), lambda b, i: (b, i, 0, 0, 0))],
        compiler_params=_params(("arbitrary", "arbitrary")),
        name="in_proj",
    )(x, norm_mix.reshape(1, D), mod, positions.reshape(B, 1, S), inv_freq.reshape(ROPE_HALF, 1),
      w_in_p, w_qv_t, q_norm.reshape(1, Q_LORA), w_uq_t, kv_norm.reshape(1, KV_LORA), w_k, w_v_t)


def _mla_kernel(qt_ref, k_ref, vt_ref, o_ref, *, tk, tr, lookahead):
    n_tiles, _, tq = qt_ref.shape[1:]
    S = k_ref.shape[1]
    n_heads = k_ref.shape[2] // MLA_QK_PAD
    n_chunks = S // tk
    tiles_per_chunk = tk // tq
    g0 = MLA_SHIFT_COL // BF16_ROWS * BF16_ROWS
    row = lax.broadcasted_iota(jnp.int32, (BF16_ROWS, tq), 0)
    qk = lambda hd: slice(hd * MLA_QK_PAD, (hd + 1) * MLA_QK_PAD)

    def shifted_q(hd, i):
        qt = qt_ref[0, i, qk(hd), :]
        s = jnp.dot(k_ref[0, :tr, qk(hd)], qt, preferred_element_type=F32)
        ref = jnp.max(s, axis=0, keepdims=True)
        grp = jnp.where(row == MLA_SHIFT_COL - g0, -ref, qt[g0:g0 + BF16_ROWS].astype(F32)).astype(BF16)
        return jnp.concatenate([qt[:g0], grp, qt[g0 + BF16_ROWS:]], axis=0)

    def probs(item, qts):
        hd, i, c = item
        s = jnp.dot(k_ref[0, c * tk:(c + 1) * tk, qk(hd)], qts[hd, i], preferred_element_type=F32)
        return jnp.exp2(s).astype(BF16)

    def values_t(hd, c):
        rows = slice(hd * MLA_VT_ROWS, (hd + 1) * MLA_VT_ROWS)
        return jnp.concatenate([vt_ref[0, c * tiles_per_chunk + j, rows, :] for j in range(tiles_per_chunk)], axis=1)

    tiles = [(hd, i) for hd in range(n_heads) for i in range(n_tiles)]
    items = [(hd, i, c) for hd, i in tiles for c in range(n_chunks)]
    qts = {tiles[0]: shifted_q(*tiles[0])}
    ahead = [probs(item, qts) for item in items[:lookahead]]
    o_sum = jnp.zeros((MLA_V, tq), F32)
    for idx, (hd, i, c) in enumerate(items):
        t = idx // n_chunks
        if c == 0 and t + 1 < len(tiles):
            qts[tiles[t + 1]] = shifted_q(*tiles[t + 1])
        p = ahead.pop(0)
        if idx + lookahead < len(items):
            ahead.append(probs(items[idx + lookahead], qts))
        pv = jnp.dot(values_t(hd, c), p, preferred_element_type=F32)
        acc = pv if c == 0 else acc + pv
        if c == n_chunks - 1:
            o = acc[:MLA_V] / acc[MLA_V:MLA_V + 1]
            o_ref[0, i, hd * MLA_V:(hd + 1) * MLA_V, :] = o.astype(BF16)
            o_sum = o_sum + o
            qts.pop((hd, i))
    bad = jnp.max(jnp.where(jnp.isfinite(o_sum), 0.0, 1.0))

    @pl.when(bad > 0.0)
    def _recompute_with_running_max():
        for hd in range(n_heads):
            def tile_body(i, carry, hd=hd):
                qt = qt_ref[0, i, qk(hd), :]

                def key_tile_body(j, state):
                    m, l, acc = state
                    k = k_ref[0, pl.ds(pl.multiple_of(j * tq, tq), tq), qk(hd)]
                    s = jnp.dot(k, qt, preferred_element_type=F32)
                    m_new = jnp.maximum(m, jnp.max(s, axis=0, keepdims=True))
                    p = jnp.exp2(s - m_new)
                    alpha = jnp.exp2(m - m_new)
                    l = alpha * l + jnp.sum(p, axis=0, keepdims=True)
                    vt = vt_ref[0, j, hd * MLA_VT_ROWS:hd * MLA_VT_ROWS + MLA_V, :]
                    acc = alpha * acc + jnp.dot(vt, p.astype(BF16), preferred_element_type=F32)
                    return m_new, l, acc

                init = (jnp.full((1, tq), -jnp.inf, F32), jnp.zeros((1, tq), F32), jnp.zeros((MLA_V, tq), F32))
                _, l, acc = lax.fori_loop(0, n_tiles, key_tile_body, init)
                o_ref[0, i, hd * MLA_V:(hd + 1) * MLA_V, :] = (acc / l).astype(BF16)
                return carry

            lax.fori_loop(0, n_tiles, tile_body, 0)


def _mla_attn(qt_mla, k_mla, vt_mla, heads_per_step, tk, tr, lookahead):
    B, n_tiles, _, tq = qt_mla.shape
    S = k_mla.shape[1]
    hps = heads_per_step
    return pl.pallas_call(
        functools.partial(_mla_kernel, tk=tk, tr=tr, lookahead=lookahead),
        out_shape=jax.ShapeDtypeStruct((B, n_tiles, MLA_HEADS * MLA_V, tq), BF16),
        grid=(B, MLA_HEADS // hps),
        in_specs=[pl.BlockSpec((1, n_tiles, hps * MLA_QK_PAD, tq), lambda b, h: (b, 0, h, 0)),
                  pl.BlockSpec((1, S, hps * MLA_QK_PAD), lambda b, h: (b, 0, h)),
                  pl.BlockSpec((1, n_tiles, hps * MLA_VT_ROWS, tq), lambda b, h: (b, 0, h, 0))],
        out_specs=pl.BlockSpec((1, n_tiles, hps * MLA_V, tq), lambda b, h: (b, 0, h, 0)),
        compiler_params=_params(("arbitrary", "arbitrary")),
        name="mla_attn",
    )(qt_mla, k_mla, vt_mla)


def _swa_kernel(qt_ref, k_ref, vt_ref, t_ref, o_ref, *, sub_blocks, unit_heads, lookahead):
    S = k_ref.shape[1]
    span = BLOCK + 2 * WINDOW
    n_win = span // BLOCK
    step = pl.program_id(1)
    units = [(sb, hd0) for sb in range(sub_blocks) for hd0 in range(0, SWA_HEADS, unit_heads)]

    def window(sb):
        q0 = (step * sub_blocks + sb) * BLOCK
        start = pl.multiple_of(jnp.clip(q0 - WINDOW, 0, S - span), BLOCK)
        cb0 = jnp.where(q0 == 0, 2, jnp.where(q0 == S - BLOCK, 0, 1))
        return start, cb0

    def scores(sb, hd0):
        start, cb0 = window(sb)
        n = hd0 // SWA_GROUP
        heads = range(hd0, hd0 + unit_heads)
        qt = jnp.concatenate(
            [qt_ref[0, hd * SWA_HEAD_DIM:(hd + 1) * SWA_HEAD_DIM, sb * BLOCK:(sb + 1) * BLOCK] for hd in heads],
            axis=1)
        kw = k_ref[0, pl.ds(start, span), n * SWA_HEAD_DIM:(n + 1) * SWA_HEAD_DIM]
        bias = jnp.concatenate(
            [jnp.concatenate([t_ref[hd, cb0 + c] for c in range(n_win)], axis=0) for hd in heads], axis=1)
        return jnp.dot(kw, qt, preferred_element_type=F32) + bias

    def values_t(sb, hd0):
        start, _ = window(sb)
        blk0 = start // BLOCK
        return jnp.concatenate([vt_ref[0, blk0 + c, hd0 // SWA_GROUP] for c in range(n_win)], axis=1)

    def store(sb, hd0, o):
        for g in range(unit_heads):
            hd = hd0 + g
            o_ref[0, hd * SWA_HEAD_DIM:(hd + 1) * SWA_HEAD_DIM, sb * BLOCK:(sb + 1) * BLOCK] = (
                o[:, g * BLOCK:(g + 1) * BLOCK].astype(BF16))

    o_sum = jnp.zeros((SWA_HEAD_DIM, unit_heads * BLOCK), F32)
    probs = lambda unit: jnp.exp2(scores(*unit)).astype(BF16)
    ahead = [probs(unit) for unit in units[:lookahead]]
    for u, unit in enumerate(units):
        p = ahead.pop(0)
        if u + lookahead < len(units):
            ahead.append(probs(units[u + lookahead]))
        ov = jnp.dot(values_t(*unit), p, preferred_element_type=F32)
        o = ov[:SWA_HEAD_DIM] / (ov[SWA_HEAD_DIM:SWA_HEAD_DIM + 1] + 1.0)
        store(*unit, o)
        o_sum = o_sum + o
    bad = jnp.max(jnp.where(jnp.isfinite(o_sum), 0.0, 1.0))

    @pl.when(bad > 0.0)
    def _recompute_with_row_max():
        for unit in units:
            s = scores(*unit)
            m = jnp.maximum(jnp.max(s, axis=0, keepdims=True), 0.0)
            p = jnp.exp2(s - m)
            l = jnp.sum(p, axis=0, keepdims=True) + jnp.exp2(-m)
            ov = jnp.dot(values_t(*unit), p.astype(BF16), preferred_element_type=F32)
            store(*unit, ov[:SWA_HEAD_DIM] / l)


def _swa_attn(qs_t, ks, vs_t, bias_tab, sub_blocks, unit_heads, lookahead):
    B, W, S = qs_t.shape
    kvw = ks.shape[2]
    tq = sub_blocks * BLOCK
    return pl.pallas_call(
        functools.partial(_swa_kernel, sub_blocks=sub_blocks, unit_heads=unit_heads, lookahead=lookahead),
        out_shape=jax.ShapeDtypeStruct((B, W, S), BF16),
        grid=(B, S // tq),
        in_specs=[pl.BlockSpec((1, W, tq), lambda b, i: (b, 0, i)),
                  pl.BlockSpec((1, S, kvw), lambda b, i: (b, 0, 0)),
                  pl.BlockSpec((1, S // BLOCK, SWA_KV_HEADS, SWA_VT_ROWS, BLOCK), lambda b, i: (b, 0, 0, 0, 0)),
                  _resident(bias_tab.shape)],
        out_specs=pl.BlockSpec((1, W, tq), lambda b, i: (b, 0, i)),
        compiler_params=_params(("arbitrary", "arbitrary")),
        name="swa_attn",
    )(qs_t, ks, vs_t, bias_tab)


def _out_mlp_kernel(x_ref, oat_ref, obt_ref, mod_ref, nmix_ref, nmlp_ref, nfin_ref,
                    wg_ref, woa_ref, wob_ref, wout_ref, w1_ref, w2_ref, o_ref, *, tf, n_groups):
    tm, D = x_ref.shape[1:]
    groups = [slice(r, r + tm // n_groups) for r in range(0, tm, tm // n_groups)]

    def gated_merge(rows):
        y_a = lax.dot_general(oat_ref[0, 0, :, rows], woa_ref[...], TN_DIMS, preferred_element_type=F32)
        y_b = lax.dot_general(obt_ref[0, :, rows], wob_ref[...], TN_DIMS, preferred_element_type=F32)
        h = (_rms(x_ref[0, rows, :]) * nmix_ref[...] * (1.0 + _mod_row(mod_ref, MOD_SC1))
             + _mod_row(mod_ref, MOD_SH1)).astype(BF16)
        merged = None
        for j, y in enumerate((y_a, y_b)):
            g = jnp.dot(h, wg_ref[:, j * D:(j + 1) * D], preferred_element_type=F32)
            term = (0.5 * jnp.tanh(0.5 * g) + 0.5) * y
            merged = term if merged is None else merged + term
        return merged.astype(BF16)

    def residual(rows, merged):
        att = jnp.dot(merged, wout_ref[...], preferred_element_type=F32)
        return x_ref[0, rows, :] + _mod_row(mod_ref, MOD_G1) * att

    def mlp(rows, x1):
        h = (_rms(x1) * nmlp_ref[...] * (1.0 + _mod_row(mod_ref, MOD_SC2)) + _mod_row(mod_ref, MOD_SH2)).astype(BF16)
        ff = jnp.zeros_like(x1)
        for c in range(w1_ref.shape[1] // tf):
            a = jnp.dot(h, w1_ref[:, c * tf:(c + 1) * tf], preferred_element_type=F32)
            a = jnp.square(jnp.maximum(a, 0.0)).astype(BF16)
            ff = ff + jnp.dot(a, w2_ref[c * tf:(c + 1) * tf, :], preferred_element_type=F32)
        x2 = x1 + _mod_row(mod_ref, MOD_G2) * ff
        o_ref[0, rows, :] = _rms(x2) * nfin_ref[...]

    merged = [gated_merge(rows) for rows in groups]
    x1s = [residual(rows, m) for rows, m in zip(groups, merged)]
    for rows, x1 in zip(groups, x1s):
        mlp(rows, x1)


def _out_mlp(x, o_mla, o_swa, mod, norm_mix, norm_mlp, norm_final,
             w_g, w_o_mla, w_o_swa, w_out, w_ff1, w_ff2, tm, tf, n_groups):
    B, S, D = x.shape
    tok = lambda w: pl.BlockSpec((1, tm, w), lambda b, i: (b, i, 0))
    return pl.pallas_call(
        functools.partial(_out_mlp_kernel, tf=tf, n_groups=n_groups),
        out_shape=jax.ShapeDtypeStruct((B, S, D), F32),
        grid=(B, S // tm),
        in_specs=[tok(D), pl.BlockSpec((1, 1, D, tm), lambda b, i: (b, i, 0, 0)),
                  pl.BlockSpec((1, D, tm), lambda b, i: (b, 0, i)),
                  _resident(mod.shape), _resident((1, D)), _resident((1, D)), _resident((1, D)),
                  _resident(w_g.shape), _resident(w_o_mla.shape), _resident(w_o_swa.shape),
                  _resident(w_out.shape), _resident(w_ff1.shape), _resident(w_ff2.shape)],
        out_specs=tok(D),
        compiler_params=_params(("arbitrary", "arbitrary")),
        name="out_mlp",
    )(x, o_mla, o_swa, mod, norm_mix.reshape(1, D), norm_mlp.reshape(1, D), norm_final.reshape(1, D),
      w_g, w_o_mla, w_o_swa, w_out, w_ff1, w_ff2)


def _pack_w_in_kernel(wt_ref, tok_ref, t_ref, gate_ref):
    kr0 = Q_LORA + KV_LORA
    kr1 = kr0 + MLA_ROPE
    n_q = SWA_HEADS * SWA_HEAD_DIM
    n_kv = SWA_KV_HEADS * SWA_HEAD_DIM
    qs1 = kr1 + n_q
    ks1 = qs1 + n_kv
    vs1 = ks1 + n_kv
    t_ref[:n_q, :] = wt_ref[kr1:qs1, :].astype(BF16)
    t_ref[n_q:, :] = wt_ref[ks1:vs1, :].astype(BF16)
    tok_ref[:, :kr0] = wt_ref[:kr0, :].T.astype(BF16)
    kr = wt_ref[kr0:kr0 + LANES, :].T
    lane = lax.broadcasted_iota(jnp.int32, kr.shape, 1)
    tok_ref[:, kr0:kr0 + LANES] = jnp.where(lane < MLA_ROPE, kr, 0.0).astype(BF16)
    tok_ref[:, kr0 + LANES:] = wt_ref[qs1:ks1, :].T.astype(BF16)
    gate_ref[...] = wt_ref[vs1:, :].T.astype(BF16)


def _pack_w_in(w_in_t, cols=256):
    n_in, D = w_in_t.shape
    n_t = (SWA_HEADS + SWA_KV_HEADS) * SWA_HEAD_DIM
    n_gate = n_in - (Q_LORA + KV_LORA + MLA_ROPE + n_t + SWA_KV_HEADS * SWA_HEAD_DIM)
    return pl.pallas_call(
        _pack_w_in_kernel,
        out_shape=(jax.ShapeDtypeStruct((D, C_END), BF16), jax.ShapeDtypeStruct((n_t, D), BF16),
                   jax.ShapeDtypeStruct((D, n_gate), BF16)),
        grid=(D // cols,),
        in_specs=[pl.BlockSpec((n_in, cols), lambda i: (0, i))],
        out_specs=(pl.BlockSpec((cols, C_END), lambda i: (i, 0)), pl.BlockSpec((n_t, cols), lambda i: (0, i)),
                   pl.BlockSpec((cols, n_gate), lambda i: (i, 0))),
        compiler_params=_params(("arbitrary",)),
        name="pack_w_in",
    )(w_in_t)


def _split_heads(w, n_first):
    w3 = w.reshape(w.shape[0], MLA_HEADS, -1)
    first = w3[:, :, :n_first].reshape(w.shape[0], -1)
    second = w3[:, :, n_first:].reshape(w.shape[0], -1)
    return first.astype(BF16), second.astype(BF16)


def kernel(x, c, positions, w_ada, b_ada, norm_mix, w_in, q_norm, w_uq, kv_norm, w_ukv, rel_bias, sink,
           w_o_mla, w_o_swa, w_out, norm_mlp, w_ff1, w_ff2, norm_final):
    B, S, D = x.shape
    assert w_ada.shape[0] == 1, "single-layer block"
    assert D == D_MODEL and S % (4 * BLOCK) == 0 and S >= BLOCK + 2 * WINDOW

    mod = _ada_mod(c, w_ada[0], b_ada[0])
    bias_tab = _swa_bias(rel_bias, sink[0])

    tile = 512
    w_k, w_v = _split_heads(w_ukv[0], MLA_NOPE)
    w_in_tok, w_in_t, w_gate = _pack_w_in(w_in[0].T)
    qt_mla, k_mla, vt_mla, qs_t, ks, vs_t = _in_proj(
        x, norm_mix[0], mod, positions, w_in_tok, w_in_t, q_norm[0], w_uq[0].T.astype(BF16),
        kv_norm[0], w_k, w_v.T, tile=tile, tiles_per_step=2)

    ot_mla = _mla_attn(qt_mla, k_mla, vt_mla, heads_per_step=1, tk=1024, tr=256, lookahead=1)
    ot_swa = _swa_attn(qs_t, ks, vs_t, bias_tab, sub_blocks=8, unit_heads=2, lookahead=2)

    return _out_mlp(x, ot_mla, ot_swa, mod, norm_mix[0], norm_mlp[0], norm_final, w_gate,
                    w_o_mla[0].astype(BF16), w_o_swa[0].astype(BF16), w_out[0].astype(BF16),
                    w_ff1[0].astype(BF16), w_ff2[0].astype(BF16), tm=tile, tf=1024, n_groups=2)
```
